```python
import math
import jax, jax.numpy as jnp
from jax import lax
import numpy as np

D_MODEL = 1024
BATCH = 4
SEQ = 8192
DEPTH = 4
DEC_BATCH = 8
DEC_SEQ = 8192
PAST_LEN = 128

GRID_W = 64
N_EVEN = (DEPTH + 1) // 2
N_ODD = DEPTH // 2
D_RET = D_MODEL // 2
RET_HEADS = 4
RET_HEAD_DIM = D_RET // RET_HEADS
RET_CHUNK = 128
D_SSM = D_MODEL - D_RET
SSM_GROUP = 16
SSM_GROUPS = D_SSM // SSM_GROUP
SSM_STATE = 64
EVEN_IN = 4 * D_RET + 2 * D_SSM
D_NA = D_MODEL
NA_HEADS = 16
NA_HEAD_DIM = D_NA // NA_HEADS
NA_ROWS_MAX = 8
NA_COLS = 16
ODD_IN = 4 * D_NA
ROPE_BASE = 10000.0
EPS = 1e-6
NEG_INF = -1e30
DT_MIN = 1e-3
DT_MAX = 1e-1

kernel_name = "hybrid_retention_s5_natten_encoder"


def _rmsnorm(x, g):
    xf = x.astype(jnp.float32)
    y = xf * lax.rsqrt(jnp.mean(xf * xf, axis=-1, keepdims=True) + EPS)
    return (y * g.astype(jnp.float32)).astype(x.dtype)


def _head_norm(o):
    of = o.astype(jnp.float32)
    mu = jnp.mean(of, axis=-1, keepdims=True)
    var = jnp.mean(jnp.square(of - mu), axis=-1, keepdims=True)
    return ((of - mu) * lax.rsqrt(var + EPS)).astype(o.dtype)


def _rotary(x):
    L, dh = x.shape[1], x.shape[-1]
    inv = ROPE_BASE ** (-jnp.arange(0, dh, 2, dtype=jnp.float32) / dh)
    ang = jnp.arange(L, dtype=jnp.float32)[:, None] * inv[None, :]
    cos = jnp.cos(ang)[None, :, None, :].astype(x.dtype)
    sin = jnp.sin(ang)[None, :, None, :].astype(x.dtype)
    x1, x2 = x[..., : dh // 2], x[..., dh // 2:]
    return jnp.concatenate([x1 * cos - x2 * sin, x1 * sin + x2 * cos], axis=-1)


def _retention(q, k, v):
    B, L, H, dk = q.shape
    dv = v.shape[-1]
    dt = q.dtype
    cs = RET_CHUNK
    n = L // cs
    log_g = jnp.log1p(-jnp.exp2(-5.0 - jnp.arange(H, dtype=jnp.float32)))
    pos = jnp.arange(cs, dtype=jnp.float32)
    intra = jnp.exp(jnp.abs(pos[:, None] - pos[None, :])[None] * log_g[:, None, None]).astype(dt)
    q_fwd = jnp.exp(pos[:, None] * log_g[None]).astype(dt)[:, :, None]
    q_bwd = jnp.exp((cs - 1.0 - pos)[:, None] * log_g[None]).astype(dt)[:, :, None]
    k_fwd = jnp.exp((cs - pos)[:, None] * log_g[None]).astype(dt)[:, :, None]
    k_bwd = jnp.exp((pos + 1.0)[:, None] * log_g[None]).astype(dt)[:, :, None]
    chunk_decay = jnp.exp(cs * log_g).astype(dt)[None, :, None, None]

    qc = q.reshape(B, n, cs, H, dk)
    kc = k.reshape(B, n, cs, H, dk)
    vc = v.reshape(B, n, cs, H, dv)
    s = jnp.einsum('bnihd,bnjhd->bnhij', qc, kc) * intra
    o = jnp.einsum('bnhij,bnjhe->bnihe', s, vc)

    kv_f = jnp.einsum('bnjhd,bnjhe->nbhde', kc * k_fwd, vc)
    kv_b = jnp.einsum('bnjhd,bnjhe->nbhde', kc * k_bwd, vc)

    def step(carry, kv):
        return chunk_decay * carry + kv, carry

    init = jnp.zeros((B, H, dk, dv), dt)
    _, st_f = lax.scan(step, init, kv_f)
    _, st_b = lax.scan(step, init, kv_b, reverse=True)
    o = (o + jnp.einsum('bnihd,nbhde->bnihe', qc * q_fwd, st_f)
         + jnp.einsum('bnihd,nbhde->bnihe', qc * q_bwd, st_b))
    return o.reshape(B, L, H, dv)


def _cplx_combine(e_i, e_j):
    ar_i, ai_i, br_i, bi_i = e_i
    ar_j, ai_j, br_j, bi_j = e_j
    ar = ar_j * ar_i - ai_j * ai_i
    ai = ar_j * ai_i + ai_j * ar_i
    br = ar_j * br_i - ai_j * bi_i + br_j
    bi = ar_j * bi_i + ai_j * br_i + bi_j
    return ar, ai, br, bi


def _s5_scan(u, a_re, a_im, log_step, b_re, b_im, c_re, c_im, reverse):
    f32 = jnp.float32
    a_re = a_re.astype(f32)
    a_im = a_im.astype(f32)
    delta = jnp.exp(log_step.astype(f32))[:, None]
    z_re, z_im = a_re * delta, a_im * delta
    mag = jnp.exp(z_re)
    abar_re, abar_im = mag * jnp.cos(z_im), mag * jnp.sin(z_im)
    den = a_re * a_re + a_im * a_im
    n_re, n_im = abar_re - 1.0, abar_im
    f_re = (n_re * a_re + n_im * a_im) / den
    f_im = (n_im * a_re - n_re * a_im) / den
    b_re = b_re.astype(f32)
    b_im = b_im.astype(f32)
    bb_re = f_re[..., None] * b_re - f_im[..., None] * b_im
    bb_im = f_re[..., None] * b_im + f_im[..., None] * b_re
    bu_re = jnp.einsum('blgi,gpi->blgp', u, bb_re)
    bu_im = jnp.einsum('blgi,gpi->blgp', u, bb_im)
    shape = (1, u.shape[1]) + abar_re.shape
    elems = (jnp.broadcast_to(abar_re[None, None], shape),
             jnp.broadcast_to(abar_im[None, None], shape), bu_re, bu_im)
    _, _, x_re, x_im = lax.associative_scan(_cplx_combine, elems, reverse=reverse, axis=1)
    return (jnp.einsum('blgp,gip->blgi', x_re, c_re.astype(f32))
            - jnp.einsum('blgp,gip->blgi', x_im, c_im.astype(f32)))


def _s5(u, a_re, a_im, log_step, b_re, b_im, c_re, c_im, d_skip, w_glu):
    B, L, _ = u.shape
    dt = u.dtype
    uf = u.astype(jnp.float32)
    ug = uf.reshape(B, L, SSM_GROUPS, SSM_GROUP)
    y = (_s5_scan(ug, a_re[0], a_im[0], log_step[0], b_re[0], b_im[0], c_re[0], c_im[0], False)
         + _s5_scan(ug, a_re[1], a_im[1], log_step[1], b_re[1], b_im[1], c_re[1], c_im[1], True))
    y = y.reshape(B, L, D_SSM) + d_skip.astype(jnp.float32) * uf
    y = jax.nn.gelu(y)
    y = y * jax.nn.sigmoid(y @ w_glu.astype(jnp.float32))
    return y.astype(dt)


def _neighbourhood_attention(q, k, v, rel_bias):
    B, L, H, dh = q.shape
    rows = L // GRID_W
    kr = min(NA_ROWS_MAX, rows)
    q = q.reshape(B, rows, GRID_W, H, dh)
    k = k.reshape(B, rows, GRID_W, H, dh)
    v = v.reshape(B, rows, GRID_W, H, dh)
    r_idx = jnp.arange(rows)
    row_start = jnp.clip(r_idx - kr // 2, 0, rows - kr)
    c_idx = jnp.arange(GRID_W)
    col_start = jnp.clip(c_idx - NA_COLS // 2, 0, GRID_W - NA_COLS)
    col_valid = ((c_idx[None, :] >= col_start[:, None])
                 & (c_idx[None, :] < col_start[:, None] + NA_COLS))
    dc_idx = jnp.clip(c_idx[None, :] - c_idx[:, None] + NA_COLS - 1, 0, 2 * NA_COLS - 2)
    col_bias = jnp.take(rel_bias.astype(jnp.float32), dc_idx, axis=2)
    scale = dh ** -0.5

    def one_row(r):
        rows_k = row_start[r] + jnp.arange(kr)
        kb = jnp.take(k, rows_k, axis=1)
        vb = jnp.take(v, rows_k, axis=1)
        qr = lax.dynamic_index_in_dim(q, r, axis=1, keepdims=False)
        s = jnp.einsum('bqhd,brkhd->bhqrk', qr, kb).astype(jnp.float32) * scale
        bias = jnp.take(col_bias, rows_k - r + NA_ROWS_MAX - 1, axis=1)
        s = s + jnp.transpose(bias, (0, 2, 1, 3))[None]
        s = jnp.where(col_valid[:, None, :], s, NEG_INF)
        p = jax.nn.softmax(s.reshape(B, H, GRID_W, kr * GRID_W), axis=-1)
        p = p.reshape(B, H, GRID_W, kr, GRID_W).astype(v.dtype)
        return jnp.einsum('bhqrk,brkhd->bqhd', p, vb)

    out = lax.map(one_row, r_idx)
    return jnp.moveaxis(out, 0, 1).reshape(B, L, H * dh)


def _even_mixer(h, w_in, w_out, a_re, a_im, log_step, b_re, b_im, c_re, c_im, d_skip, w_glu):
    B, L, _ = h.shape
    z = h @ w_in
    q, k, v, ga, ub, gb = jnp.split(
        z, [D_RET, 2 * D_RET, 3 * D_RET, 4 * D_RET, 4 * D_RET + D_SSM], axis=-1)
    hs = (B, L, RET_HEADS, RET_HEAD_DIM)
    q = _rotary(q.reshape(hs))
    k = _rotary(k.reshape(hs)) * (RET_HEAD_DIM ** -0.5)
    o_a = _head_norm(_retention(q, k, v.reshape(hs))).reshape(B, L, D_RET) * jax.nn.silu(ga)
    o_b = _s5(ub, a_re, a_im, log_step, b_re, b_im, c_re, c_im, d_skip, w_glu) * jax.nn.silu(gb)
    return jnp.concatenate([o_a, o_b], axis=-1) @ w_out


def _odd_mixer(h, w_in, w_out, rel_bias):
    B, L, _ = h.shape
    q, k, v, g = jnp.split(h @ w_in, 4, axis=-1)
    hs = (B, L, NA_HEADS, NA_HEAD_DIM)
    o = _neighbourhood_attention(q.reshape(hs), k.reshape(hs), v.reshape(hs), rel_bias)
    return (o * jax.nn.silu(g)) @ w_out


def _trunk(x, c, norm_pre, norm_post, w_mod, b_mod, w_in_ab, w_out_ab,
           ssm_a_re, ssm_a_im, ssm_log_step, ssm_b_re, ssm_b_im, ssm_c_re, ssm_c_im,
           ssm_d, ssm_w_glu, w_in_c, w_out_c, na_rel_bias):
    for i in range(DEPTH):
        mod = jax.nn.silu(c) @ w_mod[i] + b_mod[i]
        shift, scale, gate = jnp.split(mod[:, None, :], 3, axis=-1)
        h = _rmsnorm(x, norm_pre[i]) * (1.0 + scale) + shift
        j = i // 2
        if i % 2 == 0:
            y = _even_mixer(h, w_in_ab[j], w_out_ab[j], ssm_a_re[j], ssm_a_im[j],
                            ssm_log_step[j], ssm_b_re[j], ssm_b_im[j], ssm_c_re[j],
                            ssm_c_im[j], ssm_d[j], ssm_w_glu[j])
        else:
            y = _odd_mixer(h, w_in_c[j], w_out_c[j], na_rel_bias[j])
        x = x + gate * _rmsnorm(y, norm_post[i])
    return x


def setup_inputs(seed: int = 0) -> dict:
    key = jax.random.key(seed)
    ks = jax.random.split(key, 24)
    f32 = jnp.float32
    nrm = lambda k, s, sc: jax.random.normal(k, s, f32) * sc
    G, P, Gi = SSM_GROUPS, SSM_STATE, SSM_GROUP
    return {
        "x_prompt": nrm(ks[0], (BATCH, SEQ, D_MODEL), 1.0),
        "x_sample": nrm(ks[1], (DEC_BATCH, DEC_SEQ, D_MODEL), 1.0),
        "c_prompt": nrm(ks[2], (BATCH, D_MODEL), 1.0),
        "c_sample": nrm(ks[3], (DEC_BATCH, D_MODEL), 1.0),
        "norm_pre": 1.0 + nrm(ks[4], (DEPTH, D_MODEL), 0.05),
        "norm_post": 1.0 + nrm(ks[5], (DEPTH, D_MODEL), 0.05),
        "w_mod": nrm(ks[6], (DEPTH, D_MODEL, 3 * D_MODEL), 0.5 * D_MODEL ** -0.5),
        "b_mod": nrm(ks[7], (DEPTH, 3 * D_MODEL), 0.02),
        "w_in_ab": nrm(ks[8], (N_EVEN, D_MODEL, EVEN_IN), D_MODEL ** -0.5),
        "w_out_ab": nrm(ks[9], (N_EVEN, D_RET + D_SSM, D_MODEL), (D_RET + D_SSM) ** -0.5),
        "ssm_a_re": -0.5 + nrm(ks[10], (N_EVEN, 2, G, P), 0.01),
        "ssm_a_im": math.pi * jnp.arange(P, dtype=f32) + nrm(ks[11], (N_EVEN, 2, G, P), 0.01),
        "ssm_log_step": jax.random.uniform(ks[12], (N_EVEN, 2, G), f32,
                                           math.log(DT_MIN), math.log(DT_MAX)),
        "ssm_b_re": nrm(ks[13], (N_EVEN, 2, G, P, Gi), (2 * Gi) ** -0.5),
        "ssm_b_im": nrm(ks[14], (N_EVEN, 2, G, P, Gi), (2 * Gi) ** -0.5),
        "ssm_c_re": nrm(ks[15], (N_EVEN, 2, G, Gi, P), P ** -0.5),
        "ssm_c_im": nrm(ks[16], (N_EVEN, 2, G, Gi, P), P ** -0.5),
        "ssm_d": nrm(ks[17], (N_EVEN, D_SSM), 0.5),
        "ssm_w_glu": nrm(ks[18], (N_EVEN, D_SSM, D_SSM), D_SSM ** -0.5),
        "w_in_c": nrm(ks[19], (N_ODD, D_MODEL, ODD_IN), D_MODEL ** -0.5),
        "w_out_c": nrm(ks[20], (N_ODD, D_NA, D_MODEL), D_NA ** -0.5),
        "na_rel_bias": nrm(ks[21], (N_ODD, NA_HEADS, 2 * NA_ROWS_MAX - 1, 2 * NA_COLS - 1), 0.1),
    }


def reference(x_prompt, x_sample, c_prompt, c_sample, norm_pre, norm_post, w_mod, b_mod,
              w_in_ab, w_out_ab, ssm_a_re, ssm_a_im, ssm_log_step, ssm_b_re, ssm_b_im,
              ssm_c_re, ssm_c_im, ssm_d, ssm_w_glu, w_in_c, w_out_c, na_rel_bias):
    y_prompt = _trunk(x_prompt, c_prompt, norm_pre, norm_post, w_mod, b_mod, w_in_ab, w_out_ab,
                      ssm_a_re, ssm_a_im, ssm_log_step, ssm_b_re, ssm_b_im, ssm_c_re, ssm_c_im,
                      ssm_d, ssm_w_glu, w_in_c, w_out_c, na_rel_bias)
    y_sample = _trunk(x_sample, c_sample, norm_pre, norm_post, w_mod, b_mod, w_in_ab, w_out_ab,
                      ssm_a_re, ssm_a_im, ssm_log_step, ssm_b_re, ssm_b_im, ssm_c_re, ssm_c_im,
                      ssm_d, ssm_w_glu, w_in_c, w_out_c, na_rel_bias)
    return (y_prompt, y_sample)
```

```python
import functools
import math

import jax
import jax.numpy as jnp
from jax import lax
from jax.experimental import pallas as pl
from jax.experimental.pallas import tpu as pltpu

F32 = jnp.float32
BF16 = jnp.bfloat16

D_MODEL = 1024
DEPTH = 4
GRID_W = 64
D_RET = 512
RET_HEADS = 4
RET_HEAD_DIM = 128
RET_CHUNK = 128
D_SSM = 512
SSM_GROUP = 16
SSM_GROUPS = 32
SSM_STATE = 64
EVEN_IN = 4 * D_RET + 2 * D_SSM
NA_HEADS = 16
NA_HEAD_DIM = 64
NA_ROWS = 8
NA_COLS = 16
ODD_IN = 4 * D_MODEL
ROPE_BASE = 10000.0
EPS = 1e-6
NEG_INF = -1e30

S5_CHUNK = 16
S5_ROWS = 8
TOKEN_TILE = 512
NA_ROWS_PER_STEP = 8
VMEM_LIMIT = 52 * 1024 * 1024


def _cparams(*sem):
    return pltpu.CompilerParams(dimension_semantics=sem, vmem_limit_bytes=VMEM_LIMIT)


def _sigmoid(x):
    return 1.0 / (1.0 + jnp.exp(-x))


def _silu(x):
    return x * _sigmoid(x)


def _gelu_tanh(x):
    c = math.sqrt(2.0 / math.pi)
    return 0.5 * x * (1.0 + jnp.tanh(c * (x + 0.044715 * (x * x * x))))


def _mod_kernel(c_ref, w_ref, b_ref, o_ref):
    a = _silu(c_ref[...])
    o_ref[...] = jnp.dot(a, w_ref[...], preferred_element_type=F32) + b_ref[...]


def _modulation(c, w_mod, b_mod):
    B = c.shape[0]
    tn = 1024
    return pl.pallas_call(
        _mod_kernel,
        grid=(DEPTH, 3 * D_MODEL // tn),
        in_specs=[
            pl.BlockSpec((B, D_MODEL), lambda i, n: (0, 0)),
            pl.BlockSpec((None, D_MODEL, tn), lambda i, n: (i, 0, n)),
            pl.BlockSpec((None, 1, tn), lambda i, n: (i, 0, n)),
        ],
        out_specs=pl.BlockSpec((None, B, tn), lambda i, n: (i, 0, n)),
        out_shape=jax.ShapeDtypeStruct((DEPTH, B, 3 * D_MODEL), F32),
        compiler_params=_cparams("arbitrary", "arbitrary"),
    )(c, w_mod, b_mod.reshape(DEPTH, 1, 3 * D_MODEL))


def _prenorm(x_ref, mod_ref, g_ref):
    x = x_ref[...]
    y = x * lax.rsqrt(jnp.mean(x * x, axis=-1, keepdims=True) + EPS) * g_ref[...]
    return (y * (1.0 + mod_ref[1:2, :]) + mod_ref[0:1, :]).astype(BF16)


def _even_in_kernel(x_ref, mod_ref, g_ref, w_ref, cos_ref, sin_ref,
                    q_ref, k_ref, v_ref, ga_ref, u_ref, gb_ref):
    h = _prenorm(x_ref, mod_ref, g_ref)
    cos2, sin2 = cos_ref[...], sin_ref[...]

    def proj(c):
        return jnp.dot(h, w_ref[:, c * D_RET:(c + 1) * D_RET], preferred_element_type=F32)

    def rotary(z):
        cols = []
        for hd in range(RET_HEADS):
            zh = z[:, hd * RET_HEAD_DIM:(hd + 1) * RET_HEAD_DIM]
            cols.append(zh * cos2 + pltpu.roll(zh, RET_HEAD_DIM // 2, 1) * sin2)
        return jnp.concatenate(cols, axis=1)

    q_ref[...] = rotary(proj(0)).astype(BF16)
    k_ref[...] = (rotary(proj(1)) * (RET_HEAD_DIM ** -0.5)).astype(BF16)
    v_ref[...] = proj(2).astype(BF16)
    ga_ref[...] = proj(3)
    u_ref[...] = proj(4)
    gb_ref[...] = proj(5)


def _even_in(x, mod3, g_pre, w_in, cos2, sin2):
    B, L, D = x.shape
    tm = TOKEN_TILE
    tok = lambda w: pl.BlockSpec((None, tm, w), lambda b, t: (b, t, 0))
    out = lambda dt: jax.ShapeDtypeStruct((B, L, D_RET), dt)
    return pl.pallas_call(
        _even_in_kernel,
        grid=(B, L // tm),
        in_specs=[
            tok(D),
            pl.BlockSpec((None, 3, D), lambda b, t: (b, 0, 0)),
            pl.BlockSpec((1, D), lambda b, t: (0, 0)),
            pl.BlockSpec((D, EVEN_IN), lambda b, t: (0, 0)),
            pl.BlockSpec((tm, RET_HEAD_DIM), lambda b, t: (t, 0)),
            pl.BlockSpec((tm, RET_HEAD_DIM), lambda b, t: (t, 0)),
        ],
        out_specs=[tok(D_RET)] * 6,
        out_shape=[out(BF16), out(BF16), out(BF16), out(F32), out(F32), out(F32)],
        compiler_params=_cparams("parallel", "parallel"),
    )(x, mod3, g_pre, w_in, cos2, sin2)


def _odd_in_kernel(x_ref, mod_ref, g_ref, w_ref, q_ref, k_ref, v_ref, gate_ref):
    h = _prenorm(x_ref, mod_ref, g_ref)
    half = D_MODEL // 2

    def proj(c):
        return jnp.dot(h, w_ref[:, c * half:(c + 1) * half], preferred_element_type=F32)

    for c in range(2):
        sl = slice(c * half, (c + 1) * half)
        q_ref[:, sl] = (proj(c) * (NA_HEAD_DIM ** -0.5)).astype(BF16)
        k_ref[:, sl] = proj(2 + c).astype(BF16)
        v_ref[:, sl] = proj(4 + c).astype(BF16)
        gate_ref[:, sl] = proj(6 + c)


def _odd_in(x, mod3, g_pre, w_in):
    B, L, D = x.shape
    tm = TOKEN_TILE
    tok = pl.BlockSpec((None, tm, D), lambda b, t: (b, t, 0))
    out = lambda dt: jax.ShapeDtypeStruct((B, L, D), dt)
    return pl.pallas_call(
        _odd_in_kernel,
        grid=(B, L // tm),
        in_specs=[
            tok,
            pl.BlockSpec((None, 3, D), lambda b, t: (b, 0, 0)),
            pl.BlockSpec((1, D), lambda b, t: (0, 0)),
            pl.BlockSpec((D, ODD_IN), lambda b, t: (0, 0)),
        ],
        out_specs=[tok] * 4,
        out_shape=[out(BF16), out(BF16), out(BF16), out(F32)],
        compiler_params=_cparams("parallel", "parallel"),
    )(x, mod3, g_pre, w_in)


def _ret_kernel(q_ref, k_ref, v_ref, ga_ref, intra_ref, qf_ref, qb_ref, kf_ref, kb_ref, dec_ref,
                o_ref, stb_ref):
    cs = RET_CHUNK
    n = q_ref.shape[0] // cs
    dec = dec_ref[...]
    tdot = lambda a, b: lax.dot_general(a, b, (((0,), (0,)), ((), ())), preferred_element_type=F32)

    def rows(c):
        return pl.ds(pl.multiple_of(c * cs, cs), cs)

    def rev(i, st):
        c = n - 1 - i
        stb_ref[c] = st.astype(BF16)
        kb = (k_ref[rows(c), :].astype(F32) * kb_ref[...]).astype(BF16)
        return dec * st + tdot(kb, v_ref[rows(c), :])

    lax.fori_loop(0, n, rev, jnp.zeros((RET_HEAD_DIM, RET_HEAD_DIM), F32))

    def fwd(c, st):
        q = q_ref[rows(c), :]
        k = k_ref[rows(c), :]
        v = v_ref[rows(c), :]
        qf32, kf32 = q.astype(F32), k.astype(F32)
        s = lax.dot_general(q, k, (((1,), (1,)), ((), ())), preferred_element_type=F32) * intra_ref[...]
        o = jnp.dot(s.astype(BF16), v, preferred_element_type=F32)
        o += jnp.dot((qf32 * qf_ref[...]).astype(BF16), st.astype(BF16), preferred_element_type=F32)
        o += jnp.dot((qf32 * qb_ref[...]).astype(BF16), stb_ref[c], preferred_element_type=F32)
        mu = jnp.mean(o, axis=-1, keepdims=True)
        d = o - mu
        hn = d * lax.rsqrt(jnp.mean(d * d, axis=-1, keepdims=True) + EPS)
        o_ref[rows(c), :] = (hn * _silu(ga_ref[rows(c), :])).astype(BF16)
        return dec * st + tdot((kf32 * kf_ref[...]).astype(BF16), v)

    lax.fori_loop(0, n, fwd, jnp.zeros((RET_HEAD_DIM, RET_HEAD_DIM), F32))


def _retention_tables():
    H, cs, dk = RET_HEADS, RET_CHUNK, RET_HEAD_DIM
    log_g = jnp.log1p(-jnp.exp2(-5.0 - jnp.arange(H, dtype=F32)))
    pos = jnp.arange(cs, dtype=F32)
    intra = jnp.exp(jnp.abs(pos[:, None] - pos[None, :])[None] * log_g[:, None, None])
    col = lambda e: jnp.broadcast_to(jnp.exp(e[:, None] * log_g[None]).T[:, :, None], (H, cs, dk))
    q_fwd, q_bwd = col(pos), col(cs - 1.0 - pos)
    k_fwd, k_bwd = col(cs - pos), col(pos + 1.0)
    decay = jnp.broadcast_to(jnp.exp(cs * log_g)[:, None, None], (H, dk, dk))
    return intra, q_fwd, q_bwd, k_fwd, k_bwd, decay


def _retention(q, k, v, ga):
    B, L, _ = q.shape
    dk = RET_HEAD_DIM
    seq = pl.BlockSpec((None, L, dk), lambda b, h: (b, 0, h))
    tab = pl.BlockSpec((None, RET_CHUNK, dk), lambda b, h: (h, 0, 0))
    return pl.pallas_call(
        _ret_kernel,
        grid=(B, RET_HEADS),
        in_specs=[seq, seq, seq, seq] + [tab] * 6,
        out_specs=seq,
        out_shape=jax.ShapeDtypeStruct((B, L, D_RET), BF16),
        scratch_shapes=[pltpu.VMEM((L // RET_CHUNK, dk, dk), BF16)],
        compiler_params=_cparams("parallel", "parallel"),
    )(q, k, v, ga, *_retention_tables())


def _s5_kernel(x_ref, wcat_ref, wout_ref, are_ref, aim_ref, y_ref, kvre_ref, kvim_ref, h_ref):
    R = x_ref.shape[0]
    nb = S5_ROWS
    C = R // nb
    blk = min(R, 512)
    TI = S5_CHUNK * SSM_GROUP
    P = SSM_STATE

    def mm1(i, carry):
        r = pl.ds(pl.multiple_of(i * blk, blk), blk)
        res = jnp.dot(x_ref[r, :], wcat_ref[...], preferred_element_type=F32)
        y_ref[r, :] = res[:, :TI]
        kvre_ref[r, :] = res[:, TI:TI + 2 * P]
        kvim_ref[r, :] = res[:, TI + 2 * P:]
        return carry

    lax.fori_loop(0, R // blk, mm1, 0)

    are, aim = are_ref[...], aim_ref[...]

    def step(k, carry):
        sf_re, sf_im, sb_re, sb_im = carry
        rf = pl.ds(pl.multiple_of(k * nb, nb), nb)
        rb = pl.ds(pl.multiple_of((C - 1 - k) * nb, nb), nb)
        h_ref[rf, 0:P] = sf_re[:, 0:P]
        h_ref[rf, 2 * P:3 * P] = sf_im[:, 0:P]
        h_ref[rb, P:2 * P] = sb_re[:, P:2 * P]
        h_ref[rb, 3 * P:4 * P] = sb_im[:, P:2 * P]
        nf_re = are * sf_re - aim * sf_im + kvre_ref[rf, :]
        nf_im = are * sf_im + aim * sf_re + kvim_ref[rf, :]
        nb_re = are * sb_re - aim * sb_im + kvre_ref[rb, :]
        nb_im = are * sb_im + aim * sb_re + kvim_ref[rb, :]
        return nf_re, nf_im, nb_re, nb_im

    z = jnp.zeros((nb, 2 * P), F32)
    lax.fori_loop(0, C, step, (z, z, z, z))

    def mm2(i, carry):
        r = pl.ds(pl.multiple_of(i * blk, blk), blk)
        y_ref[r, :] += jnp.dot(h_ref[r, :].astype(BF16), wout_ref[...], preferred_element_type=F32)
        return carry

    lax.fori_loop(0, R // blk, mm2, 0)


def _s5_operators(a_re, a_im, log_step, b_re, b_im, c_re, c_im):
    T, G, P, Gi = S5_CHUNK, SSM_GROUPS, SSM_STATE, SSM_GROUP
    hp = lax.Precision.HIGHEST
    a_re, a_im = a_re.astype(F32), a_im.astype(F32)
    delta = jnp.exp(log_step.astype(F32))[..., None]
    z_re, z_im = a_re * delta, a_im * delta
    mag = jnp.exp(z_re)
    abar_re, abar_im = mag * jnp.cos(z_im), mag * jnp.sin(z_im)
    den = a_re * a_re + a_im * a_im
    n_re, n_im = abar_re - 1.0, abar_im
    f_re = (n_re * a_re + n_im * a_im) / den
    f_im = (n_im * a_re - n_re * a_im) / den
    b_re, b_im = b_re.astype(F32), b_im.astype(F32)
    bb_re = f_re[..., None] * b_re - f_im[..., None] * b_im
    bb_im = f_re[..., None] * b_im + f_im[..., None] * b_re
    c_re, c_im = c_re.astype(F32), c_im.astype(F32)

    pr, pi = [jnp.ones_like(abar_re)], [jnp.zeros_like(abar_re)]
    for _ in range(T):
        pr, pi = (pr + [pr[-1] * abar_re - pi[-1] * abar_im],
                  pi + [pr[-1] * abar_im + pi[-1] * abar_re])
    pw_re, pw_im = jnp.stack(pr), jnp.stack(pi)

    w_re = pw_re[:T, ..., None] * bb_re[None] - pw_im[:T, ..., None] * bb_im[None]
    w_im = pw_re[:T, ..., None] * bb_im[None] + pw_im[:T, ..., None] * bb_re[None]
    kern = (jnp.einsum('xgip,dxgpj->dxgij', c_re, w_re, precision=hp)
            - jnp.einsum('xgip,dxgpj->dxgij', c_im, w_im, precision=hp))
    s_idx = jnp.arange(T)[:, None]
    t_idx = jnp.arange(T)[None, :]
    kf = jnp.where((t_idx >= s_idx)[..., None, None, None],
                   kern[jnp.clip(t_idx - s_idx, 0, T - 1), 0], 0.0)
    kb = jnp.where((s_idx >= t_idx)[..., None, None, None],
                   kern[jnp.clip(s_idx - t_idx, 0, T - 1), 1], 0.0)
    m_intra = jnp.transpose(kf + kb, (2, 0, 4, 1, 3)).reshape(G, T * Gi, T * Gi)

    def w_in(w, direction, order):
        sel = w[order, direction]
        return jnp.transpose(sel, (1, 0, 3, 2)).reshape(G, T * Gi, P)

    fwd_order = jnp.arange(T - 1, -1, -1)
    bwd_order = jnp.arange(T)
    wcat = jnp.concatenate([
        m_intra,
        w_in(w_re, 0, fwd_order), w_in(w_re, 1, bwd_order),
        w_in(w_im, 0, fwd_order), w_in(w_im, 1, bwd_order)], axis=-1)

    def w_out(direction, order):
        qr, qi = pw_re[order, direction], pw_im[order, direction]
        cr, ci = c_re[direction], c_im[direction]
        wr = cr[None] * qr[:, :, None, :] - ci[None] * qi[:, :, None, :]
        wi = cr[None] * qi[:, :, None, :] + ci[None] * qr[:, :, None, :]
        fix = lambda w: jnp.transpose(w, (1, 3, 0, 2)).reshape(G, P, T * Gi)
        return fix(wr), fix(-wi)

    of_re, of_im = w_out(0, jnp.arange(1, T + 1))
    ob_re, ob_im = w_out(1, jnp.arange(T, 0, -1))
    wout = jnp.concatenate([of_re, ob_re, of_im, ob_im], axis=1)

    lanes = lambda p: jnp.broadcast_to(
        jnp.concatenate([p[T, 0], p[T, 1]], axis=-1)[:, None, :], (G, S5_ROWS, 2 * P))
    return wcat.astype(BF16), wout.astype(BF16), lanes(pw_re), lanes(pw_im)


def _s5_scan(u, ops):
    wcat, wout, are, aim = ops
    B, L, _ = u.shape
    T, G, Gi = S5_CHUNK, SSM_GROUPS, SSM_GROUP
    C, nb = L // T, S5_ROWS
    assert B <= nb
    R = C * nb
    TI = T * Gi
    x = jnp.transpose(u.reshape(B, C, T, G, Gi), (3, 1, 0, 2, 4)).astype(BF16)
    if B < nb:
        x = jnp.pad(x, ((0, 0), (0, 0), (0, nb - B), (0, 0), (0, 0)))
    x = x.reshape(G, R, TI)
    grp = lambda r, c: pl.BlockSpec((None, r, c), lambda g: (g, 0, 0))
    y = pl.pallas_call(
        _s5_kernel,
        grid=(G,),
        in_specs=[grp(R, TI), grp(TI, 2 * TI), grp(TI, TI), grp(nb, 128), grp(nb, 128)],
        out_specs=grp(R, TI),
        out_shape=jax.ShapeDtypeStruct((G, R, TI), F32),
        scratch_shapes=[pltpu.VMEM((R, 128), F32), pltpu.VMEM((R, 128), F32),
                        pltpu.VMEM((R, TI), F32)],
        compiler_params=_cparams("parallel"),
    )(x, wcat, wout, are, aim)
    y = y.reshape(G, C, nb, T, Gi)[:, :, :B]
    return jnp.transpose(y, (2, 1, 3, 0, 4)).reshape(B, L, G * Gi)


def _na_kernel(q_ref, k_ref, v_ref, g_ref, bias_ref, o_ref):
    W, kr, dh = GRID_W, NA_ROWS, NA_HEAD_DIM
    n_rows = k_ref.shape[0] // W
    j = pl.program_id(2)
    lane = lax.broadcasted_iota(jnp.int32, (W, 2 * dh), 1)
    first = lane < dh

    def one_row(i, carry):
        r = j * NA_ROWS_PER_STEP + i
        rs = jnp.clip(r - kr // 2, 0, n_rows - kr)
        var = rs - r + (NA_ROWS - 1)
        keys = pl.ds(pl.multiple_of(rs * W, W), kr * W)
        kblk, vblk = k_ref[keys, :], v_ref[keys, :]
        qrow = pl.ds(pl.multiple_of(i * W, W), W)
        q2 = q_ref[qrow, :]
        outs = []
        for hh in range(2):
            qm = jnp.where(first if hh == 0 else ~first, q2, jnp.zeros_like(q2))
            s = lax.dot_general(qm, kblk, (((1,), (1,)), ((), ())), preferred_element_type=F32)
            s = s + bias_ref[var, hh]
            p = jnp.exp(s - jnp.max(s, axis=-1, keepdims=True))
            den = jnp.sum(p, axis=-1, keepdims=True)
            outs.append(jnp.dot(p.astype(BF16), vblk, preferred_element_type=F32) / den)
        o = jnp.where(first, outs[0], outs[1])
        o_ref[qrow, :] = (o * _silu(g_ref[qrow, :])).astype(BF16)
        return carry

    lax.fori_loop(0, NA_ROWS_PER_STEP, one_row, 0)


def _na_bias_table(rel_bias):
    W, R = GRID_W, NA_ROWS
    c_idx = jnp.arange(W)
    col_start = jnp.clip(c_idx - NA_COLS // 2, 0, W - NA_COLS)
    col_valid = ((c_idx[None, :] >= col_start[:, None])
                 & (c_idx[None, :] < col_start[:, None] + NA_COLS))
    dc_idx = jnp.clip(c_idx[None, :] - c_idx[:, None] + NA_COLS - 1, 0, 2 * NA_COLS - 2)
    col_bias = jnp.take(rel_bias.astype(F32), dc_idx, axis=2)
    col_bias = jnp.where(col_valid[None, None], col_bias, NEG_INF)
    win = jnp.arange(R)[:, None] + jnp.arange(R)[None, :]
    tab = col_bias[:, win]
    return jnp.transpose(tab, (1, 0, 3, 2, 4)).reshape(R, NA_HEADS, W, R * W)


def _neighbourhood_attention(q, k, v, g, bias_tab):
    B, L, D = q.shape
    rows = L // GRID_W
    assert rows >= NA_ROWS and rows % NA_ROWS_PER_STEP == 0
    tq = NA_ROWS_PER_STEP * GRID_W
    lanes = 2 * NA_HEAD_DIM
    blk = pl.BlockSpec((None, tq, lanes), lambda b, h, j: (b, j, h))
    seq = pl.BlockSpec((None, L, lanes), lambda b, h, j: (b, 0, h))
    return pl.pallas_call(
        _na_kernel,
        grid=(B, NA_HEADS // 2, rows // NA_ROWS_PER_STEP),
        in_specs=[blk, seq, seq, blk,
                  pl.BlockSpec((NA_ROWS, 2, GRID_W, NA_ROWS * GRID_W), lambda b, h, j: (0, h, 0, 0))],
        out_specs=blk,
        out_shape=jax.ShapeDtypeStruct((B, L, D), BF16),
        compiler_params=_cparams("parallel", "parallel", "arbitrary"),
    )(q, k, v, g, bias_tab)


def _finish(y, x_ref, mod_ref, g_ref, o_ref):
    yn = y * lax.rsqrt(jnp.mean(y * y, axis=-1, keepdims=True) + EPS) * g_ref[...]
    o_ref[...] = x_ref[...] + mod_ref[2:3, :] * yn


def _even_out_kernel(oa_ref, ys_ref, u_ref, gb_ref, x_ref, mod_ref, g_ref, d_ref, wglu_ref, wout_ref,
                     o_ref):
    y = _gelu_tanh(ys_ref[...] + d_ref[...] * u_ref[...])
    y = y * _sigmoid(jnp.dot(y.astype(BF16), wglu_ref[...], preferred_element_type=F32))
    ob = (y * _silu(gb_ref[...])).astype(BF16)
    out = jnp.dot(oa_ref[...], wout_ref[:D_RET, :], preferred_element_type=F32)
    out += jnp.dot(ob, wout_ref[D_RET:, :], preferred_element_type=F32)
    _finish(out, x_ref, mod_ref, g_ref, o_ref)


def _even_out(oa, ys, u, gb, x, mod3, g_post, d_skip, w_glu, w_out):
    B, L, D = x.shape
    tm = TOKEN_TILE
    tok = lambda w: pl.BlockSpec((None, tm, w), lambda b, t: (b, t, 0))
    full = lambda r, c: pl.BlockSpec((r, c), lambda b, t: (0, 0))
    return pl.pallas_call(
        _even_out_kernel,
        grid=(B, L // tm),
        in_specs=[tok(D_RET), tok(D_SSM), tok(D_SSM), tok(D_SSM), tok(D),
                  pl.BlockSpec((None, 3, D), lambda b, t: (b, 0, 0)),
                  full(1, D), full(1, D_SSM), full(D_SSM, D_SSM), full(D, D)],
        out_specs=tok(D),
        out_shape=jax.ShapeDtypeStruct((B, L, D), F32),
        compiler_params=_cparams("parallel", "parallel"),
    )(oa, ys, u, gb, x, mod3, g_post, d_skip, w_glu, w_out)


def _odd_out_kernel(o_ref_in, x_ref, mod_ref, g_ref, wout_ref, o_ref):
    out = jnp.dot(o_ref_in[...], wout_ref[...], preferred_element_type=F32)
    _finish(out, x_ref, mod_ref, g_ref, o_ref)


def _odd_out(o, x, mod3, g_post, w_out):
    B, L, D = x.shape
    tm = TOKEN_TILE
    tok = pl.BlockSpec((None, tm, D), lambda b, t: (b, t, 0))
    return pl.pallas_call(
        _odd_out_kernel,
        grid=(B, L // tm),
        in_specs=[tok, tok,
                  pl.BlockSpec((None, 3, D), lambda b, t: (b, 0, 0)),
                  pl.BlockSpec((1, D), lambda b, t: (0, 0)),
                  pl.BlockSpec((D, D), lambda b, t: (0, 0))],
        out_specs=tok,
        out_shape=jax.ShapeDtypeStruct((B, L, D), F32),
        compiler_params=_cparams("parallel", "parallel"),
    )(o, x, mod3, g_post, w_out)


def _rotary_tables(L):
    dh = RET_HEAD_DIM
    inv = ROPE_BASE ** (-jnp.arange(0, dh, 2, dtype=F32) / dh)
    ang = jnp.arange(L, dtype=F32)[:, None] * inv[None, :]
    cos, sin = jnp.cos(ang), jnp.sin(ang)
    return jnp.concatenate([cos, cos], axis=1), jnp.concatenate([-sin, sin], axis=1)


def _trunk(x, c, p):
    B, L, D = x.shape
    mods = _modulation(c, p["w_mod"], p["b_mod"]).reshape(DEPTH, B, 3, D)
    cos2, sin2 = _rotary_tables(L)
    for i in range(DEPTH):
        j = i // 2
        g_pre = p["norm_pre"][i].reshape(1, D)
        g_post = p["norm_post"][i].reshape(1, D)
        if i % 2 == 0:
            q, k, v, ga, u, gb = _even_in(x, mods[i], g_pre, p["w_in_ab"][j], cos2, sin2)
            oa = _retention(q, k, v, ga)
            ys = _s5_scan(u, p["s5_ops"][j])
            x = _even_out(oa, ys, u, gb, x, mods[i], g_post, p["ssm_d"][j].reshape(1, D_SSM),
                          p["ssm_w_glu"][j], p["w_out_ab"][j])
        else:
            q, k, v, g = _odd_in(x, mods[i], g_pre, p["w_in_c"][j])
            o = _neighbourhood_attention(q, k, v, g, p["na_bias"][j])
            x = _odd_out(o, x, mods[i], g_post, p["w_out_c"][j])
    return x


def kernel(x_prompt, x_sample, c_prompt, c_sample, norm_pre, norm_post, w_mod, b_mod, w_in_ab, w_out_ab, ssm_a_re, ssm_a_im, ssm_log_step, ssm_b_re, ssm_b_im, ssm_c_re, ssm_c_im, ssm_d, ssm_w_glu, w_in_c, w_out_c, na_rel_bias):
    n_even, n_odd = w_in_ab.shape[0], w_in_c.shape[0]
    p = {
        "norm_pre": norm_pre, "norm_post": norm_post, "w_mod": w_mod, "b_mod": b_mod,
        "w_in_ab": w_in_ab.astype(BF16), "w_out_ab": w_out_ab.astype(BF16),
        "ssm_d": ssm_d, "ssm_w_glu": ssm_w_glu.astype(BF16),
        "w_in_c": w_in_c.astype(BF16), "w_out_c": w_out_c.astype(BF16),
        "s5_ops": [_s5_operators(ssm_a_re[j], ssm_a_im[j], ssm_log_step[j], ssm_b_re[j], ssm_b_im[j],
                                 ssm_c_re[j], ssm_c_im[j]) for j in range(n_even)],
        "na_bias": [_na_bias_table(na_rel_bias[j]) for j in range(n_odd)],
    }
    return _trunk(x_prompt, c_prompt, p), _trunk(x_sample, c_sample, p)
```

```python
import functools
import math

import jax
import jax.numpy as jnp
from jax import lax
from jax.experimental import pallas as pl
from jax.experimental.pallas import tpu as pltpu

F32 = jnp.float32
BF16 = jnp.bfloat16

D_MODEL = 1024
DEPTH = 4
GRID_W = 64
D_RET = 512
RET_HEADS = 4
RET_HEAD_DIM = 128
RET_CHUNK = 128
D_SSM = 512
SSM_GROUP = 16
SSM_GROUPS = 32
SSM_STATE = 64
EVEN_IN = 4 * D_RET + 2 * D_SSM
NA_HEADS = 16
NA_HEAD_DIM = 64
NA_ROWS = 8
NA_COLS = 16
ODD_IN = 4 * D_MODEL
ROPE_BASE = 10000.0
EPS = 1e-6
NEG_INF = -1e30

S5_CHUNK = 16
S5_ROWS = 8
TOKEN_TILE = 512
NA_ROWS_PER_STEP = 8
RET_UNROLL = 4
VMEM_LIMIT = 52 * 1024 * 1024


def _cparams(*sem):
    return pltpu.CompilerParams(dimension_semantics=sem, vmem_limit_bytes=VMEM_LIMIT)


def _sigmoid(x):
    return 1.0 / (1.0 + jnp.exp(-x))


def _silu(x):
    return x * _sigmoid(x)


def _gelu_tanh(x):
    c = math.sqrt(2.0 / math.pi)
    return 0.5 * x * (1.0 + jnp.tanh(c * (x + 0.044715 * (x * x * x))))


def _mod_kernel(c_ref, w_ref, b_ref, o_ref):
    a = _silu(c_ref[...])
    o_ref[...] = jnp.dot(a, w_ref[...], preferred_element_type=F32) + b_ref[...]


def _modulation(c, w_mod, b_mod):
    B = c.shape[0]
    tn = 1024
    return pl.pallas_call(
        _mod_kernel,
        grid=(DEPTH, 3 * D_MODEL // tn),
        in_specs=[
            pl.BlockSpec((B, D_MODEL), lambda i, n: (0, 0)),
            pl.BlockSpec((None, D_MODEL, tn), lambda i, n: (i, 0, n)),
            pl.BlockSpec((None, 1, tn), lambda i, n: (i, 0, n)),
        ],
        out_specs=pl.BlockSpec((None, B, tn), lambda i, n: (i, 0, n)),
        out_shape=jax.ShapeDtypeStruct((DEPTH, B, 3 * D_MODEL), F32),
        compiler_params=_cparams("arbitrary", "arbitrary"),
    )(c, w_mod, b_mod.reshape(DEPTH, 1, 3 * D_MODEL))


def _prenorm(x_ref, mod_ref, g_ref):
    x = x_ref[...]
    y = x * lax.rsqrt(jnp.mean(x * x, axis=-1, keepdims=True) + EPS) * g_ref[...]
    return (y * (1.0 + mod_ref[1:2, :]) + mod_ref[0:1, :]).astype(BF16)


def _even_in_kernel(x_ref, mod_ref, g_ref, w_ref, cos_ref, sin_ref,
                    q_ref, k_ref, v_ref, ga_ref, u_ref, gb_ref):
    h = _prenorm(x_ref, mod_ref, g_ref)
    cos2, sin2 = cos_ref[...], sin_ref[...]

    def proj(c):
        return jnp.dot(h, w_ref[:, c * D_RET:(c + 1) * D_RET], preferred_element_type=F32)

    def rotary(z):
        cols = []
        for hd in range(RET_HEADS):
            zh = z[:, hd * RET_HEAD_DIM:(hd + 1) * RET_HEAD_DIM]
            cols.append(zh * cos2 + pltpu.roll(zh, RET_HEAD_DIM // 2, 1) * sin2)
        return jnp.concatenate(cols, axis=1)

    q_ref[...] = rotary(proj(0)).astype(BF16)
    k_ref[...] = (rotary(proj(1)) * (RET_HEAD_DIM ** -0.5)).astype(BF16)
    v_ref[...] = proj(2).astype(BF16)
    ga_ref[...] = proj(3)
    u_ref[...] = proj(4)
    gb_ref[...] = proj(5)


def _even_in(x, mod3, g_pre, w_in, cos2, sin2):
    B, L, D = x.shape
    tm = TOKEN_TILE
    tok = lambda w: pl.BlockSpec((None, tm, w), lambda b, t: (b, t, 0))
    out = lambda dt: jax.ShapeDtypeStruct((B, L, D_RET), dt)
    return pl.pallas_call(
        _even_in_kernel,
        grid=(B, L // tm),
        in_specs=[
            tok(D),
            pl.BlockSpec((None, 3, D), lambda b, t: (b, 0, 0)),
            pl.BlockSpec((1, D), lambda b, t: (0, 0)),
            pl.BlockSpec((D, EVEN_IN), lambda b, t: (0, 0)),
            pl.BlockSpec((tm, RET_HEAD_DIM), lambda b, t: (t, 0)),
            pl.BlockSpec((tm, RET_HEAD_DIM), lambda b, t: (t, 0)),
        ],
        out_specs=[tok(D_RET)] * 6,
        out_shape=[out(BF16), out(BF16), out(BF16), out(F32), out(F32), out(F32)],
        compiler_params=_cparams("parallel", "parallel"),
    )(x, mod3, g_pre, w_in, cos2, sin2)


def _odd_in_kernel(x_ref, mod_ref, g_ref, w_ref, q_ref, k_ref, v_ref, gate_ref):
    h = _prenorm(x_ref, mod_ref, g_ref)
    half = D_MODEL // 2

    def proj(c):
        return jnp.dot(h, w_ref[:, c * half:(c + 1) * half], preferred_element_type=F32)

    for c in range(2):
        sl = slice(c * half, (c + 1) * half)
        q_ref[:, sl] = (proj(c) * (NA_HEAD_DIM ** -0.5)).astype(BF16)
        k_ref[:, sl] = proj(2 + c).astype(BF16)
        v_ref[:, sl] = proj(4 + c).astype(BF16)
        gate_ref[:, sl] = proj(6 + c)


def _odd_in(x, mod3, g_pre, w_in):
    B, L, D = x.shape
    tm = TOKEN_TILE
    tok = pl.BlockSpec((None, tm, D), lambda b, t: (b, t, 0))
    out = lambda dt: jax.ShapeDtypeStruct((B, L, D), dt)
    return pl.pallas_call(
        _odd_in_kernel,
        grid=(B, L // tm),
        in_specs=[
            tok,
            pl.BlockSpec((None, 3, D), lambda b, t: (b, 0, 0)),
            pl.BlockSpec((1, D), lambda b, t: (0, 0)),
            pl.BlockSpec((D, ODD_IN), lambda b, t: (0, 0)),
        ],
        out_specs=[tok] * 4,
        out_shape=[out(BF16), out(BF16), out(BF16), out(F32)],
        compiler_params=_cparams("parallel", "parallel"),
    )(x, mod3, g_pre, w_in)


def _ret_kernel(q_ref, k_ref, v_ref, ga_ref, intra_ref, qf_ref, qb_ref, kf_ref, kb_ref, dec_ref,
                o_ref, stb_ref):
    cs = RET_CHUNK
    n = q_ref.shape[0] // cs
    dec = dec_ref[...]
    tdot = lambda a, b: lax.dot_general(a, b, (((0,), (0,)), ((), ())), preferred_element_type=F32)

    def rows(c):
        return pl.ds(pl.multiple_of(c * cs, cs), cs)

    def rev(i, st):
        c = n - 1 - i
        stb_ref[c] = st.astype(BF16)
        kb = (k_ref[rows(c), :].astype(F32) * kb_ref[...]).astype(BF16)
        return dec * st + tdot(kb, v_ref[rows(c), :])

    lax.fori_loop(0, n, rev, jnp.zeros((RET_HEAD_DIM, RET_HEAD_DIM), F32), unroll=RET_UNROLL)

    def fwd(c, st):
        q = q_ref[rows(c), :]
        k = k_ref[rows(c), :]
        v = v_ref[rows(c), :]
        qf32, kf32 = q.astype(F32), k.astype(F32)
        s = lax.dot_general(q, k, (((1,), (1,)), ((), ())), preferred_element_type=F32) * intra_ref[...]
        o = jnp.dot(s.astype(BF16), v, preferred_element_type=F32)
        o += jnp.dot((qf32 * qf_ref[...]).astype(BF16), st.astype(BF16), preferred_element_type=F32)
        o += jnp.dot((qf32 * qb_ref[...]).astype(BF16), stb_ref[c], preferred_element_type=F32)
        mu = jnp.mean(o, axis=-1, keepdims=True)
        d = o - mu
        hn = d * lax.rsqrt(jnp.mean(d * d, axis=-1, keepdims=True) + EPS)
        o_ref[rows(c), :] = (hn * _silu(ga_ref[rows(c), :])).astype(BF16)
        return dec * st + tdot((kf32 * kf_ref[...]).astype(BF16), v)

    lax.fori_loop(0, n, fwd, jnp.zeros((RET_HEAD_DIM, RET_HEAD_DIM), F32), unroll=RET_UNROLL)


def _retention_tables():
    H, cs, dk = RET_HEADS, RET_CHUNK, RET_HEAD_DIM
    log_g = jnp.log1p(-jnp.exp2(-5.0 - jnp.arange(H, dtype=F32)))
    pos = jnp.arange(cs, dtype=F32)
    intra = jnp.exp(jnp.abs(pos[:, None] - pos[None, :])[None] * log_g[:, None, None])
    col = lambda e: jnp.broadcast_to(jnp.exp(e[:, None] * log_g[None]).T[:, :, None], (H, cs, dk))
    q_fwd, q_bwd = col(pos), col(cs - 1.0 - pos)
    k_fwd, k_bwd = col(cs - pos), col(pos + 1.0)
    decay = jnp.broadcast_to(jnp.exp(cs * log_g)[:, None, None], (H, dk, dk))
    return intra, q_fwd, q_bwd, k_fwd, k_bwd, decay


def _retention(q, k, v, ga):
    B, L, _ = q.shape
    dk = RET_HEAD_DIM
    seq = pl.BlockSpec((None, L, dk), lambda b, h: (b, 0, h))
    tab = pl.BlockSpec((None, RET_CHUNK, dk), lambda b, h: (h, 0, 0))
    return pl.pallas_call(
        _ret_kernel,
        grid=(B, RET_HEADS),
        in_specs=[seq, seq, seq, seq] + [tab] * 6,
        out_specs=seq,
        out_shape=jax.ShapeDtypeStruct((B, L, D_RET), BF16),
        scratch_shapes=[pltpu.VMEM((L // RET_CHUNK, dk, dk), BF16)],
        compiler_params=_cparams("parallel", "parallel"),
    )(q, k, v, ga, *_retention_tables())


def _s5_kernel(x_ref, wcat_ref, wout_ref, are_ref, aim_ref, y_ref, kvre_ref, kvim_ref, h_ref):
    R = x_ref.shape[0]
    nb = S5_ROWS
    C = R // nb
    blk = min(R, 512)
    TI = S5_CHUNK * SSM_GROUP
    P = SSM_STATE

    def mm1(i, carry):
        r = pl.ds(pl.multiple_of(i * blk, blk), blk)
        res = jnp.dot(x_ref[r, :], wcat_ref[...], preferred_element_type=F32)
        y_ref[r, :] = res[:, :TI]
        kvre_ref[r, :] = res[:, TI:TI + 2 * P]
        kvim_ref[r, :] = res[:, TI + 2 * P:]
        return carry

    lax.fori_loop(0, R // blk, mm1, 0)

    are, aim = are_ref[...], aim_ref[...]

    def step(k, carry):
        sf_re, sf_im, sb_re, sb_im = carry
        rf = pl.ds(pl.multiple_of(k * nb, nb), nb)
        rb = pl.ds(pl.multiple_of((C - 1 - k) * nb, nb), nb)
        h_ref[rf, 0:P] = sf_re[:, 0:P]
        h_ref[rf, 2 * P:3 * P] = sf_im[:, 0:P]
        h_ref[rb, P:2 * P] = sb_re[:, P:2 * P]
        h_ref[rb, 3 * P:4 * P] = sb_im[:, P:2 * P]
        nf_re = are * sf_re - aim * sf_im + kvre_ref[rf, :]
        nf_im = are * sf_im + aim * sf_re + kvim_ref[rf, :]
        nb_re = are * sb_re - aim * sb_im + kvre_ref[rb, :]
        nb_im = are * sb_im + aim * sb_re + kvim_ref[rb, :]
        return nf_re, nf_im, nb_re, nb_im

    z = jnp.zeros((nb, 2 * P), F32)
    lax.fori_loop(0, C, step, (z, z, z, z))

    def mm2(i, carry):
        r = pl.ds(pl.multiple_of(i * blk, blk), blk)
        y_ref[r, :] += jnp.dot(h_ref[r, :].astype(BF16), wout_ref[...], preferred_element_type=F32)
        return carry

    lax.fori_loop(0, R // blk, mm2, 0)


def _s5_operators(a_re, a_im, log_step, b_re, b_im, c_re, c_im):
    T, G, P, Gi = S5_CHUNK, SSM_GROUPS, SSM_STATE, SSM_GROUP
    hp = lax.Precision.HIGHEST
    a_re, a_im = a_re.astype(F32), a_im.astype(F32)
    delta = jnp.exp(log_step.astype(F32))[..., None]
    z_re, z_im = a_re * delta, a_im * delta
    mag = jnp.exp(z_re)
    abar_re, abar_im = mag * jnp.cos(z_im), mag * jnp.sin(z_im)
    den = a_re * a_re + a_im * a_im
    n_re, n_im = abar_re - 1.0, abar_im
    f_re = (n_re * a_re + n_im * a_im) / den
    f_im = (n_im * a_re - n_re * a_im) / den
    b_re, b_im = b_re.astype(F32), b_im.astype(F32)
    bb_re = f_re[..., None] * b_re - f_im[..., None] * b_im
    bb_im = f_re[..., None] * b_im + f_im[..., None] * b_re
    c_re, c_im = c_re.astype(F32), c_im.astype(F32)

    pr, pi = [jnp.ones_like(abar_re)], [jnp.zeros_like(abar_re)]
    for _ in range(T):
        pr, pi = (pr + [pr[-1] * abar_re - pi[-1] * abar_im],
                  pi + [pr[-1] * abar_im + pi[-1] * abar_re])
    pw_re, pw_im = jnp.stack(pr), jnp.stack(pi)

    w_re = pw_re[:T, ..., None] * bb_re[None] - pw_im[:T, ..., None] * bb_im[None]
    w_im = pw_re[:T, ..., None] * bb_im[None] + pw_im[:T, ..., None] * bb_re[None]
    kern = (jnp.einsum('xgip,dxgpj->dxgij', c_re, w_re, precision=hp)
            - jnp.einsum('xgip,dxgpj->dxgij', c_im, w_im, precision=hp))
    s_idx = jnp.arange(T)[:, None]
    t_idx = jnp.arange(T)[None, :]
    kf = jnp.where((t_idx >= s_idx)[..., None, None, None],
                   kern[jnp.clip(t_idx - s_idx, 0, T - 1), 0], 0.0)
    kb = jnp.where((s_idx >= t_idx)[..., None, None, None],
                   kern[jnp.clip(s_idx - t_idx, 0, T - 1), 1], 0.0)
    m_intra = jnp.transpose(kf + kb, (2, 0, 4, 1, 3)).reshape(G, T * Gi, T * Gi)

    def w_in(w, direction, order):
        sel = w[order, direction]
        return jnp.transpose(sel, (1, 0, 3, 2)).reshape(G, T * Gi, P)

    fwd_order = jnp.arange(T - 1, -1, -1)
    bwd_order = jnp.arange(T)
    wcat = jnp.concatenate([
        m_intra,
        w_in(w_re, 0, fwd_order), w_in(w_re, 1, bwd_order),
        w_in(w_im, 0, fwd_order), w_in(w_im, 1, bwd_order)], axis=-1)

    def w_out(direction, order):
        qr, qi = pw_re[order, direction], pw_im[order, direction]
        cr, ci = c_re[direction], c_im[direction]
        wr = cr[None] * qr[:, :, None, :] - ci[None] * qi[:, :, None, :]
        wi = cr[None] * qi[:, :, None, :] + ci[None] * qr[:, :, None, :]
        fix = lambda w: jnp.transpose(w, (1, 3, 0, 2)).reshape(G, P, T * Gi)
        return fix(wr), fix(-wi)

    of_re, of_im = w_out(0, jnp.arange(1, T + 1))
    ob_re, ob_im = w_out(1, jnp.arange(T, 0, -1))
    wout = jnp.concatenate([of_re, ob_re, of_im, ob_im], axis=1)

    lanes = lambda p: jnp.broadcast_to(
        jnp.concatenate([p[T, 0], p[T, 1]], axis=-1)[:, None, :], (G, S5_ROWS, 2 * P))
    return wcat.astype(BF16), wout.astype(BF16), lanes(pw_re), lanes(pw_im)


def _s5_scan(u, ops):
    wcat, wout, are, aim = ops
    B, L, _ = u.shape
    T, G, Gi = S5_CHUNK, SSM_GROUPS, SSM_GROUP
    C, nb = L // T, S5_ROWS
    assert B <= nb
    R = C * nb
    TI = T * Gi
    x = jnp.transpose(u.reshape(B, C, T, G, Gi), (3, 1, 0, 2, 4)).astype(BF16)
    if B < nb:
        x = jnp.pad(x, ((0, 0), (0, 0), (0, nb - B), (0, 0), (0, 0)))
    x = x.reshape(G, R, TI)
    grp = lambda r, c: pl.BlockSpec((None, r, c), lambda g: (g, 0, 0))
    y = pl.pallas_call(
        _s5_kernel,
        grid=(G,),
        in_specs=[grp(R, TI), grp(TI, 2 * TI), grp(TI, TI), grp(nb, 128), grp(nb, 128)],
        out_specs=grp(R, TI),
        out_shape=jax.ShapeDtypeStruct((G, R, TI), F32),
        scratch_shapes=[pltpu.VMEM((R, 128), F32), pltpu.VMEM((R, 128), F32),
                        pltpu.VMEM((R, TI), F32)],
        compiler_params=_cparams("parallel"),
    )(x, wcat, wout, are, aim)
    y = y.reshape(G, C, nb, T, Gi)[:, :, :B]
    return jnp.transpose(y, (2, 1, 3, 0, 4)).reshape(B, L, G * Gi)


def _na_kernel(q_ref, k_ref, v_ref, g_ref, bias_ref, o_ref):
    W, kr, dh = GRID_W, NA_ROWS, NA_HEAD_DIM
    n_rows = k_ref.shape[0] // W
    j = pl.program_id(2)
    first = lax.broadcasted_iota(jnp.int32, (W, 2 * dh), 1) < dh
    for i in range(NA_ROWS_PER_STEP):
        r = j * NA_ROWS_PER_STEP + i
        rs = jnp.clip(r - kr // 2, 0, n_rows - kr)
        var = rs - r + (NA_ROWS - 1)
        keys = pl.ds(pl.multiple_of(rs * W, W), kr * W)
        kblk, vblk = k_ref[keys, :], v_ref[keys, :]
        qrow = slice(i * W, (i + 1) * W)
        q2 = q_ref[qrow, :]
        zero = jnp.zeros_like(q2)
        qm = jnp.concatenate([jnp.where(first, q2, zero), jnp.where(first, zero, q2)], axis=0)
        s = lax.dot_general(qm, kblk, (((1,), (1,)), ((), ())), preferred_element_type=F32)
        s = s + bias_ref[var]
        p = jnp.exp(s - jnp.max(s, axis=-1, keepdims=True))
        den = jnp.sum(p, axis=-1, keepdims=True)
        o2 = jnp.dot(p.astype(BF16), vblk, preferred_element_type=F32) / den
        o = jnp.where(first, o2[:W], o2[W:])
        o_ref[qrow, :] = (o * _silu(g_ref[qrow, :])).astype(BF16)


def _na_bias_table(rel_bias):
    W, R = GRID_W, NA_ROWS
    c_idx = jnp.arange(W)
    col_start = jnp.clip(c_idx - NA_COLS // 2, 0, W - NA_COLS)
    col_valid = ((c_idx[None, :] >= col_start[:, None])
                 & (c_idx[None, :] < col_start[:, None] + NA_COLS))
    dc_idx = jnp.clip(c_idx[None, :] - c_idx[:, None] + NA_COLS - 1, 0, 2 * NA_COLS - 2)
    col_bias = jnp.take(rel_bias.astype(F32), dc_idx, axis=2)
    col_bias = jnp.where(col_valid[None, None], col_bias, NEG_INF)
    win = jnp.arange(R)[:, None] + jnp.arange(R)[None, :]
    tab = col_bias[:, win]
    return jnp.transpose(tab, (1, 0, 3, 2, 4)).reshape(R, NA_HEADS // 2, 2 * W, R * W)


def _neighbourhood_attention(q, k, v, g, bias_tab):
    B, L, D = q.shape
    rows = L // GRID_W
    assert rows >= NA_ROWS and rows % NA_ROWS_PER_STEP == 0
    tq = NA_ROWS_PER_STEP * GRID_W
    lanes = 2 * NA_HEAD_DIM
    blk = pl.BlockSpec((None, tq, lanes), lambda b, h, j: (b, j, h))
    seq = pl.BlockSpec((None, L, lanes), lambda b, h, j: (b, 0, h))
    return pl.pallas_call(
        _na_kernel,
        grid=(B, NA_HEADS // 2, rows // NA_ROWS_PER_STEP),
        in_specs=[blk, seq, seq, blk,
                  pl.BlockSpec((NA_ROWS, None, 2 * GRID_W, NA_ROWS * GRID_W),
                               lambda b, h, j: (0, h, 0, 0))],
        out_specs=blk,
        out_shape=jax.ShapeDtypeStruct((B, L, D), BF16),
        compiler_params=_cparams("parallel", "parallel", "arbitrary"),
    )(q, k, v, g, bias_tab)


def _finish(y, x_ref, mod_ref, g_ref, o_ref):
    yn = y * lax.rsqrt(jnp.mean(y * y, axis=-1, keepdims=True) + EPS) * g_ref[...]
    o_ref[...] = x_ref[...] + mod_ref[2:3, :] * yn


def _even_out_kernel(oa_ref, ys_ref, u_ref, gb_ref, x_ref, mod_ref, g_ref, d_ref, wglu_ref, wout_ref,
                     o_ref):
    y = _gelu_tanh(ys_ref[...] + d_ref[...] * u_ref[...])
    y = y * _sigmoid(jnp.dot(y.astype(BF16), wglu_ref[...], preferred_element_type=F32))
    ob = (y * _silu(gb_ref[...])).astype(BF16)
    out = jnp.dot(oa_ref[...], wout_ref[:D_RET, :], preferred_element_type=F32)
    out += jnp.dot(ob, wout_ref[D_RET:, :], preferred_element_type=F32)
    _finish(out, x_ref, mod_ref, g_ref, o_ref)


def _even_out(oa, ys, u, gb, x, mod3, g_post, d_skip, w_glu, w_out):
    B, L, D = x.shape
    tm = TOKEN_TILE
    tok = lambda w: pl.BlockSpec((None, tm, w), lambda b, t: (b, t, 0))
    full = lambda r, c: pl.BlockSpec((r, c), lambda b, t: (0, 0))
    return pl.pallas_call(
        _even_out_kernel,
        grid=(B, L // tm),
        in_specs=[tok(D_RET), tok(D_SSM), tok(D_SSM), tok(D_SSM), tok(D),
                  pl.BlockSpec((None, 3, D), lambda b, t: (b, 0, 0)),
                  full(1, D), full(1, D_SSM), full(D_SSM, D_SSM), full(D, D)],
        out_specs=tok(D),
        out_shape=jax.ShapeDtypeStruct((B, L, D), F32),
        compiler_params=_cparams("parallel", "parallel"),
    )(oa, ys, u, gb, x, mod3, g_post, d_skip, w_glu, w_out)


def _odd_out_kernel(o_ref_in, x_ref, mod_ref, g_ref, wout_ref, o_ref):
    out = jnp.dot(o_ref_in[...], wout_ref[...], preferred_element_type=F32)
    _finish(out, x_ref, mod_ref, g_ref, o_ref)


def _odd_out(o, x, mod3, g_post, w_out):
    B, L, D = x.shape
    tm = TOKEN_TILE
    tok = pl.BlockSpec((None, tm, D), lambda b, t: (b, t, 0))
    return pl.pallas_call(
        _odd_out_kernel,
        grid=(B, L // tm),
        in_specs=[tok, tok,
                  pl.BlockSpec((None, 3, D), lambda b, t: (b, 0, 0)),
                  pl.BlockSpec((1, D), lambda b, t: (0, 0)),
                  pl.BlockSpec((D, D), lambda b, t: (0, 0))],
        out_specs=tok,
        out_shape=jax.ShapeDtypeStruct((B, L, D), F32),
        compiler_params=_cparams("parallel", "parallel"),
    )(o, x, mod3, g_post, w_out)


def _rotary_tables(L):
    dh = RET_HEAD_DIM
    inv = ROPE_BASE ** (-jnp.arange(0, dh, 2, dtype=F32) / dh)
    ang = jnp.arange(L, dtype=F32)[:, None] * inv[None, :]
    cos, sin = jnp.cos(ang), jnp.sin(ang)
    return jnp.concatenate([cos, cos], axis=1), jnp.concatenate([-sin, sin], axis=1)


def _trunk(x, c, p):
    B, L, D = x.shape
    mods = _modulation(c, p["w_mod"], p["b_mod"]).reshape(DEPTH, B, 3, D)
    cos2, sin2 = _rotary_tables(L)
    for i in range(DEPTH):
        j = i // 2
        g_pre = p["norm_pre"][i].reshape(1, D)
        g_post = p["norm_post"][i].reshape(1, D)
        if i % 2 == 0:
            q, k, v, ga, u, gb = _even_in(x, mods[i], g_pre, p["w_in_ab"][j], cos2, sin2)
            oa = _retention(q, k, v, ga)
            ys = _s5_scan(u, p["s5_ops"][j])
            x = _even_out(oa, ys, u, gb, x, mods[i], g_post, p["ssm_d"][j].reshape(1, D_SSM),
                          p["ssm_w_glu"][j], p["w_out_ab"][j])
        else:
            q, k, v, g = _odd_in(x, mods[i], g_pre, p["w_in_c"][j])
            o = _neighbourhood_attention(q, k, v, g, p["na_bias"][j])
            x = _odd_out(o, x, mods[i], g_post, p["w_out_c"][j])
    return x


def kernel(x_prompt, x_sample, c_prompt, c_sample, norm_pre, norm_post, w_mod, b_mod, w_in_ab, w_out_ab, ssm_a_re, ssm_a_im, ssm_log_step, ssm_b_re, ssm_b_im, ssm_c_re, ssm_c_im, ssm_d, ssm_w_glu, w_in_c, w_out_c, na_rel_bias):
    n_even, n_odd = w_in_ab.shape[0], w_in_c.shape[0]
    p = {
        "norm_pre": norm_pre, "norm_post": norm_post, "w_mod": w_mod, "b_mod": b_mod,
        "w_in_ab": w_in_ab.astype(BF16), "w_out_ab": w_out_ab.astype(BF16),
        "ssm_d": ssm_d, "ssm_w_glu": ssm_w_glu.astype(BF16),
        "w_in_c": w_in_c.astype(BF16), "w_out_c": w_out_c.astype(BF16),
        "s5_ops": [_s5_operators(ssm_a_re[j], ssm_a_im[j], ssm_log_step[j], ssm_b_re[j], ssm_b_im[j],
                                 ssm_c_re[j], ssm_c_im[j]) for j in range(n_even)],
        "na_bias": [_na_bias_table(na_rel_bias[j]) for j in range(n_odd)],
    }
    return _trunk(x_prompt, c_prompt, p), _trunk(x_sample, c_sample, p)
```

```python
import functools
import math

import jax
import jax.numpy as jnp
from jax import lax
from jax.experimental import pallas as pl
from jax.experimental.pallas import tpu as pltpu

F32 = jnp.float32
BF16 = jnp.bfloat16

D_MODEL = 1024
DEPTH = 4
GRID_W = 64
D_RET = 512
RET_HEADS = 4
RET_HEAD_DIM = 128
RET_CHUNK = 128
D_SSM = 512
SSM_GROUP = 16
SSM_GROUPS = 32
SSM_STATE = 64
EVEN_IN = 4 * D_RET + 2 * D_SSM
NA_HEADS = 16
NA_HEAD_DIM = 64
NA_ROWS = 8
NA_COLS = 16
ODD_IN = 4 * D_MODEL
ROPE_BASE = 10000.0
EPS = 1e-6
NEG_INF = -1e30

S5_CHUNK = 16
S5_GROUPS_PER_STEP = 8
S5_PITCH_PAD = 8
S5_SCAN_UNROLL = 4
TOKEN_TILE = 512
NA_ROWS_PER_STEP = 8
RET_UNROLL = 4
VMEM_LIMIT = 52 * 1024 * 1024


def _cparams(*sem):
    return pltpu.CompilerParams(dimension_semantics=sem, vmem_limit_bytes=VMEM_LIMIT)


def _sigmoid(x):
    return 1.0 / (1.0 + jnp.exp(-x))


def _silu(x):
    return x * _sigmoid(x)


def _gelu_tanh(x):
    c = math.sqrt(2.0 / math.pi)
    return 0.5 * x * (1.0 + jnp.tanh(c * (x + 0.044715 * (x * x * x))))


def _mod_kernel(c_ref, w_ref, b_ref, o_ref):
    a = _silu(c_ref[...])
    o_ref[...] = jnp.dot(a, w_ref[...], preferred_element_type=F32) + b_ref[...]


def _modulation(c, w_mod, b_mod):
    B = c.shape[0]
    tn = 1024
    return pl.pallas_call(
        _mod_kernel,
        grid=(DEPTH, 3 * D_MODEL // tn),
        in_specs=[
            pl.BlockSpec((B, D_MODEL), lambda i, n: (0, 0)),
            pl.BlockSpec((None, D_MODEL, tn), lambda i, n: (i, 0, n)),
            pl.BlockSpec((None, 1, tn), lambda i, n: (i, 0, n)),
        ],
        out_specs=pl.BlockSpec((None, B, tn), lambda i, n: (i, 0, n)),
        out_shape=jax.ShapeDtypeStruct((DEPTH, B, 3 * D_MODEL), F32),
        compiler_params=_cparams("arbitrary", "arbitrary"),
    )(c, w_mod, b_mod.reshape(DEPTH, 1, 3 * D_MODEL))


def _prenorm(x_ref, mod_ref, g_ref):
    x = x_ref[...]
    y = x * lax.rsqrt(jnp.mean(x * x, axis=-1, keepdims=True) + EPS) * g_ref[...]
    return (y * (1.0 + mod_ref[1:2, :]) + mod_ref[0:1, :]).astype(BF16)


def _even_in_kernel(x_ref, mod_ref, g_ref, w_ref, cos_ref, sin_ref,
                    q_ref, k_ref, v_ref, ga_ref, u_ref, gb_ref):
    h = _prenorm(x_ref, mod_ref, g_ref)
    cos2, sin2 = cos_ref[...], sin_ref[...]

    def proj(c):
        return jnp.dot(h, w_ref[:, c * D_RET:(c + 1) * D_RET], preferred_element_type=F32)

    def rotary(z):
        cols = []
        for hd in range(RET_HEADS):
            zh = z[:, hd * RET_HEAD_DIM:(hd + 1) * RET_HEAD_DIM]
            cols.append(zh * cos2 + pltpu.roll(zh, RET_HEAD_DIM // 2, 1) * sin2)
        return jnp.concatenate(cols, axis=1)

    q_ref[...] = rotary(proj(0)).astype(BF16)
    k_ref[...] = (rotary(proj(1)) * (RET_HEAD_DIM ** -0.5)).astype(BF16)
    v_ref[...] = proj(2).astype(BF16)
    ga_ref[...] = proj(3)
    u_ref[...] = proj(4)
    gb_ref[...] = proj(5)


def _even_in(x, mod3, g_pre, w_in, cos2, sin2):
    B, L, D = x.shape
    tm = TOKEN_TILE
    tok = lambda w: pl.BlockSpec((None, tm, w), lambda b, t: (b, t, 0))
    out = lambda dt: jax.ShapeDtypeStruct((B, L, D_RET), dt)
    return pl.pallas_call(
        _even_in_kernel,
        grid=(B, L // tm),
        in_specs=[
            tok(D),
            pl.BlockSpec((None, 3, D), lambda b, t: (b, 0, 0)),
            pl.BlockSpec((1, D), lambda b, t: (0, 0)),
            pl.BlockSpec((D, EVEN_IN), lambda b, t: (0, 0)),
            pl.BlockSpec((tm, RET_HEAD_DIM), lambda b, t: (t, 0)),
            pl.BlockSpec((tm, RET_HEAD_DIM), lambda b, t: (t, 0)),
        ],
        out_specs=[tok(D_RET)] * 6,
        out_shape=[out(BF16), out(BF16), out(BF16), out(F32), out(F32), out(F32)],
        compiler_params=_cparams("parallel", "parallel"),
    )(x, mod3, g_pre, w_in, cos2, sin2)


def _odd_in_kernel(x_ref, mod_ref, g_ref, w_ref, q_ref, k_ref, v_ref, gate_ref):
    h = _prenorm(x_ref, mod_ref, g_ref)
    half = D_MODEL // 2

    def proj(c):
        return jnp.dot(h, w_ref[:, c * half:(c + 1) * half], preferred_element_type=F32)

    for c in range(2):
        sl = slice(c * half, (c + 1) * half)
        q_ref[:, sl] = (proj(c) * (NA_HEAD_DIM ** -0.5)).astype(BF16)
        k_ref[:, sl] = proj(2 + c).astype(BF16)
        v_ref[:, sl] = proj(4 + c).astype(BF16)
        gate_ref[:, sl] = proj(6 + c)


def _odd_in(x, mod3, g_pre, w_in):
    B, L, D = x.shape
    tm = TOKEN_TILE
    tok = pl.BlockSpec((None, tm, D), lambda b, t: (b, t, 0))
    out = lambda dt: jax.ShapeDtypeStruct((B, L, D), dt)
    return pl.pallas_call(
        _odd_in_kernel,
        grid=(B, L // tm),
        in_specs=[
            tok,
            pl.BlockSpec((None, 3, D), lambda b, t: (b, 0, 0)),
            pl.BlockSpec((1, D), lambda b, t: (0, 0)),
            pl.BlockSpec((D, ODD_IN), lambda b, t: (0, 0)),
        ],
        out_specs=[tok] * 4,
        out_shape=[out(BF16), out(BF16), out(BF16), out(F32)],
        compiler_params=_cparams("parallel", "parallel"),
    )(x, mod3, g_pre, w_in)


def _ret_kernel(q_ref, k_ref, v_ref, ga_ref, intra_ref, qf_ref, qb_ref, kf_ref, kb_ref, dec_ref,
                o_ref, stb_ref):
    cs = RET_CHUNK
    n = q_ref.shape[0] // cs
    dec = dec_ref[...]
    tdot = lambda a, b: lax.dot_general(a, b, (((0,), (0,)), ((), ())), preferred_element_type=F32)

    def rows(c):
        return pl.ds(pl.multiple_of(c * cs, cs), cs)

    def rev(i, st):
        c = n - 1 - i
        stb_ref[c] = st.astype(BF16)
        kb = (k_ref[rows(c), :].astype(F32) * kb_ref[...]).astype(BF16)
        return dec * st + tdot(kb, v_ref[rows(c), :])

    lax.fori_loop(0, n, rev, jnp.zeros((RET_HEAD_DIM, RET_HEAD_DIM), F32), unroll=RET_UNROLL)

    def fwd(c, st):
        q = q_ref[rows(c), :]
        k = k_ref[rows(c), :]
        v = v_ref[rows(c), :]
        qf32, kf32 = q.astype(F32), k.astype(F32)
        s = lax.dot_general(q, k, (((1,), (1,)), ((), ())), preferred_element_type=F32) * intra_ref[...]
        o = jnp.dot(s.astype(BF16), v, preferred_element_type=F32)
        o += jnp.dot((qf32 * qf_ref[...]).astype(BF16), st.astype(BF16), preferred_element_type=F32)
        o += jnp.dot((qf32 * qb_ref[...]).astype(BF16), stb_ref[c], preferred_element_type=F32)
        mu = jnp.mean(o, axis=-1, keepdims=True)
        d = o - mu
        hn = d * lax.rsqrt(jnp.mean(d * d, axis=-1, keepdims=True) + EPS)
        o_ref[rows(c), :] = (hn * _silu(ga_ref[rows(c), :])).astype(BF16)
        return dec * st + tdot((kf32 * kf_ref[...]).astype(BF16), v)

    lax.fori_loop(0, n, fwd, jnp.zeros((RET_HEAD_DIM, RET_HEAD_DIM), F32), unroll=RET_UNROLL)


def _retention_tables():
    H, cs, dk = RET_HEADS, RET_CHUNK, RET_HEAD_DIM
    log_g = jnp.log1p(-jnp.exp2(-5.0 - jnp.arange(H, dtype=F32)))
    pos = jnp.arange(cs, dtype=F32)
    intra = jnp.exp(jnp.abs(pos[:, None] - pos[None, :])[None] * log_g[:, None, None])
    col = lambda e: jnp.broadcast_to(jnp.exp(e[:, None] * log_g[None]).T[:, :, None], (H, cs, dk))
    q_fwd, q_bwd = col(pos), col(cs - 1.0 - pos)
    k_fwd, k_bwd = col(cs - pos), col(pos + 1.0)
    decay = jnp.broadcast_to(jnp.exp(cs * log_g)[:, None, None], (H, dk, dk))
    return intra, q_fwd, q_bwd, k_fwd, k_bwd, decay


def _retention(q, k, v, ga):
    B, L, _ = q.shape
    dk = RET_HEAD_DIM
    seq = pl.BlockSpec((None, L, dk), lambda b, h: (b, 0, h))
    tab = pl.BlockSpec((None, RET_CHUNK, dk), lambda b, h: (h, 0, 0))
    return pl.pallas_call(
        _ret_kernel,
        grid=(B, RET_HEADS),
        in_specs=[seq, seq, seq, seq] + [tab] * 6,
        out_specs=seq,
        out_shape=jax.ShapeDtypeStruct((B, L, D_RET), BF16),
        scratch_shapes=[pltpu.VMEM((L // RET_CHUNK, dk, dk), BF16)],
        compiler_params=_cparams("parallel", "parallel"),
    )(q, k, v, ga, *_retention_tables())


def _s5_kernel(u_ref, mt_ref, wint_ref, woutt_ref, are_ref, aim_ref, y_ref,
               xt_ref, yt_ref, kvre_ref, kvim_ref, h_ref):
    T, Gi, P = S5_CHUNK, SSM_GROUP, SSM_STATE
    C = u_ref.shape[0] // T
    pitch = C + S5_PITCH_PAD
    ng = xt_ref.shape[0]

    def relayout_in(t, carry):
        a_t = u_ref[pl.ds(t, C, stride=T), :].T
        rows = pl.ds(pl.multiple_of(t * Gi, Gi), Gi)
        for g in range(ng):
            xt_ref[g, rows, :] = a_t[g * Gi:(g + 1) * Gi, :].astype(BF16)
        return carry

    lax.fori_loop(0, T, relayout_in, 0)

    def chunk_matmuls(g, carry):
        xt = xt_ref[g]
        yt_ref[g] = jnp.dot(mt_ref[g], xt, preferred_element_type=F32)
        kv = jnp.dot(wint_ref[g], xt, preferred_element_type=F32).T
        rows = pl.ds(pl.multiple_of(g * pitch, 8), C)
        kvre_ref[rows, :] = kv[:, :2 * P]
        kvim_ref[rows, :] = kv[:, 2 * P:]
        return carry

    lax.fori_loop(0, ng, chunk_matmuls, 0)

    are, aim = are_ref[...], aim_ref[...]

    def step(k, carry):
        sf_re, sf_im, sb_re, sb_im = carry
        rf = pl.ds(k, ng, stride=pitch)
        rb = pl.ds(C - 1 - k, ng, stride=pitch)
        h_ref[0, rf, :] = sf_re
        h_ref[1, rf, :] = sf_im
        h_ref[2, rb, :] = sb_re
        h_ref[3, rb, :] = sb_im
        nf_re = are * sf_re - aim * sf_im + kvre_ref[rf, :]
        nf_im = are * sf_im + aim * sf_re + kvim_ref[rf, :]
        nb_re = are * sb_re - aim * sb_im + kvre_ref[rb, :]
        nb_im = are * sb_im + aim * sb_re + kvim_ref[rb, :]
        return nf_re, nf_im, nb_re, nb_im

    z = jnp.zeros((ng, 2 * P), F32)
    lax.fori_loop(0, C, step, (z, z, z, z), unroll=S5_SCAN_UNROLL)

    def carried_outputs(g, carry):
        rows = pl.ds(pl.multiple_of(g * pitch, 8), C)
        h = jnp.concatenate([h_ref[i, rows, :] for i in range(4)], axis=1).astype(BF16)
        yt_ref[g] += lax.dot_general(woutt_ref[g], h, (((1,), (1,)), ((), ())),
                                     preferred_element_type=F32)
        return carry

    lax.fori_loop(0, ng, carried_outputs, 0)

    def relayout_out(t, carry):
        rows = pl.ds(pl.multiple_of(t * Gi, Gi), Gi)
        b_t = jnp.concatenate([yt_ref[g, rows, :] for g in range(ng)], axis=0)
        y_ref[pl.ds(t, C, stride=T), :] = b_t.T
        return carry

    lax.fori_loop(0, T, relayout_out, 0)


def _s5_operators(a_re, a_im, log_step, b_re, b_im, c_re, c_im):
    T, G, P, Gi = S5_CHUNK, SSM_GROUPS, SSM_STATE, SSM_GROUP
    hp = lax.Precision.HIGHEST
    a_re, a_im = a_re.astype(F32), a_im.astype(F32)
    delta = jnp.exp(log_step.astype(F32))[..., None]
    z_re, z_im = a_re * delta, a_im * delta
    mag = jnp.exp(z_re)
    abar_re, abar_im = mag * jnp.cos(z_im), mag * jnp.sin(z_im)
    den = a_re * a_re + a_im * a_im
    n_re, n_im = abar_re - 1.0, abar_im
    f_re = (n_re * a_re + n_im * a_im) / den
    f_im = (n_im * a_re - n_re * a_im) / den
    b_re, b_im = b_re.astype(F32), b_im.astype(F32)
    bb_re = f_re[..., None] * b_re - f_im[..., None] * b_im
    bb_im = f_re[..., None] * b_im + f_im[..., None] * b_re
    c_re, c_im = c_re.astype(F32), c_im.astype(F32)

    pr, pi = [jnp.ones_like(abar_re)], [jnp.zeros_like(abar_re)]
    for _ in range(T):
        pr, pi = (pr + [pr[-1] * abar_re - pi[-1] * abar_im],
                  pi + [pr[-1] * abar_im + pi[-1] * abar_re])
    pw_re, pw_im = jnp.stack(pr), jnp.stack(pi)

    w_re = pw_re[:T, ..., None] * bb_re[None] - pw_im[:T, ..., None] * bb_im[None]
    w_im = pw_re[:T, ..., None] * bb_im[None] + pw_im[:T, ..., None] * bb_re[None]
    kern = (jnp.einsum('xgip,dxgpj->dxgij', c_re, w_re, precision=hp)
            - jnp.einsum('xgip,dxgpj->dxgij', c_im, w_im, precision=hp))
    s_idx = jnp.arange(T)[:, None]
    t_idx = jnp.arange(T)[None, :]
    kf = jnp.where((t_idx >= s_idx)[..., None, None, None],
                   kern[jnp.clip(t_idx - s_idx, 0, T - 1), 0], 0.0)
    kb = jnp.where((s_idx >= t_idx)[..., None, None, None],
                   kern[jnp.clip(s_idx - t_idx, 0, T - 1), 1], 0.0)
    m_intra = jnp.transpose(kf + kb, (2, 0, 4, 1, 3)).reshape(G, T * Gi, T * Gi)

    def w_in(w, direction, order):
        sel = w[order, direction]
        return jnp.transpose(sel, (1, 0, 3, 2)).reshape(G, T * Gi, P)

    fwd_order = jnp.arange(T - 1, -1, -1)
    bwd_order = jnp.arange(T)
    def w_out(direction, order):
        qr, qi = pw_re[order, direction], pw_im[order, direction]
        cr, ci = c_re[direction], c_im[direction]
        wr = cr[None] * qr[:, :, None, :] - ci[None] * qi[:, :, None, :]
        wi = cr[None] * qi[:, :, None, :] + ci[None] * qr[:, :, None, :]
        fix = lambda w: jnp.transpose(w, (1, 3, 0, 2)).reshape(G, P, T * Gi)
        return fix(wr), fix(-wi)

    of_re, of_im = w_out(0, jnp.arange(1, T + 1))
    ob_re, ob_im = w_out(1, jnp.arange(T, 0, -1))
    zero = jnp.zeros_like(of_re)
    wout = jnp.concatenate([of_re, zero, of_im, zero, zero, ob_re, zero, ob_im], axis=1)

    mt = jnp.transpose(m_intra, (0, 2, 1))
    wint = jnp.transpose(jnp.concatenate([
        w_in(w_re, 0, fwd_order), w_in(w_re, 1, bwd_order),
        w_in(w_im, 0, fwd_order), w_in(w_im, 1, bwd_order)], axis=-1), (0, 2, 1))
    woutt = jnp.transpose(wout, (0, 2, 1))
    lanes = lambda p: jnp.concatenate([p[T, 0], p[T, 1]], axis=-1)
    nblk = G // S5_GROUPS_PER_STEP
    blocked = lambda w: w.reshape((nblk, S5_GROUPS_PER_STEP) + w.shape[1:])
    return (blocked(mt.astype(BF16)), blocked(wint.astype(BF16)), blocked(woutt.astype(BF16)),
            blocked(lanes(pw_re)), blocked(lanes(pw_im)))


def _s5_scan(u, ops):
    B, L, D = u.shape
    T, P, ng = S5_CHUNK, SSM_STATE, S5_GROUPS_PER_STEP
    C = L // T
    TI = T * SSM_GROUP
    lanes = ng * SSM_GROUP
    rows = ng * (C + S5_PITCH_PAD)
    seq = pl.BlockSpec((None, L, lanes), lambda b, j: (b, 0, j))
    op = lambda r, c: pl.BlockSpec((None, ng, r, c), lambda b, j: (j, 0, 0, 0))
    coef = pl.BlockSpec((None, ng, 2 * P), lambda b, j: (j, 0, 0))
    return pl.pallas_call(
        _s5_kernel,
        grid=(B, D // lanes),
        in_specs=[seq, op(TI, TI), op(TI, TI), op(TI, 2 * TI), coef, coef],
        out_specs=seq,
        out_shape=jax.ShapeDtypeStruct((B, L, D), F32),
        scratch_shapes=[pltpu.VMEM((ng, TI, C), BF16), pltpu.VMEM((ng, TI, C), F32),
                        pltpu.VMEM((rows, 2 * P), F32), pltpu.VMEM((rows, 2 * P), F32),
                        pltpu.VMEM((4, rows, 2 * P), F32)],
        compiler_params=_cparams("parallel", "parallel"),
    )(u, *ops)


def _na_kernel(q_ref, k_ref, v_ref, g_ref, bias_ref, o_ref):
    W, kr, dh = GRID_W, NA_ROWS, NA_HEAD_DIM
    n_rows = k_ref.shape[0] // W
    j = pl.program_id(2)
    first = lax.broadcasted_iota(jnp.int32, (W, 2 * dh), 1) < dh
    for i in range(NA_ROWS_PER_STEP):
        r = j * NA_ROWS_PER_STEP + i
        rs = jnp.clip(r - kr // 2, 0, n_rows - kr)
        var = rs - r + (NA_ROWS - 1)
        keys = pl.ds(pl.multiple_of(rs * W, W), kr * W)
        kblk, vblk = k_ref[keys, :], v_ref[keys, :]
        qrow = slice(i * W, (i + 1) * W)
        q2 = q_ref[qrow, :]
        zero = jnp.zeros_like(q2)
        qm = jnp.concatenate([jnp.where(first, q2, zero), jnp.where(first, zero, q2)], axis=0)
        s = lax.dot_general(qm, kblk, (((1,), (1,)), ((), ())), preferred_element_type=F32)
        s = s + bias_ref[var]
        p = jnp.exp(s - jnp.max(s, axis=-1, keepdims=True))
        den = jnp.sum(p, axis=-1, keepdims=True)
        o2 = jnp.dot(p.astype(BF16), vblk, preferred_element_type=F32) / den
        o = jnp.where(first, o2[:W], o2[W:])
        o_ref[qrow, :] = (o * _silu(g_ref[qrow, :])).astype(BF16)


def _na_bias_table(rel_bias):
    W, R = GRID_W, NA_ROWS
    c_idx = jnp.arange(W)
    col_start = jnp.clip(c_idx - NA_COLS // 2, 0, W - NA_COLS)
    col_valid = ((c_idx[None, :] >= col_start[:, None])
                 & (c_idx[None, :] < col_start[:, None] + NA_COLS))
    dc_idx = jnp.clip(c_idx[None, :] - c_idx[:, None] + NA_COLS - 1, 0, 2 * NA_COLS - 2)
    col_bias = jnp.take(rel_bias.astype(F32), dc_idx, axis=2)
    col_bias = jnp.where(col_valid[None, None], col_bias, NEG_INF)
    win = jnp.arange(R)[:, None] + jnp.arange(R)[None, :]
    tab = col_bias[:, win]
    return jnp.transpose(tab, (1, 0, 3, 2, 4)).reshape(R, NA_HEADS // 2, 2 * W, R * W)


def _neighbourhood_attention(q, k, v, g, bias_tab):
    B, L, D = q.shape
    rows = L // GRID_W
    assert rows >= NA_ROWS and rows % NA_ROWS_PER_STEP == 0
    tq = NA_ROWS_PER_STEP * GRID_W
    lanes = 2 * NA_HEAD_DIM
    blk = pl.BlockSpec((None, tq, lanes), lambda b, h, j: (b, j, h))
    seq = pl.BlockSpec((None, L, lanes), lambda b, h, j: (b, 0, h))
    return pl.pallas_call(
        _na_kernel,
        grid=(B, NA_HEADS // 2, rows // NA_ROWS_PER_STEP),
        in_specs=[blk, seq, seq, blk,
                  pl.BlockSpec((NA_ROWS, None, 2 * GRID_W, NA_ROWS * GRID_W),
                               lambda b, h, j: (0, h, 0, 0))],
        out_specs=blk,
        out_shape=jax.ShapeDtypeStruct((B, L, D), BF16),
        compiler_params=_cparams("parallel", "parallel", "arbitrary"),
    )(q, k, v, g, bias_tab)


def _finish(y, x_ref, mod_ref, g_ref, o_ref):
    yn = y * lax.rsqrt(jnp.mean(y * y, axis=-1, keepdims=True) + EPS) * g_ref[...]
    o_ref[...] = x_ref[...] + mod_ref[2:3, :] * yn


def _even_out_kernel(oa_ref, ys_ref, u_ref, gb_ref, x_ref, mod_ref, g_ref, d_ref, wglu_ref, wout_ref,
                     o_ref):
    y = _gelu_tanh(ys_ref[...] + d_ref[...] * u_ref[...])
    y = y * _sigmoid(jnp.dot(y.astype(BF16), wglu_ref[...], preferred_element_type=F32))
    ob = (y * _silu(gb_ref[...])).astype(BF16)
    out = jnp.dot(oa_ref[...], wout_ref[:D_RET, :], preferred_element_type=F32)
    out += jnp.dot(ob, wout_ref[D_RET:, :], preferred_element_type=F32)
    _finish(out, x_ref, mod_ref, g_ref, o_ref)


def _even_out(oa, ys, u, gb, x, mod3, g_post, d_skip, w_glu, w_out):
    B, L, D = x.shape
    tm = TOKEN_TILE
    tok = lambda w: pl.BlockSpec((None, tm, w), lambda b, t: (b, t, 0))
    full = lambda r, c: pl.BlockSpec((r, c), lambda b, t: (0, 0))
    return pl.pallas_call(
        _even_out_kernel,
        grid=(B, L // tm),
        in_specs=[tok(D_RET), tok(D_SSM), tok(D_SSM), tok(D_SSM), tok(D),
                  pl.BlockSpec((None, 3, D), lambda b, t: (b, 0, 0)),
                  full(1, D), full(1, D_SSM), full(D_SSM, D_SSM), full(D, D)],
        out_specs=tok(D),
        out_shape=jax.ShapeDtypeStruct((B, L, D), F32),
        compiler_params=_cparams("parallel", "parallel"),
    )(oa, ys, u, gb, x, mod3, g_post, d_skip, w_glu, w_out)


def _odd_out_kernel(o_ref_in, x_ref, mod_ref, g_ref, wout_ref, o_ref):
    out = jnp.dot(o_ref_in[...], wout_ref[...], preferred_element_type=F32)
    _finish(out, x_ref, mod_ref, g_ref, o_ref)


def _odd_out(o, x, mod3, g_post, w_out):
    B, L, D = x.shape
    tm = TOKEN_TILE
    tok = pl.BlockSpec((None, tm, D), lambda b, t: (b, t, 0))
    return pl.pallas_call(
        _odd_out_kernel,
        grid=(B, L // tm),
        in_specs=[tok, tok,
                  pl.BlockSpec((None, 3, D), lambda b, t: (b, 0, 0)),
                  pl.BlockSpec((1, D), lambda b, t: (0, 0)),
                  pl.BlockSpec((D, D), lambda b, t: (0, 0))],
        out_specs=tok,
        out_shape=jax.ShapeDtypeStruct((B, L, D), F32),
        compiler_params=_cparams("parallel", "parallel"),
    )(o, x, mod3, g_post, w_out)


def _rotary_tables(L):
    dh = RET_HEAD_DIM
    inv = ROPE_BASE ** (-jnp.arange(0, dh, 2, dtype=F32) / dh)
    ang = jnp.arange(L, dtype=F32)[:, None] * inv[None, :]
    cos, sin = jnp.cos(ang), jnp.sin(ang)
    return jnp.concatenate([cos, cos], axis=1), jnp.concatenate([-sin, sin], axis=1)


def _trunk(x, c, p):
    B, L, D = x.shape
    mods = _modulation(c, p["w_mod"], p["b_mod"]).reshape(DEPTH, B, 3, D)
    cos2, sin2 = _rotary_tables(L)
    for i in range(DEPTH):
        j = i // 2
        g_pre = p["norm_pre"][i].reshape(1, D)
        g_post = p["norm_post"][i].reshape(1, D)
        if i % 2 == 0:
            q, k, v, ga, u, gb = _even_in(x, mods[i], g_pre, p["w_in_ab"][j], cos2, sin2)
            oa = _retention(q, k, v, ga)
            ys = _s5_scan(u, p["s5_ops"][j])
            x = _even_out(oa, ys, u, gb, x, mods[i], g_post, p["ssm_d"][j].reshape(1, D_SSM),
                          p["ssm_w_glu"][j], p["w_out_ab"][j])
        else:
            q, k, v, g = _odd_in(x, mods[i], g_pre, p["w_in_c"][j])
            o = _neighbourhood_attention(q, k, v, g, p["na_bias"][j])
            x = _odd_out(o, x, mods[i], g_post, p["w_out_c"][j])
    return x


def kernel(x_prompt, x_sample, c_prompt, c_sample, norm_pre, norm_post, w_mod, b_mod, w_in_ab, w_out_ab, ssm_a_re, ssm_a_im, ssm_log_step, ssm_b_re, ssm_b_im, ssm_c_re, ssm_c_im, ssm_d, ssm_w_glu, w_in_c, w_out_c, na_rel_bias):
    n_even, n_odd = w_in_ab.shape[0], w_in_c.shape[0]
    p = {
        "norm_pre": norm_pre, "norm_post": norm_post, "w_mod": w_mod, "b_mod": b_mod,
        "w_in_ab": w_in_ab.astype(BF16), "w_out_ab": w_out_ab.astype(BF16),
        "ssm_d": ssm_d, "ssm_w_glu": ssm_w_glu.astype(BF16),
        "w_in_c": w_in_c.astype(BF16), "w_out_c": w_out_c.astype(BF16),
        "s5_ops": [_s5_operators(ssm_a_re[j], ssm_a_im[j], ssm_log_step[j], ssm_b_re[j], ssm_b_im[j],
                                 ssm_c_re[j], ssm_c_im[j]) for j in range(n_even)],
        "na_bias": [_na_bias_table(na_rel_bias[j]) for j in range(n_odd)],
    }
    return _trunk(x_prompt, c_prompt, p), _trunk(x_sample, c_sample, p)
```

```python
import functools
import math

import jax
import jax.numpy as jnp
from jax import lax
from jax.experimental import pallas as pl
from jax.experimental.pallas import tpu as pltpu

F32 = jnp.float32
BF16 = jnp.bfloat16

D_MODEL = 1024
DEPTH = 4
GRID_W = 64
D_RET = 512
RET_HEADS = 4
RET_HEAD_DIM = 128
RET_CHUNK = 128
D_SSM = 512
SSM_GROUP = 16
SSM_GROUPS = 32
SSM_STATE = 64
EVEN_IN = 4 * D_RET + 2 * D_SSM
NA_HEADS = 16
NA_HEAD_DIM = 64
NA_ROWS = 8
NA_COLS = 16
ODD_IN = 4 * D_MODEL
ROPE_BASE = 10000.0
EPS = 1e-6
NEG_INF = -1e30
LOG2E = 1.4426950408889634

S5_CHUNK = 16
S5_GROUPS_PER_STEP = 8
S5_PITCH_PAD = 8
S5_SCAN_UNROLL = 4
TOKEN_TILE = 512
NA_ROWS_PER_STEP = 16
NA_SCORE_LEAD = 3
RET_UNROLL = 4
VMEM_LIMIT = 52 * 1024 * 1024


def _cparams(*sem):
    return pltpu.CompilerParams(dimension_semantics=sem, vmem_limit_bytes=VMEM_LIMIT)


def _sigmoid(x):
    return 1.0 / (1.0 + jnp.exp(-x))


def _silu(x):
    return x * _sigmoid(x)


def _gelu_tanh(x):
    c = math.sqrt(2.0 / math.pi)
    return 0.5 * x * (1.0 + jnp.tanh(c * (x + 0.044715 * (x * x * x))))


def _mod_kernel(c_ref, w_ref, b_ref, o_ref):
    a = _silu(c_ref[...])
    o_ref[...] = jnp.dot(a, w_ref[...], preferred_element_type=F32) + b_ref[...]


def _modulation(c, w_mod, b_mod):
    B = c.shape[0]
    tn = 1024
    return pl.pallas_call(
        _mod_kernel,
        grid=(DEPTH, 3 * D_MODEL // tn),
        in_specs=[
            pl.BlockSpec((B, D_MODEL), lambda i, n: (0, 0)),
            pl.BlockSpec((None, D_MODEL, tn), lambda i, n: (i, 0, n)),
            pl.BlockSpec((None, 1, tn), lambda i, n: (i, 0, n)),
        ],
        out_specs=pl.BlockSpec((None, B, tn), lambda i, n: (i, 0, n)),
        out_shape=jax.ShapeDtypeStruct((DEPTH, B, 3 * D_MODEL), F32),
        compiler_params=_cparams("arbitrary", "arbitrary"),
    )(c, w_mod, b_mod.reshape(DEPTH, 1, 3 * D_MODEL))


def _prenorm(x_ref, mod_ref, g_ref):
    x = x_ref[...]
    y = x * lax.rsqrt(jnp.mean(x * x, axis=-1, keepdims=True) + EPS) * g_ref[...]
    return (y * (1.0 + mod_ref[1:2, :]) + mod_ref[0:1, :]).astype(BF16)


def _even_in_kernel(x_ref, mod_ref, g_ref, w_ref, cos_ref, sin_ref,
                    q_ref, k_ref, v_ref, ga_ref, u_ref, gb_ref):
    h = _prenorm(x_ref, mod_ref, g_ref)
    cos2, sin2 = cos_ref[...], sin_ref[...]

    def proj(c):
        return jnp.dot(h, w_ref[:, c * D_RET:(c + 1) * D_RET], preferred_element_type=F32)

    def rotary(z):
        cols = []
        for hd in range(RET_HEADS):
            zh = z[:, hd * RET_HEAD_DIM:(hd + 1) * RET_HEAD_DIM]
            cols.append(zh * cos2 + pltpu.roll(zh, RET_HEAD_DIM // 2, 1) * sin2)
        return jnp.concatenate(cols, axis=1)

    q_ref[...] = rotary(proj(0)).astype(BF16)
    k_ref[...] = (rotary(proj(1)) * (RET_HEAD_DIM ** -0.5)).astype(BF16)
    v_ref[...] = proj(2).astype(BF16)
    ga_ref[...] = proj(3)
    u_ref[...] = proj(4)
    gb_ref[...] = proj(5)


def _even_in(x, mod3, g_pre, w_in, cos2, sin2):
    B, L, D = x.shape
    tm = TOKEN_TILE
    tok = lambda w: pl.BlockSpec((None, tm, w), lambda b, t: (b, t, 0))
    out = lambda dt: jax.ShapeDtypeStruct((B, L, D_RET), dt)
    return pl.pallas_call(
        _even_in_kernel,
        grid=(B, L // tm),
        in_specs=[
            tok(D),
            pl.BlockSpec((None, 3, D), lambda b, t: (b, 0, 0)),
            pl.BlockSpec((1, D), lambda b, t: (0, 0)),
            pl.BlockSpec((D, EVEN_IN), lambda b, t: (0, 0)),
            pl.BlockSpec((tm, RET_HEAD_DIM), lambda b, t: (t, 0)),
            pl.BlockSpec((tm, RET_HEAD_DIM), lambda b, t: (t, 0)),
        ],
        out_specs=[tok(D_RET)] * 6,
        out_shape=[out(BF16), out(BF16), out(BF16), out(F32), out(F32), out(F32)],
        compiler_params=_cparams("parallel", "parallel"),
    )(x, mod3, g_pre, w_in, cos2, sin2)


def _odd_in_kernel(x_ref, mod_ref, g_ref, w_ref, q_ref, k_ref, v_ref, gate_ref):
    h = _prenorm(x_ref, mod_ref, g_ref)
    half = D_MODEL // 2

    def proj(c):
        return jnp.dot(h, w_ref[:, c * half:(c + 1) * half], preferred_element_type=F32)

    for c in range(2):
        sl = slice(c * half, (c + 1) * half)
        q_ref[:, sl] = (proj(c) * (NA_HEAD_DIM ** -0.5 * LOG2E)).astype(BF16)
        k_ref[:, sl] = proj(2 + c).astype(BF16)
        v_ref[:, sl] = proj(4 + c).astype(BF16)
        gate_ref[:, sl] = proj(6 + c)


def _odd_in(x, mod3, g_pre, w_in):
    B, L, D = x.shape
    tm = TOKEN_TILE
    tok = pl.BlockSpec((None, tm, D), lambda b, t: (b, t, 0))
    out = lambda dt: jax.ShapeDtypeStruct((B, L, D), dt)
    return pl.pallas_call(
        _odd_in_kernel,
        grid=(B, L // tm),
        in_specs=[
            tok,
            pl.BlockSpec((None, 3, D), lambda b, t: (b, 0, 0)),
            pl.BlockSpec((1, D), lambda b, t: (0, 0)),
            pl.BlockSpec((D, ODD_IN), lambda b, t: (0, 0)),
        ],
        out_specs=[tok] * 4,
        out_shape=[out(BF16), out(BF16), out(BF16), out(F32)],
        compiler_params=_cparams("parallel", "parallel"),
    )(x, mod3, g_pre, w_in)


def _ret_kernel(q_ref, k_ref, v_ref, ga_ref, intra_ref, qf_ref, qb_ref, kf_ref, kb_ref, dec_ref,
                o_ref, stb_ref):
    cs = RET_CHUNK
    n = q_ref.shape[0] // cs
    dec = dec_ref[...]
    tdot = lambda a, b: lax.dot_general(a, b, (((0,), (0,)), ((), ())), preferred_element_type=F32)

    def rows(c):
        return pl.ds(pl.multiple_of(c * cs, cs), cs)

    ways = RET_UNROLL
    state0 = jnp.zeros((RET_HEAD_DIM, RET_HEAD_DIM), F32)

    def rev(it, st):
        chunks = [n - 1 - (it * ways + w) for w in range(ways)]
        kvs = [tdot((k_ref[rows(c), :].astype(F32) * kb_ref[...]).astype(BF16), v_ref[rows(c), :])
               for c in chunks]
        for c, kv in zip(chunks, kvs):
            stb_ref[c] = st.astype(BF16)
            st = dec * st + kv
        return st

    lax.fori_loop(0, n // ways, rev, state0)

    def fwd(it, st):
        chunks = [it * ways + w for w in range(ways)]
        q = [q_ref[rows(c), :] for c in chunks]
        k = [k_ref[rows(c), :] for c in chunks]
        v = [v_ref[rows(c), :] for c in chunks]
        s = [lax.dot_general(q[w], k[w], (((1,), (1,)), ((), ())), preferred_element_type=F32)
             * intra_ref[...] for w in range(ways)]
        kvs = [tdot((k[w].astype(F32) * kf_ref[...]).astype(BF16), v[w]) for w in range(ways)]
        before = []
        for kv in kvs:
            before.append(st.astype(BF16))
            st = dec * st + kv
        outs = []
        for w, c in enumerate(chunks):
            qf32 = q[w].astype(F32)
            o = jnp.dot(s[w].astype(BF16), v[w], preferred_element_type=F32)
            o += jnp.dot((qf32 * qf_ref[...]).astype(BF16), before[w], preferred_element_type=F32)
            o += jnp.dot((qf32 * qb_ref[...]).astype(BF16), stb_ref[c], preferred_element_type=F32)
            outs.append(o)
        for c, o in zip(chunks, outs):
            mu = jnp.mean(o, axis=-1, keepdims=True)
            d = o - mu
            hn = d * lax.rsqrt(jnp.mean(d * d, axis=-1, keepdims=True) + EPS)
            o_ref[rows(c), :] = (hn * _silu(ga_ref[rows(c), :])).astype(BF16)
        return st

    lax.fori_loop(0, n // ways, fwd, state0)


def _retention_tables():
    H, cs, dk = RET_HEADS, RET_CHUNK, RET_HEAD_DIM
    log_g = jnp.log1p(-jnp.exp2(-5.0 - jnp.arange(H, dtype=F32)))
    pos = jnp.arange(cs, dtype=F32)
    intra = jnp.exp(jnp.abs(pos[:, None] - pos[None, :])[None] * log_g[:, None, None])
    col = lambda e: jnp.broadcast_to(jnp.exp(e[:, None] * log_g[None]).T[:, :, None], (H, cs, dk))
    q_fwd, q_bwd = col(pos), col(cs - 1.0 - pos)
    k_fwd, k_bwd = col(cs - pos), col(pos + 1.0)
    decay = jnp.broadcast_to(jnp.exp(cs * log_g)[:, None, None], (H, dk, dk))
    return intra, q_fwd, q_bwd, k_fwd, k_bwd, decay


def _retention(q, k, v, ga):
    B, L, _ = q.shape
    dk = RET_HEAD_DIM
    seq = pl.BlockSpec((None, L, dk), lambda b, h: (b, 0, h))
    tab = pl.BlockSpec((None, RET_CHUNK, dk), lambda b, h: (h, 0, 0))
    return pl.pallas_call(
        _ret_kernel,
        grid=(B, RET_HEADS),
        in_specs=[seq, seq, seq, seq] + [tab] * 6,
        out_specs=seq,
        out_shape=jax.ShapeDtypeStruct((B, L, D_RET), BF16),
        scratch_shapes=[pltpu.VMEM((L // RET_CHUNK, dk, dk), BF16)],
        compiler_params=_cparams("parallel", "parallel"),
    )(q, k, v, ga, *_retention_tables())


def _s5_kernel(u_ref, mt_ref, wint_ref, woutt_ref, are_ref, aim_ref, y_ref,
               xt_ref, yt_ref, kvre_ref, kvim_ref, h_ref):
    T, Gi, P = S5_CHUNK, SSM_GROUP, SSM_STATE
    C = u_ref.shape[0] // T
    pitch = C + S5_PITCH_PAD
    ng = xt_ref.shape[0]

    def relayout_in(t, carry):
        a_t = u_ref[pl.ds(t, C, stride=T), :].T
        rows = pl.ds(pl.multiple_of(t * Gi, Gi), Gi)
        for g in range(ng):
            xt_ref[g, rows, :] = a_t[g * Gi:(g + 1) * Gi, :].astype(BF16)
        return carry

    lax.fori_loop(0, T, relayout_in, 0)

    def chunk_matmuls(g, carry):
        xt = xt_ref[g]
        yt_ref[g] = jnp.dot(mt_ref[g], xt, preferred_element_type=F32)
        kv = jnp.dot(wint_ref[g], xt, preferred_element_type=F32).T
        rows = pl.ds(pl.multiple_of(g * pitch, 8), C)
        kvre_ref[rows, :] = kv[:, :2 * P]
        kvim_ref[rows, :] = kv[:, 2 * P:]
        return carry

    lax.fori_loop(0, ng, chunk_matmuls, 0)

    are, aim = are_ref[...], aim_ref[...]

    def step(k, carry):
        sf_re, sf_im, sb_re, sb_im = carry
        rf = pl.ds(k, ng, stride=pitch)
        rb = pl.ds(C - 1 - k, ng, stride=pitch)
        h_ref[0, rf, :] = sf_re
        h_ref[1, rf, :] = sf_im
        h_ref[2, rb, :] = sb_re
        h_ref[3, rb, :] = sb_im
        nf_re = are * sf_re - aim * sf_im + kvre_ref[rf, :]
        nf_im = are * sf_im + aim * sf_re + kvim_ref[rf, :]
        nb_re = are * sb_re - aim * sb_im + kvre_ref[rb, :]
        nb_im = are * sb_im + aim * sb_re + kvim_ref[rb, :]
        return nf_re, nf_im, nb_re, nb_im

    z = jnp.zeros((ng, 2 * P), F32)
    lax.fori_loop(0, C, step, (z, z, z, z), unroll=S5_SCAN_UNROLL)

    def carried_outputs(g, carry):
        rows = pl.ds(pl.multiple_of(g * pitch, 8), C)
        h = jnp.concatenate([h_ref[i, rows, :] for i in range(4)], axis=1).astype(BF16)
        yt_ref[g] += lax.dot_general(woutt_ref[g], h, (((1,), (1,)), ((), ())),
                                     preferred_element_type=F32)
        return carry

    lax.fori_loop(0, ng, carried_outputs, 0)

    def relayout_out(t, carry):
        rows = pl.ds(pl.multiple_of(t * Gi, Gi), Gi)
        b_t = jnp.concatenate([yt_ref[g, rows, :] for g in range(ng)], axis=0)
        y_ref[pl.ds(t, C, stride=T), :] = b_t.T
        return carry

    lax.fori_loop(0, T, relayout_out, 0)


def _s5_operators(a_re, a_im, log_step, b_re, b_im, c_re, c_im):
    T, G, P, Gi = S5_CHUNK, SSM_GROUPS, SSM_STATE, SSM_GROUP
    hp = lax.Precision.HIGHEST
    a_re, a_im = a_re.astype(F32), a_im.astype(F32)
    delta = jnp.exp(log_step.astype(F32))[..., None]
    z_re, z_im = a_re * delta, a_im * delta
    mag = jnp.exp(z_re)
    abar_re, abar_im = mag * jnp.cos(z_im), mag * jnp.sin(z_im)
    den = a_re * a_re + a_im * a_im
    n_re, n_im = abar_re - 1.0, abar_im
    f_re = (n_re * a_re + n_im * a_im) / den
    f_im = (n_im * a_re - n_re * a_im) / den
    b_re, b_im = b_re.astype(F32), b_im.astype(F32)
    bb_re = f_re[..., None] * b_re - f_im[..., None] * b_im
    bb_im = f_re[..., None] * b_im + f_im[..., None] * b_re
    c_re, c_im = c_re.astype(F32), c_im.astype(F32)

    pr, pi = [jnp.ones_like(abar_re)], [jnp.zeros_like(abar_re)]
    for _ in range(T):
        pr, pi = (pr + [pr[-1] * abar_re - pi[-1] * abar_im],
                  pi + [pr[-1] * abar_im + pi[-1] * abar_re])
    pw_re, pw_im = jnp.stack(pr), jnp.stack(pi)

    w_re = pw_re[:T, ..., None] * bb_re[None] - pw_im[:T, ..., None] * bb_im[None]
    w_im = pw_re[:T, ..., None] * bb_im[None] + pw_im[:T, ..., None] * bb_re[None]
    kern = (jnp.einsum('xgip,dxgpj->dxgij', c_re, w_re, precision=hp)
            - jnp.einsum('xgip,dxgpj->dxgij', c_im, w_im, precision=hp))
    s_idx = jnp.arange(T)[:, None]
    t_idx = jnp.arange(T)[None, :]
    kf = jnp.where((t_idx >= s_idx)[..., None, None, None],
                   kern[jnp.clip(t_idx - s_idx, 0, T - 1), 0], 0.0)
    kb = jnp.where((s_idx >= t_idx)[..., None, None, None],
                   kern[jnp.clip(s_idx - t_idx, 0, T - 1), 1], 0.0)
    m_intra = jnp.transpose(kf + kb, (2, 0, 4, 1, 3)).reshape(G, T * Gi, T * Gi)

    def w_in(w, direction, order):
        sel = w[order, direction]
        return jnp.transpose(sel, (1, 0, 3, 2)).reshape(G, T * Gi, P)

    fwd_order = jnp.arange(T - 1, -1, -1)
    bwd_order = jnp.arange(T)
    def w_out(direction, order):
        qr, qi = pw_re[order, direction], pw_im[order, direction]
        cr, ci = c_re[direction], c_im[direction]
        wr = cr[None] * qr[:, :, None, :] - ci[None] * qi[:, :, None, :]
        wi = cr[None] * qi[:, :, None, :] + ci[None] * qr[:, :, None, :]
        fix = lambda w: jnp.transpose(w, (1, 3, 0, 2)).reshape(G, P, T * Gi)
        return fix(wr), fix(-wi)

    of_re, of_im = w_out(0, jnp.arange(1, T + 1))
    ob_re, ob_im = w_out(1, jnp.arange(T, 0, -1))
    zero = jnp.zeros_like(of_re)
    wout = jnp.concatenate([of_re, zero, of_im, zero, zero, ob_re, zero, ob_im], axis=1)

    mt = jnp.transpose(m_intra, (0, 2, 1))
    wint = jnp.transpose(jnp.concatenate([
        w_in(w_re, 0, fwd_order), w_in(w_re, 1, bwd_order),
        w_in(w_im, 0, fwd_order), w_in(w_im, 1, bwd_order)], axis=-1), (0, 2, 1))
    woutt = jnp.transpose(wout, (0, 2, 1))
    lanes = lambda p: jnp.concatenate([p[T, 0], p[T, 1]], axis=-1)
    nblk = G // S5_GROUPS_PER_STEP
    blocked = lambda w: w.reshape((nblk, S5_GROUPS_PER_STEP) + w.shape[1:])
    return (blocked(mt.astype(BF16)), blocked(wint.astype(BF16)), blocked(woutt.astype(BF16)),
            blocked(lanes(pw_re)), blocked(lanes(pw_im)))


def _s5_scan(u, ops):
    B, L, D = u.shape
    T, P, ng = S5_CHUNK, SSM_STATE, S5_GROUPS_PER_STEP
    C = L // T
    TI = T * SSM_GROUP
    lanes = ng * SSM_GROUP
    rows = ng * (C + S5_PITCH_PAD)
    seq = pl.BlockSpec((None, L, lanes), lambda b, j: (b, 0, j))
    op = lambda r, c: pl.BlockSpec((None, ng, r, c), lambda b, j: (j, 0, 0, 0))
    coef = pl.BlockSpec((None, ng, 2 * P), lambda b, j: (j, 0, 0))
    return pl.pallas_call(
        _s5_kernel,
        grid=(B, D // lanes),
        in_specs=[seq, op(TI, TI), op(TI, TI), op(TI, 2 * TI), coef, coef],
        out_specs=seq,
        out_shape=jax.ShapeDtypeStruct((B, L, D), F32),
        scratch_shapes=[pltpu.VMEM((ng, TI, C), BF16), pltpu.VMEM((ng, TI, C), F32),
                        pltpu.VMEM((rows, 2 * P), F32), pltpu.VMEM((rows, 2 * P), F32),
                        pltpu.VMEM((4, rows, 2 * P), F32)],
        compiler_params=_cparams("parallel", "parallel"),
    )(u, *ops)


def _na_kernel(q_ref, k_ref, v_ref, g_ref, bias_ref, o_ref):
    W, kr, dh = GRID_W, NA_ROWS, NA_HEAD_DIM
    n_rows = k_ref.shape[0] // W
    j = pl.program_id(2)
    first = lax.broadcasted_iota(jnp.int32, (W, 2 * dh), 1) < dh

    def scores(i):
        r = j * NA_ROWS_PER_STEP + i
        rs = jnp.clip(r - kr // 2, 0, n_rows - kr)
        var = rs - r + (NA_ROWS - 1)
        keys = pl.ds(pl.multiple_of(rs * W, W), kr * W)
        q2 = q_ref[i * W:(i + 1) * W, :]
        zero = jnp.zeros_like(q2)
        qm = jnp.concatenate([jnp.where(first, q2, zero), jnp.where(first, zero, q2)], axis=0)
        s = lax.dot_general(qm, k_ref[keys, :], (((1,), (1,)), ((), ())), preferred_element_type=F32)
        return s + bias_ref[var], keys

    def attend(i, s, keys):
        p = jnp.exp2(s - jnp.max(s, axis=-1, keepdims=True))
        den = jnp.sum(p, axis=-1, keepdims=True)
        o2 = jnp.dot(p.astype(BF16), v_ref[keys, :], preferred_element_type=F32) / den
        o = jnp.where(first, o2[:W], o2[W:])
        qrow = slice(i * W, (i + 1) * W)
        o_ref[qrow, :] = (o * _silu(g_ref[qrow, :])).astype(BF16)

    pending = [scores(i) for i in range(NA_SCORE_LEAD)]
    for i in range(NA_ROWS_PER_STEP):
        if i + NA_SCORE_LEAD < NA_ROWS_PER_STEP:
            pending.append(scores(i + NA_SCORE_LEAD))
        attend(i, *pending.pop(0))


def _na_bias_table(rel_bias):
    W, R = GRID_W, NA_ROWS
    c_idx = jnp.arange(W)
    col_start = jnp.clip(c_idx - NA_COLS // 2, 0, W - NA_COLS)
    col_valid = ((c_idx[None, :] >= col_start[:, None])
                 & (c_idx[None, :] < col_start[:, None] + NA_COLS))
    dc_idx = jnp.clip(c_idx[None, :] - c_idx[:, None] + NA_COLS - 1, 0, 2 * NA_COLS - 2)
    col_bias = jnp.take(rel_bias.astype(F32), dc_idx, axis=2)
    col_bias = jnp.where(col_valid[None, None], col_bias, NEG_INF)
    win = jnp.arange(R)[:, None] + jnp.arange(R)[None, :]
    tab = col_bias[:, win]
    tab = jnp.transpose(tab, (1, 0, 3, 2, 4)).reshape(R, NA_HEADS // 2, 2 * W, R * W)
    return tab * LOG2E


def _neighbourhood_attention(q, k, v, g, bias_tab):
    B, L, D = q.shape
    rows = L // GRID_W
    assert rows >= NA_ROWS and rows % NA_ROWS_PER_STEP == 0
    tq = NA_ROWS_PER_STEP * GRID_W
    lanes = 2 * NA_HEAD_DIM
    blk = pl.BlockSpec((None, tq, lanes), lambda b, h, j: (b, j, h))
    seq = pl.BlockSpec((None, L, lanes), lambda b, h, j: (b, 0, h))
    return pl.pallas_call(
        _na_kernel,
        grid=(B, NA_HEADS // 2, rows // NA_ROWS_PER_STEP),
        in_specs=[blk, seq, seq, blk,
                  pl.BlockSpec((NA_ROWS, None, 2 * GRID_W, NA_ROWS * GRID_W),
                               lambda b, h, j: (0, h, 0, 0))],
        out_specs=blk,
        out_shape=jax.ShapeDtypeStruct((B, L, D), BF16),
        compiler_params=_cparams("parallel", "parallel", "arbitrary"),
    )(q, k, v, g, bias_tab)


def _finish(y, x_ref, mod_ref, g_ref, o_ref):
    yn = y * lax.rsqrt(jnp.mean(y * y, axis=-1, keepdims=True) + EPS) * g_ref[...]
    o_ref[...] = x_ref[...] + mod_ref[2:3, :] * yn


def _even_out_kernel(oa_ref, ys_ref, u_ref, gb_ref, x_ref, mod_ref, g_ref, d_ref, wglu_ref, wout_ref,
                     o_ref):
    y = _gelu_tanh(ys_ref[...] + d_ref[...] * u_ref[...])
    y = y * _sigmoid(jnp.dot(y.astype(BF16), wglu_ref[...], preferred_element_type=F32))
    ob = (y * _silu(gb_ref[...])).astype(BF16)
    out = jnp.dot(oa_ref[...], wout_ref[:D_RET, :], preferred_element_type=F32)
    out += jnp.dot(ob, wout_ref[D_RET:, :], preferred_element_type=F32)
    _finish(out, x_ref, mod_ref, g_ref, o_ref)


def _even_out(oa, ys, u, gb, x, mod3, g_post, d_skip, w_glu, w_out):
    B, L, D = x.shape
    tm = TOKEN_TILE
    tok = lambda w: pl.BlockSpec((None, tm, w), lambda b, t: (b, t, 0))
    full = lambda r, c: pl.BlockSpec((r, c), lambda b, t: (0, 0))
    return pl.pallas_call(
        _even_out_kernel,
        grid=(B, L // tm),
        in_specs=[tok(D_RET), tok(D_SSM), tok(D_SSM), tok(D_SSM), tok(D),
                  pl.BlockSpec((None, 3, D), lambda b, t: (b, 0, 0)),
                  full(1, D), full(1, D_SSM), full(D_SSM, D_SSM), full(D, D)],
        out_specs=tok(D),
        out_shape=jax.ShapeDtypeStruct((B, L, D), F32),
        compiler_params=_cparams("parallel", "parallel"),
    )(oa, ys, u, gb, x, mod3, g_post, d_skip, w_glu, w_out)


def _odd_out_kernel(o_ref_in, x_ref, mod_ref, g_ref, wout_ref, o_ref):
    out = jnp.dot(o_ref_in[...], wout_ref[...], preferred_element_type=F32)
    _finish(out, x_ref, mod_ref, g_ref, o_ref)


def _odd_out(o, x, mod3, g_post, w_out):
    B, L, D = x.shape
    tm = TOKEN_TILE
    tok = pl.BlockSpec((None, tm, D), lambda b, t: (b, t, 0))
    return pl.pallas_call(
        _odd_out_kernel,
        grid=(B, L // tm),
        in_specs=[tok, tok,
                  pl.BlockSpec((None, 3, D), lambda b, t: (b, 0, 0)),
                  pl.BlockSpec((1, D), lambda b, t: (0, 0)),
                  pl.BlockSpec((D, D), lambda b, t: (0, 0))],
        out_specs=tok,
        out_shape=jax.ShapeDtypeStruct((B, L, D), F32),
        compiler_params=_cparams("parallel", "parallel"),
    )(o, x, mod3, g_post, w_out)


def _rotary_tables(L):
    dh = RET_HEAD_DIM
    inv = ROPE_BASE ** (-jnp.arange(0, dh, 2, dtype=F32) / dh)
    ang = jnp.arange(L, dtype=F32)[:, None] * inv[None, :]
    cos, sin = jnp.cos(ang), jnp.sin(ang)
    return jnp.concatenate([cos, cos], axis=1), jnp.concatenate([-sin, sin], axis=1)


def _trunk(x, c, p):
    B, L, D = x.shape
    mods = _modulation(c, p["w_mod"], p["b_mod"]).reshape(DEPTH, B, 3, D)
    cos2, sin2 = _rotary_tables(L)
    for i in range(DEPTH):
        j = i // 2
        g_pre = p["norm_pre"][i].reshape(1, D)
        g_post = p["norm_post"][i].reshape(1, D)
        if i % 2 == 0:
            q, k, v, ga, u, gb = _even_in(x, mods[i], g_pre, p["w_in_ab"][j], cos2, sin2)
            oa = _retention(q, k, v, ga)
            ys = _s5_scan(u, p["s5_ops"][j])
            x = _even_out(oa, ys, u, gb, x, mods[i], g_post, p["ssm_d"][j].reshape(1, D_SSM),
                          p["ssm_w_glu"][j], p["w_out_ab"][j])
        else:
            q, k, v, g = _odd_in(x, mods[i], g_pre, p["w_in_c"][j])
            o = _neighbourhood_attention(q, k, v, g, p["na_bias"][j])
            x = _odd_out(o, x, mods[i], g_post, p["w_out_c"][j])
    return x


def kernel(x_prompt, x_sample, c_prompt, c_sample, norm_pre, norm_post, w_mod, b_mod, w_in_ab, w_out_ab, ssm_a_re, ssm_a_im, ssm_log_step, ssm_b_re, ssm_b_im, ssm_c_re, ssm_c_im, ssm_d, ssm_w_glu, w_in_c, w_out_c, na_rel_bias):
    n_even, n_odd = w_in_ab.shape[0], w_in_c.shape[0]
    p = {
        "norm_pre": norm_pre, "norm_post": norm_post, "w_mod": w_mod, "b_mod": b_mod,
        "w_in_ab": w_in_ab.astype(BF16), "w_out_ab": w_out_ab.astype(BF16),
        "ssm_d": ssm_d, "ssm_w_glu": ssm_w_glu.astype(BF16),
        "w_in_c": w_in_c.astype(BF16), "w_out_c": w_out_c.astype(BF16),
        "s5_ops": [_s5_operators(ssm_a_re[j], ssm_a_im[j], ssm_log_step[j], ssm_b_re[j], ssm_b_im[j],
                                 ssm_c_re[j], ssm_c_im[j]) for j in range(n_even)],
        "na_bias": [_na_bias_table(na_rel_bias[j]) for j in range(n_odd)],
    }
    return _trunk(x_prompt, c_prompt, p), _trunk(x_sample, c_sample, p)
```

```python
import functools
import math

import jax
import jax.numpy as jnp
from jax import lax
from jax.experimental import pallas as pl
from jax.experimental.pallas import tpu as pltpu

F32 = jnp.float32
BF16 = jnp.bfloat16

D_MODEL = 1024
DEPTH = 4
GRID_W = 64
D_RET = 512
RET_HEADS = 4
RET_HEAD_DIM = 128
RET_CHUNK = 128
D_SSM = 512
SSM_GROUP = 16
SSM_GROUPS = 32
SSM_STATE = 64
EVEN_IN = 4 * D_RET + 2 * D_SSM
NA_HEADS = 16
NA_HEAD_DIM = 64
NA_ROWS = 8
NA_COLS = 16
ODD_IN = 4 * D_MODEL
ROPE_BASE = 10000.0
EPS = 1e-6
NEG_INF = -1e30
LOG2E = 1.4426950408889634

S5_CHUNK = 16
S5_GROUPS_PER_STEP = 8
S5_PITCH_PAD = 8
S5_SCAN_UNROLL = 4
S5_PHASE_UNROLL = 4
TOKEN_TILE = 512
NA_ROWS_PER_STEP = 16
NA_SCORE_LEAD = 3
RET_UNROLL = 8
VMEM_LIMIT = 52 * 1024 * 1024


def _cparams(*sem):
    return pltpu.CompilerParams(dimension_semantics=sem, vmem_limit_bytes=VMEM_LIMIT)


def _sigmoid(x):
    return 1.0 / (1.0 + jnp.exp(-x))


def _silu(x):
    return x * _sigmoid(x)


def _gelu_tanh(x):
    c = math.sqrt(2.0 / math.pi)
    return 0.5 * x * (1.0 + jnp.tanh(c * (x + 0.044715 * (x * x * x))))


def _mod_kernel(c_ref, w_ref, b_ref, o_ref):
    a = _silu(c_ref[...])
    o_ref[...] = jnp.dot(a, w_ref[...], preferred_element_type=F32) + b_ref[...]


def _modulation(c, w_mod, b_mod):
    B = c.shape[0]
    tn = 1024
    return pl.pallas_call(
        _mod_kernel,
        grid=(DEPTH, 3 * D_MODEL // tn),
        in_specs=[
            pl.BlockSpec((B, D_MODEL), lambda i, n: (0, 0)),
            pl.BlockSpec((None, D_MODEL, tn), lambda i, n: (i, 0, n)),
            pl.BlockSpec((None, 1, tn), lambda i, n: (i, 0, n)),
        ],
        out_specs=pl.BlockSpec((None, B, tn), lambda i, n: (i, 0, n)),
        out_shape=jax.ShapeDtypeStruct((DEPTH, B, 3 * D_MODEL), F32),
        compiler_params=_cparams("arbitrary", "arbitrary"),
    )(c, w_mod, b_mod.reshape(DEPTH, 1, 3 * D_MODEL))


def _prenorm(x_ref, mod_ref, g_ref):
    x = x_ref[...]
    y = x * lax.rsqrt(jnp.mean(x * x, axis=-1, keepdims=True) + EPS) * g_ref[...]
    return (y * (1.0 + mod_ref[1:2, :]) + mod_ref[0:1, :]).astype(BF16)


def _even_in_kernel(x_ref, mod_ref, g_ref, w_ref, cos_ref, sin_ref,
                    q_ref, k_ref, v_ref, ga_ref, u_ref, gb_ref):
    h = _prenorm(x_ref, mod_ref, g_ref)
    cos2, sin2 = cos_ref[...], sin_ref[...]

    def proj(c):
        return jnp.dot(h, w_ref[:, c * D_RET:(c + 1) * D_RET], preferred_element_type=F32)

    def rotary(z):
        cols = []
        for hd in range(RET_HEADS):
            zh = z[:, hd * RET_HEAD_DIM:(hd + 1) * RET_HEAD_DIM]
            cols.append(zh * cos2 + pltpu.roll(zh, RET_HEAD_DIM // 2, 1) * sin2)
        return jnp.concatenate(cols, axis=1)

    q_ref[...] = rotary(proj(0)).astype(BF16)
    k_ref[...] = (rotary(proj(1)) * (RET_HEAD_DIM ** -0.5)).astype(BF16)
    v_ref[...] = proj(2).astype(BF16)
    ga_ref[...] = proj(3)
    u_ref[...] = proj(4)
    gb_ref[...] = proj(5)


def _even_in(x, mod3, g_pre, w_in, cos2, sin2):
    B, L, D = x.shape
    tm = TOKEN_TILE
    tok = lambda w: pl.BlockSpec((None, tm, w), lambda b, t: (b, t, 0))
    out = lambda dt: jax.ShapeDtypeStruct((B, L, D_RET), dt)
    return pl.pallas_call(
        _even_in_kernel,
        grid=(B, L // tm),
        in_specs=[
            tok(D),
            pl.BlockSpec((None, 3, D), lambda b, t: (b, 0, 0)),
            pl.BlockSpec((1, D), lambda b, t: (0, 0)),
            pl.BlockSpec((D, EVEN_IN), lambda b, t: (0, 0)),
            pl.BlockSpec((tm, RET_HEAD_DIM), lambda b, t: (t, 0)),
            pl.BlockSpec((tm, RET_HEAD_DIM), lambda b, t: (t, 0)),
        ],
        out_specs=[tok(D_RET)] * 6,
        out_shape=[out(BF16), out(BF16), out(BF16), out(F32), out(F32), out(F32)],
        compiler_params=_cparams("parallel", "parallel"),
    )(x, mod3, g_pre, w_in, cos2, sin2)


def _odd_in_kernel(x_ref, mod_ref, g_ref, w_ref, q_ref, k_ref, v_ref, gate_ref):
    h = _prenorm(x_ref, mod_ref, g_ref)
    half = D_MODEL // 2

    def proj(c):
        return jnp.dot(h, w_ref[:, c * half:(c + 1) * half], preferred_element_type=F32)

    for c in range(2):
        sl = slice(c * half, (c + 1) * half)
        q_ref[:, sl] = (proj(c) * (NA_HEAD_DIM ** -0.5 * LOG2E)).astype(BF16)
        k_ref[:, sl] = proj(2 + c).astype(BF16)
        v_ref[:, sl] = proj(4 + c).astype(BF16)
        gate_ref[:, sl] = proj(6 + c)


def _odd_in(x, mod3, g_pre, w_in):
    B, L, D = x.shape
    tm = TOKEN_TILE
    tok = pl.BlockSpec((None, tm, D), lambda b, t: (b, t, 0))
    out = lambda dt: jax.ShapeDtypeStruct((B, L, D), dt)
    return pl.pallas_call(
        _odd_in_kernel,
        grid=(B, L // tm),
        in_specs=[
            tok,
            pl.BlockSpec((None, 3, D), lambda b, t: (b, 0, 0)),
            pl.BlockSpec((1, D), lambda b, t: (0, 0)),
            pl.BlockSpec((D, ODD_IN), lambda b, t: (0, 0)),
        ],
        out_specs=[tok] * 4,
        out_shape=[out(BF16), out(BF16), out(BF16), out(F32)],
        compiler_params=_cparams("parallel", "parallel"),
    )(x, mod3, g_pre, w_in)


def _ret_kernel(q_ref, k_ref, v_ref, ga_ref, intra_ref, qf_ref, qb_ref, kf_ref, kb_ref, dec_ref,
                o_ref, stb_ref):
    cs = RET_CHUNK
    n = q_ref.shape[0] // cs
    dec = dec_ref[...]
    tdot = lambda a, b: lax.dot_general(a, b, (((0,), (0,)), ((), ())), preferred_element_type=F32)

    def rows(c):
        return pl.ds(pl.multiple_of(c * cs, cs), cs)

    ways = RET_UNROLL
    state0 = jnp.zeros((RET_HEAD_DIM, RET_HEAD_DIM), F32)

    def rev(it, st):
        chunks = [n - 1 - (it * ways + w) for w in range(ways)]
        kvs = [tdot((k_ref[rows(c), :].astype(F32) * kb_ref[...]).astype(BF16), v_ref[rows(c), :])
               for c in chunks]
        for c, kv in zip(chunks, kvs):
            stb_ref[c] = st.astype(BF16)
            st = dec * st + kv
        return st

    lax.fori_loop(0, n // ways, rev, state0)

    def fwd(it, st):
        chunks = [it * ways + w for w in range(ways)]
        q = [q_ref[rows(c), :] for c in chunks]
        k = [k_ref[rows(c), :] for c in chunks]
        v = [v_ref[rows(c), :] for c in chunks]
        s = [lax.dot_general(q[w], k[w], (((1,), (1,)), ((), ())), preferred_element_type=F32)
             * intra_ref[...] for w in range(ways)]
        kvs = [tdot((k[w].astype(F32) * kf_ref[...]).astype(BF16), v[w]) for w in range(ways)]
        before = []
        for kv in kvs:
            before.append(st.astype(BF16))
            st = dec * st + kv
        outs = []
        for w, c in enumerate(chunks):
            qf32 = q[w].astype(F32)
            o = jnp.dot(s[w].astype(BF16), v[w], preferred_element_type=F32)
            o += jnp.dot((qf32 * qf_ref[...]).astype(BF16), before[w], preferred_element_type=F32)
            o += jnp.dot((qf32 * qb_ref[...]).astype(BF16), stb_ref[c], preferred_element_type=F32)
            outs.append(o)
        for c, o in zip(chunks, outs):
            mu = jnp.mean(o, axis=-1, keepdims=True)
            d = o - mu
            hn = d * lax.rsqrt(jnp.mean(d * d, axis=-1, keepdims=True) + EPS)
            o_ref[rows(c), :] = (hn * _silu(ga_ref[rows(c), :])).astype(BF16)
        return st

    lax.fori_loop(0, n // ways, fwd, state0)


def _retention_tables():
    H, cs, dk = RET_HEADS, RET_CHUNK, RET_HEAD_DIM
    log_g = jnp.log1p(-jnp.exp2(-5.0 - jnp.arange(H, dtype=F32)))
    pos = jnp.arange(cs, dtype=F32)
    intra = jnp.exp(jnp.abs(pos[:, None] - pos[None, :])[None] * log_g[:, None, None])
    col = lambda e: jnp.broadcast_to(jnp.exp(e[:, None] * log_g[None]).T[:, :, None], (H, cs, dk))
    q_fwd, q_bwd = col(pos), col(cs - 1.0 - pos)
    k_fwd, k_bwd = col(cs - pos), col(pos + 1.0)
    decay = jnp.broadcast_to(jnp.exp(cs * log_g)[:, None, None], (H, dk, dk))
    return intra, q_fwd, q_bwd, k_fwd, k_bwd, decay


def _retention(q, k, v, ga):
    B, L, _ = q.shape
    dk = RET_HEAD_DIM
    seq = pl.BlockSpec((None, L, dk), lambda b, h: (b, 0, h))
    tab = pl.BlockSpec((None, RET_CHUNK, dk), lambda b, h: (h, 0, 0))
    return pl.pallas_call(
        _ret_kernel,
        grid=(B, RET_HEADS),
        in_specs=[seq, seq, seq, seq] + [tab] * 6,
        out_specs=seq,
        out_shape=jax.ShapeDtypeStruct((B, L, D_RET), BF16),
        scratch_shapes=[pltpu.VMEM((L // RET_CHUNK, dk, dk), BF16)],
        compiler_params=_cparams("parallel", "parallel"),
    )(q, k, v, ga, *_retention_tables())


def _s5_kernel(u_ref, mt_ref, wint_ref, woutt_ref, are_ref, aim_ref, y_ref,
               xt_ref, yt_ref, kvre_ref, kvim_ref, h_ref):
    T, Gi, P = S5_CHUNK, SSM_GROUP, SSM_STATE
    C = u_ref.shape[0] // T
    pitch = C + S5_PITCH_PAD
    ng = xt_ref.shape[0]

    def relayout_in(t, carry):
        a_t = u_ref[pl.ds(t, C, stride=T), :].T
        rows = pl.ds(pl.multiple_of(t * Gi, Gi), Gi)
        for g in range(ng):
            xt_ref[g, rows, :] = a_t[g * Gi:(g + 1) * Gi, :].astype(BF16)
        return carry

    lax.fori_loop(0, T, relayout_in, 0, unroll=S5_PHASE_UNROLL)

    def chunk_matmuls(g, carry):
        xt = xt_ref[g]
        yt_ref[g] = jnp.dot(mt_ref[g], xt, preferred_element_type=F32)
        kv = jnp.dot(wint_ref[g], xt, preferred_element_type=F32).T
        rows = pl.ds(pl.multiple_of(g * pitch, 8), C)
        kvre_ref[rows, :] = kv[:, :2 * P]
        kvim_ref[rows, :] = kv[:, 2 * P:]
        return carry

    lax.fori_loop(0, ng, chunk_matmuls, 0, unroll=S5_PHASE_UNROLL)

    are, aim = are_ref[...], aim_ref[...]

    def step(k, carry):
        sf_re, sf_im, sb_re, sb_im = carry
        rf = pl.ds(k, ng, stride=pitch)
        rb = pl.ds(C - 1 - k, ng, stride=pitch)
        h_ref[0, rf, :] = sf_re
        h_ref[1, rf, :] = sf_im
        h_ref[2, rb, :] = sb_re
        h_ref[3, rb, :] = sb_im
        nf_re = are * sf_re - aim * sf_im + kvre_ref[rf, :]
        nf_im = are * sf_im + aim * sf_re + kvim_ref[rf, :]
        nb_re = are * sb_re - aim * sb_im + kvre_ref[rb, :]
        nb_im = are * sb_im + aim * sb_re + kvim_ref[rb, :]
        return nf_re, nf_im, nb_re, nb_im

    z = jnp.zeros((ng, 2 * P), F32)
    lax.fori_loop(0, C, step, (z, z, z, z), unroll=S5_SCAN_UNROLL)

    def carried_outputs(g, carry):
        rows = pl.ds(pl.multiple_of(g * pitch, 8), C)
        h = jnp.concatenate([h_ref[i, rows, :] for i in range(4)], axis=1).astype(BF16)
        yt_ref[g] += lax.dot_general(woutt_ref[g], h, (((1,), (1,)), ((), ())),
                                     preferred_element_type=F32)
        return carry

    lax.fori_loop(0, ng, carried_outputs, 0, unroll=S5_PHASE_UNROLL)

    def relayout_out(t, carry):
        rows = pl.ds(pl.multiple_of(t * Gi, Gi), Gi)
        b_t = jnp.concatenate([yt_ref[g, rows, :] for g in range(ng)], axis=0)
        y_ref[pl.ds(t, C, stride=T), :] = b_t.T
        return carry

    lax.fori_loop(0, T, relayout_out, 0, unroll=S5_PHASE_UNROLL)


def _s5_operators(a_re, a_im, log_step, b_re, b_im, c_re, c_im):
    T, G, P, Gi = S5_CHUNK, SSM_GROUPS, SSM_STATE, SSM_GROUP
    hp = lax.Precision.HIGHEST
    a_re, a_im = a_re.astype(F32), a_im.astype(F32)
    delta = jnp.exp(log_step.astype(F32))[..., None]
    z_re, z_im = a_re * delta, a_im * delta
    mag = jnp.exp(z_re)
    abar_re, abar_im = mag * jnp.cos(z_im), mag * jnp.sin(z_im)
    den = a_re * a_re + a_im * a_im
    n_re, n_im = abar_re - 1.0, abar_im
    f_re = (n_re * a_re + n_im * a_im) / den
    f_im = (n_im * a_re - n_re * a_im) / den
    b_re, b_im = b_re.astype(F32), b_im.astype(F32)
    bb_re = f_re[..., None] * b_re - f_im[..., None] * b_im
    bb_im = f_re[..., None] * b_im + f_im[..., None] * b_re
    c_re, c_im = c_re.astype(F32), c_im.astype(F32)

    pr, pi = [jnp.ones_like(abar_re)], [jnp.zeros_like(abar_re)]
    for _ in range(T):
        pr, pi = (pr + [pr[-1] * abar_re - pi[-1] * abar_im],
                  pi + [pr[-1] * abar_im + pi[-1] * abar_re])
    pw_re, pw_im = jnp.stack(pr), jnp.stack(pi)

    w_re = pw_re[:T, ..., None] * bb_re[None] - pw_im[:T, ..., None] * bb_im[None]
    w_im = pw_re[:T, ..., None] * bb_im[None] + pw_im[:T, ..., None] * bb_re[None]
    kern = (jnp.einsum('xgip,dxgpj->dxgij', c_re, w_re, precision=hp)
            - jnp.einsum('xgip,dxgpj->dxgij', c_im, w_im, precision=hp))
    s_idx = jnp.arange(T)[:, None]
    t_idx = jnp.arange(T)[None, :]
    kf = jnp.where((t_idx >= s_idx)[..., None, None, None],
                   kern[jnp.clip(t_idx - s_idx, 0, T - 1), 0], 0.0)
    kb = jnp.where((s_idx >= t_idx)[..., None, None, None],
                   kern[jnp.clip(s_idx - t_idx, 0, T - 1), 1], 0.0)
    m_intra = jnp.transpose(kf + kb, (2, 0, 4, 1, 3)).reshape(G, T * Gi, T * Gi)

    def w_in(w, direction, order):
        sel = w[order, direction]
        return jnp.transpose(sel, (1, 0, 3, 2)).reshape(G, T * Gi, P)

    fwd_order = jnp.arange(T - 1, -1, -1)
    bwd_order = jnp.arange(T)
    def w_out(direction, order):
        qr, qi = pw_re[order, direction], pw_im[order, direction]
        cr, ci = c_re[direction], c_im[direction]
        wr = cr[None] * qr[:, :, None, :] - ci[None] * qi[:, :, None, :]
        wi = cr[None] * qi[:, :, None, :] + ci[None] * qr[:, :, None, :]
        fix = lambda w: jnp.transpose(w, (1, 3, 0, 2)).reshape(G, P, T * Gi)
        return fix(wr), fix(-wi)

    of_re, of_im = w_out(0, jnp.arange(1, T + 1))
    ob_re, ob_im = w_out(1, jnp.arange(T, 0, -1))
    zero = jnp.zeros_like(of_re)
    wout = jnp.concatenate([of_re, zero, of_im, zero, zero, ob_re, zero, ob_im], axis=1)

    mt = jnp.transpose(m_intra, (0, 2, 1))
    wint = jnp.transpose(jnp.concatenate([
        w_in(w_re, 0, fwd_order), w_in(w_re, 1, bwd_order),
        w_in(w_im, 0, fwd_order), w_in(w_im, 1, bwd_order)], axis=-1), (0, 2, 1))
    woutt = jnp.transpose(wout, (0, 2, 1))
    lanes = lambda p: jnp.concatenate([p[T, 0], p[T, 1]], axis=-1)
    nblk = G // S5_GROUPS_PER_STEP
    blocked = lambda w: w.reshape((nblk, S5_GROUPS_PER_STEP) + w.shape[1:])
    return (blocked(mt.astype(BF16)), blocked(wint.astype(BF16)), blocked(woutt.astype(BF16)),
            blocked(lanes(pw_re)), blocked(lanes(pw_im)))


def _s5_scan(u, ops):
    B, L, D = u.shape
    T, P, ng = S5_CHUNK, SSM_STATE, S5_GROUPS_PER_STEP
    C = L // T
    TI = T * SSM_GROUP
    lanes = ng * SSM_GROUP
    rows = ng * (C + S5_PITCH_PAD)
    seq = pl.BlockSpec((None, L, lanes), lambda b, j: (b, 0, j))
    op = lambda r, c: pl.BlockSpec((None, ng, r, c), lambda b, j: (j, 0, 0, 0))
    coef = pl.BlockSpec((None, ng, 2 * P), lambda b, j: (j, 0, 0))
    return pl.pallas_call(
        _s5_kernel,
        grid=(B, D // lanes),
        in_specs=[seq, op(TI, TI), op(TI, TI), op(TI, 2 * TI), coef, coef],
        out_specs=seq,
        out_shape=jax.ShapeDtypeStruct((B, L, D), F32),
        scratch_shapes=[pltpu.VMEM((ng, TI, C), BF16), pltpu.VMEM((ng, TI, C), F32),
                        pltpu.VMEM((rows, 2 * P), F32), pltpu.VMEM((rows, 2 * P), F32),
                        pltpu.VMEM((4, rows, 2 * P), F32)],
        compiler_params=_cparams("parallel", "parallel"),
    )(u, *ops)


def _na_kernel(q_ref, k_ref, v_ref, g_ref, bias_ref, o_ref):
    W, kr, dh = GRID_W, NA_ROWS, NA_HEAD_DIM
    n_rows = k_ref.shape[0] // W
    j = pl.program_id(2)
    first = lax.broadcasted_iota(jnp.int32, (W, 2 * dh), 1) < dh

    def scores(i):
        r = j * NA_ROWS_PER_STEP + i
        rs = jnp.clip(r - kr // 2, 0, n_rows - kr)
        var = rs - r + (NA_ROWS - 1)
        keys = pl.ds(pl.multiple_of(rs * W, W), kr * W)
        q2 = q_ref[i * W:(i + 1) * W, :]
        zero = jnp.zeros_like(q2)
        qm = jnp.concatenate([jnp.where(first, q2, zero), jnp.where(first, zero, q2)], axis=0)
        s = lax.dot_general(qm, k_ref[keys, :], (((1,), (1,)), ((), ())), preferred_element_type=F32)
        return s + bias_ref[var], keys

    def attend(i, s, keys):
        p = jnp.exp2(s - jnp.max(s, axis=-1, keepdims=True))
        den = jnp.sum(p, axis=-1, keepdims=True)
        o2 = jnp.dot(p.astype(BF16), v_ref[keys, :], preferred_element_type=F32) / den
        o = jnp.where(first, o2[:W], o2[W:])
        qrow = slice(i * W, (i + 1) * W)
        o_ref[qrow, :] = (o * _silu(g_ref[qrow, :])).astype(BF16)

    pending = [scores(i) for i in range(NA_SCORE_LEAD)]
    for i in range(NA_ROWS_PER_STEP):
        if i + NA_SCORE_LEAD < NA_ROWS_PER_STEP:
            pending.append(scores(i + NA_SCORE_LEAD))
        attend(i, *pending.pop(0))


def _na_bias_table(rel_bias):
    W, R = GRID_W, NA_ROWS
    c_idx = jnp.arange(W)
    col_start = jnp.clip(c_idx - NA_COLS // 2, 0, W - NA_COLS)
    col_valid = ((c_idx[None, :] >= col_start[:, None])
                 & (c_idx[None, :] < col_start[:, None] + NA_COLS))
    dc_idx = jnp.clip(c_idx[None, :] - c_idx[:, None] + NA_COLS - 1, 0, 2 * NA_COLS - 2)
    col_bias = jnp.take(rel_bias.astype(F32), dc_idx, axis=2)
    col_bias = jnp.where(col_valid[None, None], col_bias, NEG_INF)
    win = jnp.arange(R)[:, None] + jnp.arange(R)[None, :]
    tab = col_bias[:, win]
    tab = jnp.transpose(tab, (1, 0, 3, 2, 4)).reshape(R, NA_HEADS // 2, 2 * W, R * W)
    return tab * LOG2E


def _neighbourhood_attention(q, k, v, g, bias_tab):
    B, L, D = q.shape
    rows = L // GRID_W
    assert rows >= NA_ROWS and rows % NA_ROWS_PER_STEP == 0
    tq = NA_ROWS_PER_STEP * GRID_W
    lanes = 2 * NA_HEAD_DIM
    blk = pl.BlockSpec((None, tq, lanes), lambda b, h, j: (b, j, h))
    seq = pl.BlockSpec((None, L, lanes), lambda b, h, j: (b, 0, h))
    return pl.pallas_call(
        _na_kernel,
        grid=(B, NA_HEADS // 2, rows // NA_ROWS_PER_STEP),
        in_specs=[blk, seq, seq, blk,
                  pl.BlockSpec((NA_ROWS, None, 2 * GRID_W, NA_ROWS * GRID_W),
                               lambda b, h, j: (0, h, 0, 0))],
        out_specs=blk,
        out_shape=jax.ShapeDtypeStruct((B, L, D), BF16),
        compiler_params=_cparams("parallel", "parallel", "arbitrary"),
    )(q, k, v, g, bias_tab)


def _finish(y, x_ref, mod_ref, g_ref, o_ref):
    yn = y * lax.rsqrt(jnp.mean(y * y, axis=-1, keepdims=True) + EPS) * g_ref[...]
    o_ref[...] = x_ref[...] + mod_ref[2:3, :] * yn


def _even_out_kernel(oa_ref, ys_ref, u_ref, gb_ref, x_ref, mod_ref, g_ref, d_ref, wglu_ref, wout_ref,
                     o_ref):
    y = _gelu_tanh(ys_ref[...] + d_ref[...] * u_ref[...])
    y = y * _sigmoid(jnp.dot(y.astype(BF16), wglu_ref[...], preferred_element_type=F32))
    ob = (y * _silu(gb_ref[...])).astype(BF16)
    out = jnp.dot(oa_ref[...], wout_ref[:D_RET, :], preferred_element_type=F32)
    out += jnp.dot(ob, wout_ref[D_RET:, :], preferred_element_type=F32)
    _finish(out, x_ref, mod_ref, g_ref, o_ref)


def _even_out(oa, ys, u, gb, x, mod3, g_post, d_skip, w_glu, w_out):
    B, L, D = x.shape
    tm = TOKEN_TILE
    tok = lambda w: pl.BlockSpec((None, tm, w), lambda b, t: (b, t, 0))
    full = lambda r, c: pl.BlockSpec((r, c), lambda b, t: (0, 0))
    return pl.pallas_call(
        _even_out_kernel,
        grid=(B, L // tm),
        in_specs=[tok(D_RET), tok(D_SSM), tok(D_SSM), tok(D_SSM), tok(D),
                  pl.BlockSpec((None, 3, D), lambda b, t: (b, 0, 0)),
                  full(1, D), full(1, D_SSM), full(D_SSM, D_SSM), full(D, D)],
        out_specs=tok(D),
        out_shape=jax.ShapeDtypeStruct((B, L, D), F32),
        compiler_params=_cparams("parallel", "parallel"),
    )(oa, ys, u, gb, x, mod3, g_post, d_skip, w_glu, w_out)


def _odd_out_kernel(o_ref_in, x_ref, mod_ref, g_ref, wout_ref, o_ref):
    out = jnp.dot(o_ref_in[...], wout_ref[...], preferred_element_type=F32)
    _finish(out, x_ref, mod_ref, g_ref, o_ref)


def _odd_out(o, x, mod3, g_post, w_out):
    B, L, D = x.shape
    tm = TOKEN_TILE
    tok = pl.BlockSpec((None, tm, D), lambda b, t: (b, t, 0))
    return pl.pallas_call(
        _odd_out_kernel,
        grid=(B, L // tm),
        in_specs=[tok, tok,
                  pl.BlockSpec((None, 3, D), lambda b, t: (b, 0, 0)),
                  pl.BlockSpec((1, D), lambda b, t: (0, 0)),
                  pl.BlockSpec((D, D), lambda b, t: (0, 0))],
        out_specs=tok,
        out_shape=jax.ShapeDtypeStruct((B, L, D), F32),
        compiler_params=_cparams("parallel", "parallel"),
    )(o, x, mod3, g_post, w_out)


def _rotary_tables(L):
    dh = RET_HEAD_DIM
    inv = ROPE_BASE ** (-jnp.arange(0, dh, 2, dtype=F32) / dh)
    ang = jnp.arange(L, dtype=F32)[:, None] * inv[None, :]
    cos, sin = jnp.cos(ang), jnp.sin(ang)
    return jnp.concatenate([cos, cos], axis=1), jnp.concatenate([-sin, sin], axis=1)


def _trunk(x, c, p):
    B, L, D = x.shape
    mods = _modulation(c, p["w_mod"], p["b_mod"]).reshape(DEPTH, B, 3, D)
    cos2, sin2 = _rotary_tables(L)
    for i in range(DEPTH):
        j = i // 2
        g_pre = p["norm_pre"][i].reshape(1, D)
        g_post = p["norm_post"][i].reshape(1, D)
        if i % 2 == 0:
            q, k, v, ga, u, gb = _even_in(x, mods[i], g_pre, p["w_in_ab"][j], cos2, sin2)
            oa = _retention(q, k, v, ga)
            ys = _s5_scan(u, p["s5_ops"][j])
            x = _even_out(oa, ys, u, gb, x, mods[i], g_post, p["ssm_d"][j].reshape(1, D_SSM),
                          p["ssm_w_glu"][j], p["w_out_ab"][j])
        else:
            q, k, v, g = _odd_in(x, mods[i], g_pre, p["w_in_c"][j])
            o = _neighbourhood_attention(q, k, v, g, p["na_bias"][j])
            x = _odd_out(o, x, mods[i], g_post, p["w_out_c"][j])
    return x


def kernel(x_prompt, x_sample, c_prompt, c_sample, norm_pre, norm_post, w_mod, b_mod, w_in_ab, w_out_ab, ssm_a_re, ssm_a_im, ssm_log_step, ssm_b_re, ssm_b_im, ssm_c_re, ssm_c_im, ssm_d, ssm_w_glu, w_in_c, w_out_c, na_rel_bias):
    n_even, n_odd = w_in_ab.shape[0], w_in_c.shape[0]
    p = {
        "norm_pre": norm_pre, "norm_post": norm_post, "w_mod": w_mod, "b_mod": b_mod,
        "w_in_ab": w_in_ab.astype(BF16), "w_out_ab": w_out_ab.astype(BF16),
        "ssm_d": ssm_d, "ssm_w_glu": ssm_w_glu.astype(BF16),
        "w_in_c": w_in_c.astype(BF16), "w_out_c": w_out_c.astype(BF16),
        "s5_ops": [_s5_operators(ssm_a_re[j], ssm_a_im[j], ssm_log_step[j], ssm_b_re[j], ssm_b_im[j],
                                 ssm_c_re[j], ssm_c_im[j]) for j in range(n_even)],
        "na_bias": [_na_bias_table(na_rel_bias[j]) for j in range(n_odd)],
    }
    return _trunk(x_prompt, c_prompt, p), _trunk(x_sample, c_sample, p)
```

```python
import functools
import math

import jax
import jax.numpy as jnp
from jax import lax
from jax.experimental import pallas as pl
from jax.experimental.pallas import tpu as pltpu

F32 = jnp.float32
BF16 = jnp.bfloat16

D_MODEL = 1024
DEPTH = 4
GRID_W = 64
D_RET = 512
RET_HEADS = 4
RET_HEAD_DIM = 128
RET_CHUNK = 128
D_SSM = 512
SSM_GROUP = 16
SSM_GROUPS = 32
SSM_STATE = 64
EVEN_IN = 4 * D_RET + 2 * D_SSM
NA_HEADS = 16
NA_HEAD_DIM = 64
NA_ROWS = 8
NA_COLS = 16
ODD_IN = 4 * D_MODEL
ROPE_BASE = 10000.0
EPS = 1e-6
NEG_INF = -1e30
LOG2E = 1.4426950408889634

S5_CHUNK = 16
S5_GROUPS_PER_STEP = 8
S5_PITCH_PAD = 8
S5_SCAN_UNROLL = 4
S5_PHASE_UNROLL = 4
TOKEN_TILE = 512
NA_ROWS_PER_STEP = 16
NA_SCORE_LEAD = 3
RET_UNROLL = 8
VMEM_LIMIT = 52 * 1024 * 1024


def _cparams(*sem):
    return pltpu.CompilerParams(dimension_semantics=sem, vmem_limit_bytes=VMEM_LIMIT)


def _sigmoid(x):
    return 1.0 / (1.0 + jnp.exp(-x))


def _silu(x):
    return x * _sigmoid(x)


def _gelu_tanh(x):
    c = math.sqrt(2.0 / math.pi)
    return 0.5 * x * (1.0 + jnp.tanh(c * (x + 0.044715 * (x * x * x))))


def _mod_kernel(c_ref, w_ref, b_ref, o_ref):
    a = _silu(c_ref[...])
    o_ref[...] = jnp.dot(a, w_ref[...], preferred_element_type=F32) + b_ref[...]


def _modulation(c, w_mod, b_mod):
    B = c.shape[0]
    tn = 1024
    return pl.pallas_call(
        _mod_kernel,
        grid=(DEPTH, 3 * D_MODEL // tn),
        in_specs=[
            pl.BlockSpec((B, D_MODEL), lambda i, n: (0, 0)),
            pl.BlockSpec((None, D_MODEL, tn), lambda i, n: (i, 0, n)),
            pl.BlockSpec((None, 1, tn), lambda i, n: (i, 0, n)),
        ],
        out_specs=pl.BlockSpec((None, B, tn), lambda i, n: (i, 0, n)),
        out_shape=jax.ShapeDtypeStruct((DEPTH, B, 3 * D_MODEL), F32),
        compiler_params=_cparams("arbitrary", "arbitrary"),
    )(c, w_mod, b_mod.reshape(DEPTH, 1, 3 * D_MODEL))


def _prenorm(x, mod_ref, g_ref):
    y = x * lax.rsqrt(jnp.mean(x * x, axis=-1, keepdims=True) + EPS) * g_ref[...]
    return (y * (1.0 + mod_ref[1:2, :]) + mod_ref[0:1, :]).astype(BF16)


def _even_open(x, mod_ref, g_ref, w_ref, cos_ref, sin_ref, q_ref, k_ref, v_ref, ga_ref, u_ref, gb_ref):
    h = _prenorm(x, mod_ref, g_ref)
    cos2, sin2 = cos_ref[...], sin_ref[...]

    def proj(c):
        return jnp.dot(h, w_ref[:, c * D_RET:(c + 1) * D_RET], preferred_element_type=F32)

    def rotary(z):
        cols = []
        for hd in range(RET_HEADS):
            zh = z[:, hd * RET_HEAD_DIM:(hd + 1) * RET_HEAD_DIM]
            cols.append(zh * cos2 + pltpu.roll(zh, RET_HEAD_DIM // 2, 1) * sin2)
        return jnp.concatenate(cols, axis=1)

    q_ref[...] = rotary(proj(0)).astype(BF16)
    k_ref[...] = (rotary(proj(1)) * (RET_HEAD_DIM ** -0.5)).astype(BF16)
    v_ref[...] = proj(2).astype(BF16)
    ga_ref[...] = proj(3)
    u_ref[...] = proj(4)
    gb_ref[...] = proj(5)


def _odd_open(x, mod_ref, g_ref, w_ref, q_ref, k_ref, v_ref, gate_ref):
    h = _prenorm(x, mod_ref, g_ref)
    half = D_MODEL // 2

    def proj(c):
        return jnp.dot(h, w_ref[:, c * half:(c + 1) * half], preferred_element_type=F32)

    for c in range(2):
        sl = slice(c * half, (c + 1) * half)
        q_ref[:, sl] = (proj(c) * (NA_HEAD_DIM ** -0.5 * LOG2E)).astype(BF16)
        k_ref[:, sl] = proj(2 + c).astype(BF16)
        v_ref[:, sl] = proj(4 + c).astype(BF16)
        gate_ref[:, sl] = proj(6 + c)


def _residual(y, x_ref, mod_ref, g_ref):
    yn = y * lax.rsqrt(jnp.mean(y * y, axis=-1, keepdims=True) + EPS) * g_ref[...]
    return x_ref[...] + mod_ref[2:3, :] * yn


def _even_close(oa_ref, ys_ref, u_ref, gb_ref, x_ref, mod_ref, g_ref, d_ref, wglu_ref, wout_ref):
    y = _gelu_tanh(ys_ref[...] + d_ref[...] * u_ref[...])
    y = y * _sigmoid(jnp.dot(y.astype(BF16), wglu_ref[...], preferred_element_type=F32))
    ob = (y * _silu(gb_ref[...])).astype(BF16)
    out = jnp.dot(oa_ref[...], wout_ref[:D_RET, :], preferred_element_type=F32)
    out += jnp.dot(ob, wout_ref[D_RET:, :], preferred_element_type=F32)
    return _residual(out, x_ref, mod_ref, g_ref)


def _odd_close(o_ref, x_ref, mod_ref, g_ref, wout_ref):
    out = jnp.dot(o_ref[...], wout_ref[...], preferred_element_type=F32)
    return _residual(out, x_ref, mod_ref, g_ref)


_CLOSE = {"even": _even_close, "odd": _odd_close}
_OPEN = {"even": _even_open, "odd": _odd_open}


def _stage_kernel(*refs, close, open_, n_close, n_open):
    ins, outs = refs[:n_close + n_open], refs[n_close + n_open:]
    if close is None:
        x = ins[0][...]
    else:
        x = _CLOSE[close](*ins[:n_close])
        outs[0][...] = x
        outs = outs[1:]
    if open_ is not None:
        _OPEN[open_](x, *ins[n_close:], *outs)


def _stage(B, L, close, close_args, open_, open_args):
    D, tm = D_MODEL, TOKEN_TILE
    tok = lambda w: pl.BlockSpec((None, tm, w), lambda b, t: (b, t, 0))
    mod = pl.BlockSpec((None, 3, D), lambda b, t: (b, 0, 0))
    const = lambda r, c: pl.BlockSpec((r, c), lambda b, t: (0, 0), pipeline_mode=pl.Buffered(1))
    rope = pl.BlockSpec((tm, RET_HEAD_DIM), lambda b, t: (t, 0))
    act = lambda w, dt: jax.ShapeDtypeStruct((B, L, w), dt)
    close_specs = {
        None: [tok(D)],
        "even": [tok(D_RET), tok(D_SSM), tok(D_SSM), tok(D_SSM), tok(D), mod,
                 const(1, D), const(1, D_SSM), const(D_SSM, D_SSM), const(D, D)],
        "odd": [tok(D), tok(D), mod, const(1, D), const(D, D)],
    }[close]
    open_specs = {
        None: [],
        "even": [mod, const(1, D), const(D, EVEN_IN), rope, rope],
        "odd": [mod, const(1, D), const(D, ODD_IN)],
    }[open_]
    outs = {
        None: [],
        "even": [(D_RET, BF16)] * 3 + [(D_RET, F32)] * 3,
        "odd": [(D, BF16)] * 3 + [(D, F32)],
    }[open_]
    if close is not None:
        outs = [(D, F32)] + outs
    return pl.pallas_call(
        functools.partial(_stage_kernel, close=close, open_=open_,
                          n_close=len(close_specs), n_open=len(open_specs)),
        grid=(B, L // tm),
        in_specs=close_specs + open_specs,
        out_specs=[tok(w) for w, _ in outs],
        out_shape=[act(w, dt) for w, dt in outs],
        compiler_params=_cparams("parallel", "parallel"),
    )(*close_args, *open_args)


def _ret_kernel(q_ref, k_ref, v_ref, ga_ref, intra_ref, qf_ref, qb_ref, kf_ref, kb_ref, dec_ref,
                o_ref, stb_ref):
    cs = RET_CHUNK
    n = q_ref.shape[0] // cs
    dec = dec_ref[...]
    tdot = lambda a, b: lax.dot_general(a, b, (((0,), (0,)), ((), ())), preferred_element_type=F32)

    def rows(c):
        return pl.ds(pl.multiple_of(c * cs, cs), cs)

    ways = RET_UNROLL
    state0 = jnp.zeros((RET_HEAD_DIM, RET_HEAD_DIM), F32)

    def rev(it, st):
        chunks = [n - 1 - (it * ways + w) for w in range(ways)]
        kvs = [tdot((k_ref[rows(c), :].astype(F32) * kb_ref[...]).astype(BF16), v_ref[rows(c), :])
               for c in chunks]
        for c, kv in zip(chunks, kvs):
            stb_ref[c] = st.astype(BF16)
            st = dec * st + kv
        return st

    lax.fori_loop(0, n // ways, rev, state0)

    def fwd(it, st):
        chunks = [it * ways + w for w in range(ways)]
        q = [q_ref[rows(c), :] for c in chunks]
        k = [k_ref[rows(c), :] for c in chunks]
        v = [v_ref[rows(c), :] for c in chunks]
        s = [lax.dot_general(q[w], k[w], (((1,), (1,)), ((), ())), preferred_element_type=F32)
             * intra_ref[...] for w in range(ways)]
        kvs = [tdot((k[w].astype(F32) * kf_ref[...]).astype(BF16), v[w]) for w in range(ways)]
        before = []
        for kv in kvs:
            before.append(st.astype(BF16))
            st = dec * st + kv
        outs = []
        for w, c in enumerate(chunks):
            qf32 = q[w].astype(F32)
            o = jnp.dot(s[w].astype(BF16), v[w], preferred_element_type=F32)
            o += jnp.dot((qf32 * qf_ref[...]).astype(BF16), before[w], preferred_element_type=F32)
            o += jnp.dot((qf32 * qb_ref[...]).astype(BF16), stb_ref[c], preferred_element_type=F32)
            outs.append(o)
        for c, o in zip(chunks, outs):
            mu = jnp.mean(o, axis=-1, keepdims=True)
            d = o - mu
            hn = d * lax.rsqrt(jnp.mean(d * d, axis=-1, keepdims=True) + EPS)
            o_ref[rows(c), :] = (hn * _silu(ga_ref[rows(c), :])).astype(BF16)
        return st

    lax.fori_loop(0, n // ways, fwd, state0)


def _retention_tables():
    H, cs, dk = RET_HEADS, RET_CHUNK, RET_HEAD_DIM
    log_g = jnp.log1p(-jnp.exp2(-5.0 - jnp.arange(H, dtype=F32)))
    pos = jnp.arange(cs, dtype=F32)
    intra = jnp.exp(jnp.abs(pos[:, None] - pos[None, :])[None] * log_g[:, None, None])
    col = lambda e: jnp.broadcast_to(jnp.exp(e[:, None] * log_g[None]).T[:, :, None], (H, cs, dk))
    q_fwd, q_bwd = col(pos), col(cs - 1.0 - pos)
    k_fwd, k_bwd = col(cs - pos), col(pos + 1.0)
    decay = jnp.broadcast_to(jnp.exp(cs * log_g)[:, None, None], (H, dk, dk))
    return intra, q_fwd, q_bwd, k_fwd, k_bwd, decay


def _retention(q, k, v, ga):
    B, L, _ = q.shape
    dk = RET_HEAD_DIM
    seq = pl.BlockSpec((None, L, dk), lambda b, h: (b, 0, h))
    tab = pl.BlockSpec((None, RET_CHUNK, dk), lambda b, h: (h, 0, 0))
    return pl.pallas_call(
        _ret_kernel,
        grid=(B, RET_HEADS),
        in_specs=[seq, seq, seq, seq] + [tab] * 6,
        out_specs=seq,
        out_shape=jax.ShapeDtypeStruct((B, L, D_RET), BF16),
        scratch_shapes=[pltpu.VMEM((L // RET_CHUNK, dk, dk), BF16)],
        compiler_params=_cparams("parallel", "parallel"),
    )(q, k, v, ga, *_retention_tables())


def _s5_kernel(u_ref, mt_ref, wint_ref, woutt_ref, are_ref, aim_ref, y_ref,
               xt_ref, yt_ref, kvre_ref, kvim_ref, h_ref):
    T, Gi, P = S5_CHUNK, SSM_GROUP, SSM_STATE
    C = u_ref.shape[0] // T
    pitch = C + S5_PITCH_PAD
    ng = xt_ref.shape[0]

    def relayout_in(t, carry):
        a_t = u_ref[pl.ds(t, C, stride=T), :].T
        rows = pl.ds(pl.multiple_of(t * Gi, Gi), Gi)
        for g in range(ng):
            xt_ref[g, rows, :] = a_t[g * Gi:(g + 1) * Gi, :].astype(BF16)
        return carry

    lax.fori_loop(0, T, relayout_in, 0, unroll=S5_PHASE_UNROLL)

    def chunk_matmuls(g, carry):
        xt = xt_ref[g]
        yt_ref[g] = jnp.dot(mt_ref[g], xt, preferred_element_type=F32)
        kv = jnp.dot(wint_ref[g], xt, preferred_element_type=F32).T
        rows = pl.ds(pl.multiple_of(g * pitch, 8), C)
        kvre_ref[rows, :] = kv[:, :2 * P]
        kvim_ref[rows, :] = kv[:, 2 * P:]
        return carry

    lax.fori_loop(0, ng, chunk_matmuls, 0, unroll=S5_PHASE_UNROLL)

    are, aim = are_ref[...], aim_ref[...]

    def step(k, carry):
        sf_re, sf_im, sb_re, sb_im = carry
        rf = pl.ds(k, ng, stride=pitch)
        rb = pl.ds(C - 1 - k, ng, stride=pitch)
        h_ref[0, rf, :] = sf_re
        h_ref[1, rf, :] = sf_im
        h_ref[2, rb, :] = sb_re
        h_ref[3, rb, :] = sb_im
        nf_re = are * sf_re - aim * sf_im + kvre_ref[rf, :]
        nf_im = are * sf_im + aim * sf_re + kvim_ref[rf, :]
        nb_re = are * sb_re - aim * sb_im + kvre_ref[rb, :]
        nb_im = are * sb_im + aim * sb_re + kvim_ref[rb, :]
        return nf_re, nf_im, nb_re, nb_im

    z = jnp.zeros((ng, 2 * P), F32)
    lax.fori_loop(0, C, step, (z, z, z, z), unroll=S5_SCAN_UNROLL)

    def carried_outputs(g, carry):
        rows = pl.ds(pl.multiple_of(g * pitch, 8), C)
        h = jnp.concatenate([h_ref[i, rows, :] for i in range(4)], axis=1).astype(BF16)
        yt_ref[g] += lax.dot_general(woutt_ref[g], h, (((1,), (1,)), ((), ())),
                                     preferred_element_type=F32)
        return carry

    lax.fori_loop(0, ng, carried_outputs, 0, unroll=S5_PHASE_UNROLL)

    def relayout_out(t, carry):
        rows = pl.ds(pl.multiple_of(t * Gi, Gi), Gi)
        b_t = jnp.concatenate([yt_ref[g, rows, :] for g in range(ng)], axis=0)
        y_ref[pl.ds(t, C, stride=T), :] = b_t.T
        return carry

    lax.fori_loop(0, T, relayout_out, 0, unroll=S5_PHASE_UNROLL)


def _s5_operators(a_re, a_im, log_step, b_re, b_im, c_re, c_im):
    T, G, P, Gi = S5_CHUNK, SSM_GROUPS, SSM_STATE, SSM_GROUP
    hp = lax.Precision.HIGHEST
    a_re, a_im = a_re.astype(F32), a_im.astype(F32)
    delta = jnp.exp(log_step.astype(F32))[..., None]
    z_re, z_im = a_re * delta, a_im * delta
    mag = jnp.exp(z_re)
    abar_re, abar_im = mag * jnp.cos(z_im), mag * jnp.sin(z_im)
    den = a_re * a_re + a_im * a_im
    n_re, n_im = abar_re - 1.0, abar_im
    f_re = (n_re * a_re + n_im * a_im) / den
    f_im = (n_im * a_re - n_re * a_im) / den
    b_re, b_im = b_re.astype(F32), b_im.astype(F32)
    bb_re = f_re[..., None] * b_re - f_im[..., None] * b_im
    bb_im = f_re[..., None] * b_im + f_im[..., None] * b_re
    c_re, c_im = c_re.astype(F32), c_im.astype(F32)

    pr, pi = [jnp.ones_like(abar_re)], [jnp.zeros_like(abar_re)]
    for _ in range(T):
        pr, pi = (pr + [pr[-1] * abar_re - pi[-1] * abar_im],
                  pi + [pr[-1] * abar_im + pi[-1] * abar_re])
    pw_re, pw_im = jnp.stack(pr), jnp.stack(pi)

    w_re = pw_re[:T, ..., None] * bb_re[None] - pw_im[:T, ..., None] * bb_im[None]
    w_im = pw_re[:T, ..., None] * bb_im[None] + pw_im[:T, ..., None] * bb_re[None]
    kern = (jnp.einsum('xgip,dxgpj->dxgij', c_re, w_re, precision=hp)
            - jnp.einsum('xgip,dxgpj->dxgij', c_im, w_im, precision=hp))

    def toeplitz(kd):
        rows = [kd] + [jnp.concatenate([jnp.zeros_like(kd[:a]), kd[:T - a]], axis=0) for a in range(1, T)]
        return jnp.stack(rows)

    kf = toeplitz(kern[:, 0])
    kb = jnp.swapaxes(toeplitz(kern[:, 1]), 0, 1)
    m_intra = jnp.transpose(kf + kb, (2, 0, 4, 1, 3)).reshape(G, T * Gi, T * Gi)

    def w_in(w):
        return jnp.transpose(w, (1, 0, 3, 2)).reshape(G, T * Gi, P)

    win_re = [w_in(w_re[::-1, 0]), w_in(w_re[:, 1])]
    win_im = [w_in(w_im[::-1, 0]), w_in(w_im[:, 1])]

    def w_out(direction, qr, qi):
        cr, ci = c_re[direction], c_im[direction]
        wr = cr[None] * qr[:, :, None, :] - ci[None] * qi[:, :, None, :]
        wi = cr[None] * qi[:, :, None, :] + ci[None] * qr[:, :, None, :]
        fix = lambda w: jnp.transpose(w, (1, 3, 0, 2)).reshape(G, P, T * Gi)
        return fix(wr), fix(-wi)

    of_re, of_im = w_out(0, pw_re[1:, 0], pw_im[1:, 0])
    ob_re, ob_im = w_out(1, pw_re[:0:-1, 1], pw_im[:0:-1, 1])
    zero = jnp.zeros_like(of_re)
    wout = jnp.concatenate([of_re, zero, of_im, zero, zero, ob_re, zero, ob_im], axis=1)

    mt = jnp.transpose(m_intra, (0, 2, 1))
    wint = jnp.transpose(jnp.concatenate(win_re + win_im, axis=-1), (0, 2, 1))
    woutt = jnp.transpose(wout, (0, 2, 1))
    lanes = lambda p: jnp.concatenate([p[T, 0], p[T, 1]], axis=-1)
    nblk = G // S5_GROUPS_PER_STEP
    blocked = lambda w: w.reshape((nblk, S5_GROUPS_PER_STEP) + w.shape[1:])
    return (blocked(mt.astype(BF16)), blocked(wint.astype(BF16)), blocked(woutt.astype(BF16)),
            blocked(lanes(pw_re)), blocked(lanes(pw_im)))


def _s5_scan(u, ops):
    B, L, D = u.shape
    T, P, ng = S5_CHUNK, SSM_STATE, S5_GROUPS_PER_STEP
    C = L // T
    TI = T * SSM_GROUP
    lanes = ng * SSM_GROUP
    rows = ng * (C + S5_PITCH_PAD)
    seq = pl.BlockSpec((None, L, lanes), lambda b, j: (b, 0, j))
    op = lambda r, c: pl.BlockSpec((None, ng, r, c), lambda b, j: (j, 0, 0, 0))
    coef = pl.BlockSpec((None, ng, 2 * P), lambda b, j: (j, 0, 0))
    return pl.pallas_call(
        _s5_kernel,
        grid=(B, D // lanes),
        in_specs=[seq, op(TI, TI), op(TI, TI), op(TI, 2 * TI), coef, coef],
        out_specs=seq,
        out_shape=jax.ShapeDtypeStruct((B, L, D), F32),
        scratch_shapes=[pltpu.VMEM((ng, TI, C), BF16), pltpu.VMEM((ng, TI, C), F32),
                        pltpu.VMEM((rows, 2 * P), F32), pltpu.VMEM((rows, 2 * P), F32),
                        pltpu.VMEM((4, rows, 2 * P), F32)],
        compiler_params=_cparams("parallel", "parallel"),
    )(u, *ops)


def _na_kernel(q_ref, k_ref, v_ref, g_ref, bias_ref, o_ref):
    W, kr, dh = GRID_W, NA_ROWS, NA_HEAD_DIM
    n_rows = k_ref.shape[0] // W
    j = pl.program_id(2)
    first = lax.broadcasted_iota(jnp.int32, (W, 2 * dh), 1) < dh

    def scores(i):
        r = j * NA_ROWS_PER_STEP + i
        rs = jnp.clip(r - kr // 2, 0, n_rows - kr)
        var = rs - r + (NA_ROWS - 1)
        keys = pl.ds(pl.multiple_of(rs * W, W), kr * W)
        q2 = q_ref[i * W:(i + 1) * W, :]
        zero = jnp.zeros_like(q2)
        qm = jnp.concatenate([jnp.where(first, q2, zero), jnp.where(first, zero, q2)], axis=0)
        s = lax.dot_general(qm, k_ref[keys, :], (((1,), (1,)), ((), ())), preferred_element_type=F32)
        return s + bias_ref[var], keys

    def attend(i, s, keys):
        p = jnp.exp2(s - jnp.max(s, axis=-1, keepdims=True))
        den = jnp.sum(p, axis=-1, keepdims=True)
        o2 = jnp.dot(p.astype(BF16), v_ref[keys, :], preferred_element_type=F32) / den
        o = jnp.where(first, o2[:W], o2[W:])
        qrow = slice(i * W, (i + 1) * W)
        o_ref[qrow, :] = (o * _silu(g_ref[qrow, :])).astype(BF16)

    pending = [scores(i) for i in range(NA_SCORE_LEAD)]
    for i in range(NA_ROWS_PER_STEP):
        if i + NA_SCORE_LEAD < NA_ROWS_PER_STEP:
            pending.append(scores(i + NA_SCORE_LEAD))
        attend(i, *pending.pop(0))


def _na_bias_table(rel_bias):
    W, R = GRID_W, NA_ROWS
    c_idx = jnp.arange(W)
    col_start = jnp.clip(c_idx - NA_COLS // 2, 0, W - NA_COLS)
    col_valid = ((c_idx[None, :] >= col_start[:, None])
                 & (c_idx[None, :] < col_start[:, None] + NA_COLS))
    pad = W - NA_COLS
    padded = jnp.pad(rel_bias.astype(F32), ((0, 0), (0, 0), (pad, pad)))
    col_bias = jnp.stack([padded[:, :, NA_COLS - 1 - q + pad:NA_COLS - 1 - q + pad + W]
                          for q in range(W)], axis=2)
    col_bias = jnp.where(col_valid[None, None], col_bias, NEG_INF)
    tab = jnp.stack([col_bias[:, v:v + R] for v in range(R)], axis=1)
    tab = jnp.transpose(tab, (1, 0, 3, 2, 4)).reshape(R, NA_HEADS // 2, 2 * W, R * W)
    return tab * LOG2E


def _neighbourhood_attention(q, k, v, g, bias_tab):
    B, L, D = q.shape
    rows = L // GRID_W
    assert rows >= NA_ROWS and rows % NA_ROWS_PER_STEP == 0
    tq = NA_ROWS_PER_STEP * GRID_W
    lanes = 2 * NA_HEAD_DIM
    blk = pl.BlockSpec((None, tq, lanes), lambda b, h, j: (b, j, h))
    seq = pl.BlockSpec((None, L, lanes), lambda b, h, j: (b, 0, h))
    return pl.pallas_call(
        _na_kernel,
        grid=(B, NA_HEADS // 2, rows // NA_ROWS_PER_STEP),
        in_specs=[blk, seq, seq, blk,
                  pl.BlockSpec((NA_ROWS, None, 2 * GRID_W, NA_ROWS * GRID_W),
                               lambda b, h, j: (0, h, 0, 0))],
        out_specs=blk,
        out_shape=jax.ShapeDtypeStruct((B, L, D), BF16),
        compiler_params=_cparams("parallel", "parallel", "arbitrary"),
    )(q, k, v, g, bias_tab)


def _rotary_tables(L):
    dh = RET_HEAD_DIM
    inv = ROPE_BASE ** (-jnp.arange(0, dh, 2, dtype=F32) / dh)
    ang = jnp.arange(L, dtype=F32)[:, None] * inv[None, :]
    cos, sin = jnp.cos(ang), jnp.sin(ang)
    return jnp.concatenate([cos, cos], axis=1), jnp.concatenate([-sin, sin], axis=1)


def _trunk(x, c, p):
    B, L, D = x.shape
    mods = _modulation(c, p["w_mod"], p["b_mod"]).reshape(DEPTH, B, 3, D)
    cos2, sin2 = _rotary_tables(L)
    kind = lambda i: "even" if i % 2 == 0 else "odd"

    def open_args(i):
        g_pre = p["norm_pre"][i].reshape(1, D)
        if kind(i) == "even":
            return (mods[i], g_pre, p["w_in_ab"][i // 2], cos2, sin2)
        return (mods[i], g_pre, p["w_in_c"][i // 2])

    opened = _stage(B, L, None, (x,), kind(0), open_args(0))
    for i in range(DEPTH):
        j = i // 2
        g_post = p["norm_post"][i].reshape(1, D)
        if kind(i) == "even":
            q, k, v, ga, u, gb = opened
            oa = _retention(q, k, v, ga)
            ys = _s5_scan(u, p["s5_ops"][j])
            close_args = (oa, ys, u, gb, x, mods[i], g_post, p["ssm_d"][j].reshape(1, D_SSM),
                          p["ssm_w_glu"][j], p["w_out_ab"][j])
        else:
            q, k, v, g = opened
            o = _neighbourhood_attention(q, k, v, g, p["na_bias"][j])
            close_args = (o, x, mods[i], g_post, p["w_out_c"][j])
        last = i + 1 == DEPTH
        res = _stage(B, L, kind(i), close_args, None if last else kind(i + 1),
                     () if last else open_args(i + 1))
        x, opened = res[0], res[1:]
    return x


def kernel(x_prompt, x_sample, c_prompt, c_sample, norm_pre, norm_post, w_mod, b_mod, w_in_ab, w_out_ab, ssm_a_re, ssm_a_im, ssm_log_step, ssm_b_re, ssm_b_im, ssm_c_re, ssm_c_im, ssm_d, ssm_w_glu, w_in_c, w_out_c, na_rel_bias):
    n_even, n_odd = w_in_ab.shape[0], w_in_c.shape[0]
    p = {
        "norm_pre": norm_pre, "norm_post": norm_post, "w_mod": w_mod, "b_mod": b_mod,
        "w_in_ab": w_in_ab.astype(BF16), "w_out_ab": w_out_ab.astype(BF16),
        "ssm_d": ssm_d, "ssm_w_glu": ssm_w_glu.astype(BF16),
        "w_in_c": w_in_c.astype(BF16), "w_out_c": w_out_c.astype(BF16),
        "s5_ops": [_s5_operators(ssm_a_re[j], ssm_a_im[j], ssm_log_step[j], ssm_b_re[j], ssm_b_im[j],
                                 ssm_c_re[j], ssm_c_im[j]) for j in range(n_even)],
        "na_bias": [_na_bias_table(na_rel_bias[j]) for j in range(n_odd)],
    }
    return _trunk(x_prompt, c_prompt, p), _trunk(x_sample, c_sample, p)
```

```python
import functools
import math

import jax
import jax.numpy as jnp
from jax import lax
from jax.experimental import pallas as pl
from jax.experimental.pallas import tpu as pltpu

F32 = jnp.float32
BF16 = jnp.bfloat16

D_MODEL = 1024
DEPTH = 4
GRID_W = 64
D_RET = 512
RET_HEADS = 4
RET_HEAD_DIM = 128
RET_CHUNK = 128
D_SSM = 512
SSM_GROUP = 16
SSM_GROUPS = 32
SSM_STATE = 64
EVEN_IN = 4 * D_RET + 2 * D_SSM
NA_HEADS = 16
NA_HEAD_DIM = 64
NA_ROWS = 8
NA_COLS = 16
ODD_IN = 4 * D_MODEL
ROPE_BASE = 10000.0
EPS = 1e-6
NEG_INF = -1e30
LOG2E = 1.4426950408889634

S5_CHUNK = 16
S5_GROUPS_PER_STEP = 8
S5_PITCH_PAD = 8
S5_SCAN_UNROLL = 4
S5_PHASE_UNROLL = 4
TOKEN_TILE = 512
NA_ROWS_PER_STEP = 16
NA_SCORE_LEAD = 3
RET_UNROLL = 8
VMEM_LIMIT = 52 * 1024 * 1024


def _cparams(*sem):
    return pltpu.CompilerParams(dimension_semantics=sem, vmem_limit_bytes=VMEM_LIMIT)


def _sigmoid(x):
    return 1.0 / (1.0 + jnp.exp(-x))


def _silu(x):
    return x * _sigmoid(x)


def _gelu_tanh(x):
    c = math.sqrt(2.0 / math.pi)
    return 0.5 * x * (1.0 + jnp.tanh(c * (x + 0.044715 * (x * x * x))))


def _mod_kernel(c_ref, w_ref, b_ref, o_ref):
    a = _silu(c_ref[...])
    o_ref[...] = jnp.dot(a, w_ref[...], preferred_element_type=F32) + b_ref[...]


def _modulation(c, w_mod, b_mod):
    B = c.shape[0]
    tn = 1024
    return pl.pallas_call(
        _mod_kernel,
        grid=(DEPTH, 3 * D_MODEL // tn),
        in_specs=[
            pl.BlockSpec((B, D_MODEL), lambda i, n: (0, 0)),
            pl.BlockSpec((None, D_MODEL, tn), lambda i, n: (i, 0, n)),
            pl.BlockSpec((None, 1, tn), lambda i, n: (i, 0, n)),
        ],
        out_specs=pl.BlockSpec((None, B, tn), lambda i, n: (i, 0, n)),
        out_shape=jax.ShapeDtypeStruct((DEPTH, B, 3 * D_MODEL), F32),
        compiler_params=_cparams("arbitrary", "arbitrary"),
    )(c, w_mod, b_mod.reshape(DEPTH, 1, 3 * D_MODEL))


def _prenorm(x, mod_ref, g_ref):
    y = x * lax.rsqrt(jnp.mean(x * x, axis=-1, keepdims=True) + EPS) * g_ref[...]
    return (y * (1.0 + mod_ref[1:2, :]) + mod_ref[0:1, :]).astype(BF16)


def _even_open(x, mod_ref, g_ref, w_ref, cos_ref, sin_ref, q_ref, k_ref, v_ref, ga_ref, u_ref, gb_ref):
    h = _prenorm(x, mod_ref, g_ref)
    cos2, sin2 = cos_ref[...], sin_ref[...]

    def proj(c):
        return jnp.dot(h, w_ref[:, c * D_RET:(c + 1) * D_RET], preferred_element_type=F32)

    def rotary(z):
        cols = []
        for hd in range(RET_HEADS):
            zh = z[:, hd * RET_HEAD_DIM:(hd + 1) * RET_HEAD_DIM]
            cols.append(zh * cos2 + pltpu.roll(zh, RET_HEAD_DIM // 2, 1) * sin2)
        return jnp.concatenate(cols, axis=1)

    q_ref[...] = rotary(proj(0)).astype(BF16)
    k_ref[...] = (rotary(proj(1)) * (RET_HEAD_DIM ** -0.5)).astype(BF16)
    v_ref[...] = proj(2).astype(BF16)
    ga_ref[...] = proj(3)
    u_ref[...] = proj(4)
    gb_ref[...] = proj(5)


def _odd_open(x, mod_ref, g_ref, w_ref, q_ref, k_ref, v_ref, gate_ref):
    h = _prenorm(x, mod_ref, g_ref)
    half = D_MODEL // 2

    def proj(c):
        return jnp.dot(h, w_ref[:, c * half:(c + 1) * half], preferred_element_type=F32)

    for c in range(2):
        sl = slice(c * half, (c + 1) * half)
        q_ref[:, sl] = (proj(c) * (NA_HEAD_DIM ** -0.5 * LOG2E)).astype(BF16)
        k_ref[:, sl] = proj(2 + c).astype(BF16)
        v_ref[:, sl] = proj(4 + c).astype(BF16)
        gate_ref[:, sl] = proj(6 + c)


def _residual(y, x_ref, mod_ref, g_ref):
    yn = y * lax.rsqrt(jnp.mean(y * y, axis=-1, keepdims=True) + EPS) * g_ref[...]
    return x_ref[...] + mod_ref[2:3, :] * yn


def _even_close(oa_ref, ys_ref, u_ref, gb_ref, x_ref, mod_ref, g_ref, d_ref, wglu_ref, wout_ref):
    y = _gelu_tanh(ys_ref[...] + d_ref[...] * u_ref[...])
    y = y * _sigmoid(jnp.dot(y.astype(BF16), wglu_ref[...], preferred_element_type=F32))
    ob = (y * _silu(gb_ref[...])).astype(BF16)
    out = jnp.dot(oa_ref[...], wout_ref[:D_RET, :], preferred_element_type=F32)
    out += jnp.dot(ob, wout_ref[D_RET:, :], preferred_element_type=F32)
    return _residual(out, x_ref, mod_ref, g_ref)


def _odd_close(o_ref, x_ref, mod_ref, g_ref, wout_ref):
    out = jnp.dot(o_ref[...], wout_ref[...], preferred_element_type=F32)
    return _residual(out, x_ref, mod_ref, g_ref)


_CLOSE = {"even": _even_close, "odd": _odd_close}
_OPEN = {"even": _even_open, "odd": _odd_open}


def _stage_kernel(*refs, close, open_, n_close, n_open):
    ins, outs = refs[:n_close + n_open], refs[n_close + n_open:]
    if close is None:
        x = ins[0][...]
    else:
        x = _CLOSE[close](*ins[:n_close])
        outs[0][...] = x
        outs = outs[1:]
    if open_ is not None:
        _OPEN[open_](x, *ins[n_close:], *outs)


def _stage(B, L, close, close_args, open_, open_args):
    D, tm = D_MODEL, TOKEN_TILE
    tok = lambda w: pl.BlockSpec((None, tm, w), lambda b, t: (b, t, 0))
    mod = pl.BlockSpec((None, 3, D), lambda b, t: (b, 0, 0))
    const = lambda r, c: pl.BlockSpec((r, c), lambda b, t: (0, 0), pipeline_mode=pl.Buffered(1))
    rope = pl.BlockSpec((tm, RET_HEAD_DIM), lambda b, t: (t, 0))
    act = lambda w, dt: jax.ShapeDtypeStruct((B, L, w), dt)
    close_specs = {
        None: [tok(D)],
        "even": [tok(D_RET), tok(D_SSM), tok(D_SSM), tok(D_SSM), tok(D), mod,
                 const(1, D), const(1, D_SSM), const(D_SSM, D_SSM), const(D, D)],
        "odd": [tok(D), tok(D), mod, const(1, D), const(D, D)],
    }[close]
    open_specs = {
        None: [],
        "even": [mod, const(1, D), const(D, EVEN_IN), rope, rope],
        "odd": [mod, const(1, D), const(D, ODD_IN)],
    }[open_]
    outs = {
        None: [],
        "even": [(D_RET, BF16)] * 3 + [(D_RET, F32)] * 3,
        "odd": [(D, BF16)] * 3 + [(D, F32)],
    }[open_]
    if close is not None:
        outs = [(D, F32)] + outs
    return pl.pallas_call(
        functools.partial(_stage_kernel, close=close, open_=open_,
                          n_close=len(close_specs), n_open=len(open_specs)),
        grid=(B, L // tm),
        in_specs=close_specs + open_specs,
        out_specs=[tok(w) for w, _ in outs],
        out_shape=[act(w, dt) for w, dt in outs],
        compiler_params=_cparams("parallel", "parallel"),
    )(*close_args, *open_args)


def _ret_kernel(q_ref, k_ref, v_ref, ga_ref, intra_ref, qf_ref, qb_ref, kf_ref, kb_ref, dec_ref,
                o_ref, stb_ref):
    cs = RET_CHUNK
    n = q_ref.shape[0] // cs
    dec = dec_ref[...]
    tdot = lambda a, b: lax.dot_general(a, b, (((0,), (0,)), ((), ())), preferred_element_type=F32)

    def rows(c):
        return pl.ds(pl.multiple_of(c * cs, cs), cs)

    ways = RET_UNROLL
    state0 = jnp.zeros((RET_HEAD_DIM, RET_HEAD_DIM), F32)

    def rev(it, st):
        chunks = [n - 1 - (it * ways + w) for w in range(ways)]
        kvs = [tdot((k_ref[rows(c), :].astype(F32) * kb_ref[...]).astype(BF16), v_ref[rows(c), :])
               for c in chunks]
        for c, kv in zip(chunks, kvs):
            stb_ref[c] = st.astype(BF16)
            st = dec * st + kv
        return st

    lax.fori_loop(0, n // ways, rev, state0)

    def fwd(it, st):
        chunks = [it * ways + w for w in range(ways)]
        q = [q_ref[rows(c), :] for c in chunks]
        k = [k_ref[rows(c), :] for c in chunks]
        v = [v_ref[rows(c), :] for c in chunks]
        s = [lax.dot_general(q[w], k[w], (((1,), (1,)), ((), ())), preferred_element_type=F32)
             * intra_ref[...] for w in range(ways)]
        kvs = [tdot((k[w].astype(F32) * kf_ref[...]).astype(BF16), v[w]) for w in range(ways)]
        before = []
        for kv in kvs:
            before.append(st.astype(BF16))
            st = dec * st + kv
        outs = []
        for w, c in enumerate(chunks):
            qf32 = q[w].astype(F32)
            o = jnp.dot(s[w].astype(BF16), v[w], preferred_element_type=F32)
            o += jnp.dot((qf32 * qf_ref[...]).astype(BF16), before[w], preferred_element_type=F32)
            o += jnp.dot((qf32 * qb_ref[...]).astype(BF16), stb_ref[c], preferred_element_type=F32)
            outs.append(o)
        for c, o in zip(chunks, outs):
            mu = jnp.mean(o, axis=-1, keepdims=True)
            d = o - mu
            hn = d * lax.rsqrt(jnp.mean(d * d, axis=-1, keepdims=True) + EPS)
            o_ref[rows(c), :] = (hn * _silu(ga_ref[rows(c), :])).astype(BF16)
        return st

    lax.fori_loop(0, n // ways, fwd, state0)


def _retention_tables():
    H, cs, dk = RET_HEADS, RET_CHUNK, RET_HEAD_DIM
    log_g = jnp.log1p(-jnp.exp2(-5.0 - jnp.arange(H, dtype=F32)))
    pos = jnp.arange(cs, dtype=F32)
    intra = jnp.exp(jnp.abs(pos[:, None] - pos[None, :])[None] * log_g[:, None, None])
    col = lambda e: jnp.broadcast_to(jnp.exp(e[:, None] * log_g[None]).T[:, :, None], (H, cs, dk))
    q_fwd, q_bwd = col(pos), col(cs - 1.0 - pos)
    k_fwd, k_bwd = col(cs - pos), col(pos + 1.0)
    decay = jnp.broadcast_to(jnp.exp(cs * log_g)[:, None, None], (H, dk, dk))
    return intra, q_fwd, q_bwd, k_fwd, k_bwd, decay


def _retention(q, k, v, ga):
    B, L, _ = q.shape
    dk = RET_HEAD_DIM
    seq = pl.BlockSpec((None, L, dk), lambda b, h: (b, 0, h))
    tab = pl.BlockSpec((None, RET_CHUNK, dk), lambda b, h: (h, 0, 0))
    return pl.pallas_call(
        _ret_kernel,
        grid=(B, RET_HEADS),
        in_specs=[seq, seq, seq, seq] + [tab] * 6,
        out_specs=seq,
        out_shape=jax.ShapeDtypeStruct((B, L, D_RET), BF16),
        scratch_shapes=[pltpu.VMEM((L // RET_CHUNK, dk, dk), BF16)],
        compiler_params=_cparams("parallel", "parallel"),
    )(q, k, v, ga, *_retention_tables())


def _s5_kernel(u_ref, mt_ref, wint_ref, woutt_ref, are_ref, aim_ref, y_ref,
               xt_ref, yt_ref, kvre_ref, kvim_ref, h_ref):
    T, Gi, P = S5_CHUNK, SSM_GROUP, SSM_STATE
    C = u_ref.shape[0] // T
    pitch = C + S5_PITCH_PAD
    ng = xt_ref.shape[0]

    def relayout_in(t, carry):
        a_t = u_ref[pl.ds(t, C, stride=T), :].T
        rows = pl.ds(pl.multiple_of(t * Gi, Gi), Gi)
        for g in range(ng):
            xt_ref[g, rows, :] = a_t[g * Gi:(g + 1) * Gi, :].astype(BF16)
        return carry

    lax.fori_loop(0, T, relayout_in, 0, unroll=S5_PHASE_UNROLL)

    def chunk_matmuls(g, carry):
        xt = xt_ref[g]
        yt_ref[g] = jnp.dot(mt_ref[g], xt, preferred_element_type=F32)
        kv = jnp.dot(wint_ref[g], xt, preferred_element_type=F32).T
        rows = pl.ds(pl.multiple_of(g * pitch, 8), C)
        kvre_ref[rows, :] = kv[:, :2 * P]
        kvim_ref[rows, :] = kv[:, 2 * P:]
        return carry

    lax.fori_loop(0, ng, chunk_matmuls, 0, unroll=S5_PHASE_UNROLL)

    are, aim = are_ref[...], aim_ref[...]

    def step(k, carry):
        sf_re, sf_im, sb_re, sb_im = carry
        rf = pl.ds(k, ng, stride=pitch)
        rb = pl.ds(C - 1 - k, ng, stride=pitch)
        h_ref[0, rf, :] = sf_re
        h_ref[1, rf, :] = sf_im
        h_ref[2, rb, :] = sb_re
        h_ref[3, rb, :] = sb_im
        nf_re = are * sf_re - aim * sf_im + kvre_ref[rf, :]
        nf_im = are * sf_im + aim * sf_re + kvim_ref[rf, :]
        nb_re = are * sb_re - aim * sb_im + kvre_ref[rb, :]
        nb_im = are * sb_im + aim * sb_re + kvim_ref[rb, :]
        return nf_re, nf_im, nb_re, nb_im

    z = jnp.zeros((ng, 2 * P), F32)
    lax.fori_loop(0, C, step, (z, z, z, z), unroll=S5_SCAN_UNROLL)

    def carried_outputs(g, carry):
        rows = pl.ds(pl.multiple_of(g * pitch, 8), C)
        h = jnp.concatenate([h_ref[i, rows, :] for i in range(4)], axis=1).astype(BF16)
        yt_ref[g] += lax.dot_general(woutt_ref[g], h, (((1,), (1,)), ((), ())),
                                     preferred_element_type=F32)
        return carry

    lax.fori_loop(0, ng, carried_outputs, 0, unroll=S5_PHASE_UNROLL)

    def relayout_out(t, carry):
        rows = pl.ds(pl.multiple_of(t * Gi, Gi), Gi)
        b_t = jnp.concatenate([yt_ref[g, rows, :] for g in range(ng)], axis=0)
        y_ref[pl.ds(t, C, stride=T), :] = b_t.T
        return carry

    lax.fori_loop(0, T, relayout_out, 0, unroll=S5_PHASE_UNROLL)


def _s5_operators(a_re, a_im, log_step, b_re, b_im, c_re, c_im):
    T, G, P, Gi = S5_CHUNK, SSM_GROUPS, SSM_STATE, SSM_GROUP
    hp = lax.Precision.HIGHEST
    a_re, a_im = a_re.astype(F32), a_im.astype(F32)
    delta = jnp.exp(log_step.astype(F32))[..., None]
    z_re, z_im = a_re * delta, a_im * delta
    mag = jnp.exp(z_re)
    abar_re, abar_im = mag * jnp.cos(z_im), mag * jnp.sin(z_im)
    den = a_re * a_re + a_im * a_im
    n_re, n_im = abar_re - 1.0, abar_im
    f_re = (n_re * a_re + n_im * a_im) / den
    f_im = (n_im * a_re - n_re * a_im) / den
    b_re, b_im = b_re.astype(F32), b_im.astype(F32)
    bb_re = f_re[..., None] * b_re - f_im[..., None] * b_im
    bb_im = f_re[..., None] * b_im + f_im[..., None] * b_re
    c_re, c_im = c_re.astype(F32), c_im.astype(F32)

    pr, pi = [jnp.ones_like(abar_re)], [jnp.zeros_like(abar_re)]
    for _ in range(T):
        pr, pi = (pr + [pr[-1] * abar_re - pi[-1] * abar_im],
                  pi + [pr[-1] * abar_im + pi[-1] * abar_re])
    pw_re, pw_im = jnp.stack(pr), jnp.stack(pi)

    w_re = pw_re[:T, ..., None] * bb_re[None] - pw_im[:T, ..., None] * bb_im[None]
    w_im = pw_re[:T, ..., None] * bb_im[None] + pw_im[:T, ..., None] * bb_re[None]
    kern = (jnp.einsum('xgip,dxgpj->dxgij', c_re, w_re, precision=hp)
            - jnp.einsum('xgip,dxgpj->dxgij', c_im, w_im, precision=hp))

    lag = jnp.arange(T)[None, :] - jnp.arange(T)[:, None]
    place_f = (jnp.arange(T)[:, None, None] == lag).astype(F32)
    place_b = (jnp.arange(T)[:, None, None] == -lag).astype(F32)
    both = (jnp.einsum('dst,dgij->gsjti', place_f, kern[:, 0], precision=hp)
            + jnp.einsum('dst,dgij->gsjti', place_b, kern[:, 1], precision=hp))
    m_intra = both.reshape(G, T * Gi, T * Gi)

    def w_in(w):
        return jnp.transpose(w, (1, 0, 3, 2)).reshape(G, T * Gi, P)

    win_re = [w_in(w_re[::-1, 0]), w_in(w_re[:, 1])]
    win_im = [w_in(w_im[::-1, 0]), w_in(w_im[:, 1])]

    def w_out(direction, qr, qi):
        cr, ci = c_re[direction], c_im[direction]
        wr = cr[None] * qr[:, :, None, :] - ci[None] * qi[:, :, None, :]
        wi = cr[None] * qi[:, :, None, :] + ci[None] * qr[:, :, None, :]
        fix = lambda w: jnp.transpose(w, (1, 3, 0, 2)).reshape(G, P, T * Gi)
        return fix(wr), fix(-wi)

    of_re, of_im = w_out(0, pw_re[1:, 0], pw_im[1:, 0])
    ob_re, ob_im = w_out(1, pw_re[:0:-1, 1], pw_im[:0:-1, 1])
    zero = jnp.zeros_like(of_re)
    wout = jnp.concatenate([of_re, zero, of_im, zero, zero, ob_re, zero, ob_im], axis=1)

    mt = jnp.transpose(m_intra, (0, 2, 1))
    wint = jnp.transpose(jnp.concatenate(win_re + win_im, axis=-1), (0, 2, 1))
    woutt = jnp.transpose(wout, (0, 2, 1))
    lanes = lambda p: jnp.concatenate([p[T, 0], p[T, 1]], axis=-1)
    nblk = G // S5_GROUPS_PER_STEP
    blocked = lambda w: w.reshape((nblk, S5_GROUPS_PER_STEP) + w.shape[1:])
    return (blocked(mt.astype(BF16)), blocked(wint.astype(BF16)), blocked(woutt.astype(BF16)),
            blocked(lanes(pw_re)), blocked(lanes(pw_im)))


def _s5_scan(u, ops):
    B, L, D = u.shape
    T, P, ng = S5_CHUNK, SSM_STATE, S5_GROUPS_PER_STEP
    C = L // T
    TI = T * SSM_GROUP
    lanes = ng * SSM_GROUP
    rows = ng * (C + S5_PITCH_PAD)
    seq = pl.BlockSpec((None, L, lanes), lambda b, j: (b, 0, j))
    op = lambda r, c: pl.BlockSpec((None, ng, r, c), lambda b, j: (j, 0, 0, 0))
    coef = pl.BlockSpec((None, ng, 2 * P), lambda b, j: (j, 0, 0))
    return pl.pallas_call(
        _s5_kernel,
        grid=(B, D // lanes),
        in_specs=[seq, op(TI, TI), op(TI, TI), op(TI, 2 * TI), coef, coef],
        out_specs=seq,
        out_shape=jax.ShapeDtypeStruct((B, L, D), F32),
        scratch_shapes=[pltpu.VMEM((ng, TI, C), BF16), pltpu.VMEM((ng, TI, C), F32),
                        pltpu.VMEM((rows, 2 * P), F32), pltpu.VMEM((rows, 2 * P), F32),
                        pltpu.VMEM((4, rows, 2 * P), F32)],
        compiler_params=_cparams("parallel", "parallel"),
    )(u, *ops)


def _na_kernel(q_ref, k_ref, v_ref, g_ref, bias_ref, o_ref):
    W, kr, dh = GRID_W, NA_ROWS, NA_HEAD_DIM
    n_rows = k_ref.shape[0] // W
    j = pl.program_id(2)
    first = lax.broadcasted_iota(jnp.int32, (W, 2 * dh), 1) < dh

    def scores(i):
        r = j * NA_ROWS_PER_STEP + i
        rs = jnp.clip(r - kr // 2, 0, n_rows - kr)
        var = rs - r + (NA_ROWS - 1)
        keys = pl.ds(pl.multiple_of(rs * W, W), kr * W)
        q2 = q_ref[i * W:(i + 1) * W, :]
        zero = jnp.zeros_like(q2)
        qm = jnp.concatenate([jnp.where(first, q2, zero), jnp.where(first, zero, q2)], axis=0)
        s = lax.dot_general(qm, k_ref[keys, :], (((1,), (1,)), ((), ())), preferred_element_type=F32)
        bias = jnp.concatenate([bias_ref[var + 2 * c] for c in range(kr // 2)], axis=1)
        return s + bias, keys

    def attend(i, s, keys):
        p = jnp.exp2(s - jnp.max(s, axis=-1, keepdims=True))
        den = jnp.sum(p, axis=-1, keepdims=True)
        o2 = jnp.dot(p.astype(BF16), v_ref[keys, :], preferred_element_type=F32) / den
        o = jnp.where(first, o2[:W], o2[W:])
        qrow = slice(i * W, (i + 1) * W)
        o_ref[qrow, :] = (o * _silu(g_ref[qrow, :])).astype(BF16)

    pending = [scores(i) for i in range(NA_SCORE_LEAD)]
    for i in range(NA_ROWS_PER_STEP):
        if i + NA_SCORE_LEAD < NA_ROWS_PER_STEP:
            pending.append(scores(i + NA_SCORE_LEAD))
        attend(i, *pending.pop(0))


def _na_bias_table(rel_bias):
    W, R = GRID_W, NA_ROWS
    c_idx = jnp.arange(W)
    col_start = jnp.clip(c_idx - NA_COLS // 2, 0, W - NA_COLS)
    col_valid = ((c_idx[None, :] >= col_start[:, None])
                 & (c_idx[None, :] < col_start[:, None] + NA_COLS))
    offset = c_idx[None, None, :] - c_idx[None, :, None] + NA_COLS - 1
    onehot = (jnp.arange(2 * NA_COLS - 1)[:, None, None] == offset).astype(F32)
    col_bias = jnp.einsum('hdm,mqk->dhqk', rel_bias.astype(F32) * LOG2E, onehot,
                          precision=lax.Precision.HIGHEST)
    col_bias = jnp.where(col_valid[None, None], col_bias, NEG_INF)
    rows = col_bias.reshape(2 * R - 1, NA_HEADS // 2, 2 * W, W)
    return jnp.concatenate([rows[:-1], rows[1:]], axis=-1)


def _neighbourhood_attention(q, k, v, g, bias_tab):
    B, L, D = q.shape
    rows = L // GRID_W
    assert rows >= NA_ROWS and rows % NA_ROWS_PER_STEP == 0
    tq = NA_ROWS_PER_STEP * GRID_W
    lanes = 2 * NA_HEAD_DIM
    blk = pl.BlockSpec((None, tq, lanes), lambda b, h, j: (b, j, h))
    seq = pl.BlockSpec((None, L, lanes), lambda b, h, j: (b, 0, h))
    return pl.pallas_call(
        _na_kernel,
        grid=(B, NA_HEADS // 2, rows // NA_ROWS_PER_STEP),
        in_specs=[blk, seq, seq, blk,
                  pl.BlockSpec((2 * NA_ROWS - 2, None, 2 * GRID_W, 2 * GRID_W),
                               lambda b, h, j: (0, h, 0, 0))],
        out_specs=blk,
        out_shape=jax.ShapeDtypeStruct((B, L, D), BF16),
        compiler_params=_cparams("parallel", "parallel", "arbitrary"),
    )(q, k, v, g, bias_tab)


def _rotary_tables(L):
    dh = RET_HEAD_DIM
    inv = ROPE_BASE ** (-jnp.arange(0, dh, 2, dtype=F32) / dh)
    ang = jnp.arange(L, dtype=F32)[:, None] * inv[None, :]
    cos, sin = jnp.cos(ang), jnp.sin(ang)
    return jnp.concatenate([cos, cos], axis=1), jnp.concatenate([-sin, sin], axis=1)


def _trunk(x, c, p):
    B, L, D = x.shape
    mods = _modulation(c, p["w_mod"], p["b_mod"]).reshape(DEPTH, B, 3, D)
    cos2, sin2 = p["rope"][L]
    kind =lambda i: "even" if i % 2 == 0 else "odd"

    def open_args(i):
        g_pre = p["norm_pre"][i].reshape(1, D)
        if kind(i) == "even":
            return (mods[i], g_pre, p["w_in_ab"][i // 2], cos2, sin2)
        return (mods[i], g_pre, p["w_in_c"][i // 2])

    opened = _stage(B, L, None, (x,), kind(0), open_args(0))
    for i in range(DEPTH):
        j = i // 2
        g_post = p["norm_post"][i].reshape(1, D)
        if kind(i) == "even":
            q, k, v, ga, u, gb = opened
            oa = _retention(q, k, v, ga)
            ys = _s5_scan(u, p["s5_ops"][j])
            close_args = (oa, ys, u, gb, x, mods[i], g_post, p["ssm_d"][j].reshape(1, D_SSM),
                          p["ssm_w_glu"][j], p["w_out_ab"][j])
        else:
            q, k, v, g = opened
            o = _neighbourhood_attention(q, k, v, g, p["na_bias"][j])
            close_args = (o, x, mods[i], g_post, p["w_out_c"][j])
        last = i + 1 == DEPTH
        res = _stage(B, L, kind(i), close_args, None if last else kind(i + 1),
                     () if last else open_args(i + 1))
        x, opened = res[0], res[1:]
    return x


def kernel(x_prompt, x_sample, c_prompt, c_sample, norm_pre, norm_post, w_mod, b_mod, w_in_ab, w_out_ab, ssm_a_re, ssm_a_im, ssm_log_step, ssm_b_re, ssm_b_im, ssm_c_re, ssm_c_im, ssm_d, ssm_w_glu, w_in_c, w_out_c, na_rel_bias):
    n_even, n_odd = w_in_ab.shape[0], w_in_c.shape[0]
    p = {
        "norm_pre": norm_pre, "norm_post": norm_post, "w_mod": w_mod, "b_mod": b_mod,
        "w_in_ab": w_in_ab.astype(BF16), "w_out_ab": w_out_ab.astype(BF16),
        "ssm_d": ssm_d, "ssm_w_glu": ssm_w_glu.astype(BF16),
        "w_in_c": w_in_c.astype(BF16), "w_out_c": w_out_c.astype(BF16),
        "s5_ops": [_s5_operators(ssm_a_re[j], ssm_a_im[j], ssm_log_step[j], ssm_b_re[j], ssm_b_im[j],
                                 ssm_c_re[j], ssm_c_im[j]) for j in range(n_even)],
        "na_bias": [_na_bias_table(na_rel_bias[j]) for j in range(n_odd)],
        "rope": {L: _rotary_tables(L) for L in {x_prompt.shape[1], x_sample.shape[1]}},
    }
    return _trunk(x_prompt, c_prompt, p), _trunk(x_sample, c_sample, p)
```

```python
import functools
import math

import jax
import jax.numpy as jnp
from jax import lax
from jax.experimental import pallas as pl
from jax.experimental.pallas import tpu as pltpu

F32 = jnp.float32
BF16 = jnp.bfloat16

D_MODEL = 1024
DEPTH = 4
GRID_W = 64
D_RET = 512
RET_HEADS = 4
RET_HEAD_DIM = 128
RET_CHUNK = 128
D_SSM = 512
SSM_GROUP = 16
SSM_GROUPS = 32
SSM_STATE = 64
EVEN_IN = 4 * D_RET + 2 * D_SSM
NA_HEADS = 16
NA_HEAD_DIM = 64
NA_ROWS = 8
NA_COLS = 16
ODD_IN = 4 * D_MODEL
ROPE_BASE = 10000.0
EPS = 1e-6
NEG_INF = -1e30
LOG2E = 1.4426950408889634

S5_CHUNK = 16
S5_GROUPS_PER_STEP = 8
S5_PITCH_PAD = 8
S5_SCAN_UNROLL = 4
S5_PHASE_UNROLL = 4
TOKEN_TILE = 512
NA_ROWS_PER_STEP = 32
NA_SCORE_LEAD = 3
RET_UNROLL = 8
VMEM_LIMIT = 52 * 1024 * 1024


def _cparams(*sem):
    return pltpu.CompilerParams(dimension_semantics=sem, vmem_limit_bytes=VMEM_LIMIT)


def _sigmoid(x):
    return 1.0 / (1.0 + jnp.exp(-x))


def _silu(x):
    return x * _sigmoid(x)


def _gelu_tanh(x):
    c = math.sqrt(2.0 / math.pi)
    return 0.5 * x * (1.0 + jnp.tanh(c * (x + 0.044715 * (x * x * x))))


def _mod_kernel(c_ref, w_ref, b_ref, o_ref):
    a = _silu(c_ref[...])
    o_ref[...] = jnp.dot(a, w_ref[...], preferred_element_type=F32) + b_ref[...]


def _modulation(c, w_mod, b_mod):
    B = c.shape[0]
    tn = 1024
    return pl.pallas_call(
        _mod_kernel,
        grid=(DEPTH, 3 * D_MODEL // tn),
        in_specs=[
            pl.BlockSpec((B, D_MODEL), lambda i, n: (0, 0)),
            pl.BlockSpec((None, D_MODEL, tn), lambda i, n: (i, 0, n)),
            pl.BlockSpec((None, 1, tn), lambda i, n: (i, 0, n)),
        ],
        out_specs=pl.BlockSpec((None, B, tn), lambda i, n: (i, 0, n)),
        out_shape=jax.ShapeDtypeStruct((DEPTH, B, 3 * D_MODEL), F32),
        compiler_params=_cparams("arbitrary", "arbitrary"),
    )(c, w_mod, b_mod.reshape(DEPTH, 1, 3 * D_MODEL))


def _prenorm(x, mod_ref, g_ref):
    y = x * lax.rsqrt(jnp.mean(x * x, axis=-1, keepdims=True) + EPS) * g_ref[...]
    return (y * (1.0 + mod_ref[1:2, :]) + mod_ref[0:1, :]).astype(BF16)


def _even_open(x, mod_ref, g_ref, w_ref, cos_ref, sin_ref, q_ref, k_ref, v_ref, ga_ref, u_ref, gb_ref):
    h = _prenorm(x, mod_ref, g_ref)
    cos2, sin2 = cos_ref[...], sin_ref[...]

    def proj(c):
        return jnp.dot(h, w_ref[:, c * D_RET:(c + 1) * D_RET], preferred_element_type=F32)

    def rotary(z):
        cols = []
        for hd in range(RET_HEADS):
            zh = z[:, hd * RET_HEAD_DIM:(hd + 1) * RET_HEAD_DIM]
            cols.append(zh * cos2 + pltpu.roll(zh, RET_HEAD_DIM // 2, 1) * sin2)
        return jnp.concatenate(cols, axis=1)

    q_ref[...] = rotary(proj(0)).astype(BF16)
    k_ref[...] = (rotary(proj(1)) * (RET_HEAD_DIM ** -0.5)).astype(BF16)
    v_ref[...] = proj(2).astype(BF16)
    ga_ref[...] = proj(3)
    u_ref[...] = proj(4)
    gb_ref[...] = proj(5)


def _odd_open(x, mod_ref, g_ref, w_ref, q_ref, k_ref, v_ref, gate_ref):
    h = _prenorm(x, mod_ref, g_ref)
    half = D_MODEL // 2

    def proj(c):
        return jnp.dot(h, w_ref[:, c * half:(c + 1) * half], preferred_element_type=F32)

    for c in range(2):
        sl = slice(c * half, (c + 1) * half)
        q_ref[:, sl] = (proj(c) * (NA_HEAD_DIM ** -0.5 * LOG2E)).astype(BF16)
        k_ref[:, sl] = proj(2 + c).astype(BF16)
        v_ref[:, sl] = proj(4 + c).astype(BF16)
        gate_ref[:, sl] = proj(6 + c)


def _residual(y, x_ref, mod_ref, g_ref):
    yn = y * lax.rsqrt(jnp.mean(y * y, axis=-1, keepdims=True) + EPS) * g_ref[...]
    return x_ref[...] + mod_ref[2:3, :] * yn


def _even_close(oa_ref, ys_ref, u_ref, gb_ref, x_ref, mod_ref, g_ref, d_ref, wglu_ref, wout_ref):
    y = _gelu_tanh(ys_ref[...] + d_ref[...] * u_ref[...])
    y = y * _sigmoid(jnp.dot(y.astype(BF16), wglu_ref[...], preferred_element_type=F32))
    ob = (y * _silu(gb_ref[...])).astype(BF16)
    out = jnp.dot(oa_ref[...], wout_ref[:D_RET, :], preferred_element_type=F32)
    out += jnp.dot(ob, wout_ref[D_RET:, :], preferred_element_type=F32)
    return _residual(out, x_ref, mod_ref, g_ref)


def _odd_close(o_ref, x_ref, mod_ref, g_ref, wout_ref):
    out = jnp.dot(o_ref[...], wout_ref[...], preferred_element_type=F32)
    return _residual(out, x_ref, mod_ref, g_ref)


_CLOSE = {"even": _even_close, "odd": _odd_close}
_OPEN = {"even": _even_open, "odd": _odd_open}


def _stage_kernel(*refs, close, open_, n_close, n_open):
    ins, outs = refs[:n_close + n_open], refs[n_close + n_open:]
    if close is None:
        x = ins[0][...]
    else:
        x = _CLOSE[close](*ins[:n_close])
        outs[0][...] = x
        outs = outs[1:]
    if open_ is not None:
        _OPEN[open_](x, *ins[n_close:], *outs)


def _stage(B, L, close, close_args, open_, open_args):
    D, tm = D_MODEL, TOKEN_TILE
    tok = lambda w: pl.BlockSpec((None, tm, w), lambda b, t: (b, t, 0))
    mod = pl.BlockSpec((None, 3, D), lambda b, t: (b, 0, 0))
    const = lambda r, c: pl.BlockSpec((r, c), lambda b, t: (0, 0), pipeline_mode=pl.Buffered(1))
    rope = pl.BlockSpec((tm, RET_HEAD_DIM), lambda b, t: (t, 0))
    act = lambda w, dt: jax.ShapeDtypeStruct((B, L, w), dt)
    close_specs = {
        None: [tok(D)],
        "even": [tok(D_RET), tok(D_SSM), tok(D_SSM), tok(D_SSM), tok(D), mod,
                 const(1, D), const(1, D_SSM), const(D_SSM, D_SSM), const(D, D)],
        "odd": [tok(D), tok(D), mod, const(1, D), const(D, D)],
    }[close]
    open_specs = {
        None: [],
        "even": [mod, const(1, D), const(D, EVEN_IN), rope, rope],
        "odd": [mod, const(1, D), const(D, ODD_IN)],
    }[open_]
    outs = {
        None: [],
        "even": [(D_RET, BF16)] * 3 + [(D_RET, F32)] * 3,
        "odd": [(D, BF16)] * 3 + [(D, F32)],
    }[open_]
    if close is not None:
        outs = [(D, F32)] + outs
    return pl.pallas_call(
        functools.partial(_stage_kernel, close=close, open_=open_,
                          n_close=len(close_specs), n_open=len(open_specs)),
        grid=(B, L // tm),
        in_specs=close_specs + open_specs,
        out_specs=[tok(w) for w, _ in outs],
        out_shape=[act(w, dt) for w, dt in outs],
        compiler_params=_cparams("parallel", "parallel"),
    )(*close_args, *open_args)


def _ret_kernel(q_ref, k_ref, v_ref, ga_ref, intra_ref, qf_ref, qb_ref, kf_ref, kb_ref, dec_ref,
                o_ref, stb_ref):
    cs = RET_CHUNK
    n = q_ref.shape[0] // cs
    dec = dec_ref[...]
    tdot = lambda a, b: lax.dot_general(a, b, (((0,), (0,)), ((), ())), preferred_element_type=F32)

    def rows(c):
        return pl.ds(pl.multiple_of(c * cs, cs), cs)

    ways = RET_UNROLL
    state0 = jnp.zeros((RET_HEAD_DIM, RET_HEAD_DIM), F32)

    def rev(it, st):
        chunks = [n - 1 - (it * ways + w) for w in range(ways)]
        kvs = [tdot((k_ref[rows(c), :].astype(F32) * kb_ref[...]).astype(BF16), v_ref[rows(c), :])
               for c in chunks]
        for c, kv in zip(chunks, kvs):
            stb_ref[c] = st.astype(BF16)
            st = dec * st + kv
        return st

    lax.fori_loop(0, n // ways, rev, state0)

    def fwd(it, st):
        chunks = [it * ways + w for w in range(ways)]
        q = [q_ref[rows(c), :] for c in chunks]
        k = [k_ref[rows(c), :] for c in chunks]
        v = [v_ref[rows(c), :] for c in chunks]
        s = [lax.dot_general(q[w], k[w], (((1,), (1,)), ((), ())), preferred_element_type=F32)
             * intra_ref[...] for w in range(ways)]
        kvs = [tdot((k[w].astype(F32) * kf_ref[...]).astype(BF16), v[w]) for w in range(ways)]
        before = []
        for kv in kvs:
            before.append(st.astype(BF16))
            st = dec * st + kv
        outs = []
        for w, c in enumerate(chunks):
            qf32 = q[w].astype(F32)
            o = jnp.dot(s[w].astype(BF16), v[w], preferred_element_type=F32)
            o += jnp.dot((qf32 * qf_ref[...]).astype(BF16), before[w], preferred_element_type=F32)
            o += jnp.dot((qf32 * qb_ref[...]).astype(BF16), stb_ref[c], preferred_element_type=F32)
            outs.append(o)
        for c, o in zip(chunks, outs):
            mu = jnp.mean(o, axis=-1, keepdims=True)
            d = o - mu
            hn = d * lax.rsqrt(jnp.mean(d * d, axis=-1, keepdims=True) + EPS)
            o_ref[rows(c), :] = (hn * _silu(ga_ref[rows(c), :])).astype(BF16)
        return st

    lax.fori_loop(0, n // ways, fwd, state0)


def _retention_tables():
    H, cs, dk = RET_HEADS, RET_CHUNK, RET_HEAD_DIM
    log_g = jnp.log1p(-jnp.exp2(-5.0 - jnp.arange(H, dtype=F32)))
    pos = jnp.arange(cs, dtype=F32)
    intra = jnp.exp(jnp.abs(pos[:, None] - pos[None, :])[None] * log_g[:, None, None])
    col = lambda e: jnp.broadcast_to(jnp.exp(e[:, None] * log_g[None]).T[:, :, None], (H, cs, dk))
    q_fwd, q_bwd = col(pos), col(cs - 1.0 - pos)
    k_fwd, k_bwd = col(cs - pos), col(pos + 1.0)
    decay = jnp.broadcast_to(jnp.exp(cs * log_g)[:, None, None], (H, dk, dk))
    return intra, q_fwd, q_bwd, k_fwd, k_bwd, decay


def _retention(q, k, v, ga):
    B, L, _ = q.shape
    dk = RET_HEAD_DIM
    seq = pl.BlockSpec((None, L, dk), lambda b, h: (b, 0, h))
    tab = pl.BlockSpec((None, RET_CHUNK, dk), lambda b, h: (h, 0, 0))
    return pl.pallas_call(
        _ret_kernel,
        grid=(B, RET_HEADS),
        in_specs=[seq, seq, seq, seq] + [tab] * 6,
        out_specs=seq,
        out_shape=jax.ShapeDtypeStruct((B, L, D_RET), BF16),
        scratch_shapes=[pltpu.VMEM((L // RET_CHUNK, dk, dk), BF16)],
        compiler_params=_cparams("parallel", "parallel"),
    )(q, k, v, ga, *_retention_tables())


def _s5_kernel(u_ref, mt_ref, win_ref, woutt_ref, are_ref, aim_ref, y_ref,
               xt_ref, yt_ref, kvre_ref, kvim_ref, h_ref):
    T, Gi, P = S5_CHUNK, SSM_GROUP, SSM_STATE
    C = u_ref.shape[0] // T
    pitch = C + S5_PITCH_PAD
    ng = xt_ref.shape[0]

    def relayout_in(t, carry):
        a_t = u_ref[pl.ds(t, C, stride=T), :].T
        rows = pl.ds(pl.multiple_of(t * Gi, Gi), Gi)
        for g in range(ng):
            xt_ref[g, rows, :] = a_t[g * Gi:(g + 1) * Gi, :].astype(BF16)
        return carry

    lax.fori_loop(0, T, relayout_in, 0, unroll=S5_PHASE_UNROLL)

    def chunk_matmuls(g, carry):
        xt = xt_ref[g]
        yt_ref[g] = jnp.dot(mt_ref[g], xt, preferred_element_type=F32)
        kv = lax.dot_general(xt, win_ref[g], (((0,), (0,)), ((), ())),
                             preferred_element_type=F32)
        rows = pl.ds(pl.multiple_of(g * pitch, 8), C)
        kvre_ref[rows, :] = kv[:, :2 * P]
        kvim_ref[rows, :] = kv[:, 2 * P:]
        return carry

    lax.fori_loop(0, ng, chunk_matmuls, 0, unroll=S5_PHASE_UNROLL)

    are, aim = are_ref[...], aim_ref[...]

    def step(k, carry):
        sf_re, sf_im, sb_re, sb_im = carry
        rf = pl.ds(k, ng, stride=pitch)
        rb = pl.ds(C - 1 - k, ng, stride=pitch)
        h_ref[0, rf, :] = sf_re
        h_ref[1, rf, :] = sf_im
        h_ref[2, rb, :] = sb_re
        h_ref[3, rb, :] = sb_im
        nf_re = are * sf_re - aim * sf_im + kvre_ref[rf, :]
        nf_im = are * sf_im + aim * sf_re + kvim_ref[rf, :]
        nb_re = are * sb_re - aim * sb_im + kvre_ref[rb, :]
        nb_im = are * sb_im + aim * sb_re + kvim_ref[rb, :]
        return nf_re, nf_im, nb_re, nb_im

    z = jnp.zeros((ng, 2 * P), F32)
    lax.fori_loop(0, C, step, (z, z, z, z), unroll=S5_SCAN_UNROLL)

    def carried_outputs(g, carry):
        rows = pl.ds(pl.multiple_of(g * pitch, 8), C)
        h = jnp.concatenate([h_ref[i, rows, :] for i in range(4)], axis=1).astype(BF16)
        yt_ref[g] += lax.dot_general(woutt_ref[g], h, (((1,), (1,)), ((), ())),
                                     preferred_element_type=F32)
        return carry

    lax.fori_loop(0, ng, carried_outputs, 0, unroll=S5_PHASE_UNROLL)

    def relayout_out(t, carry):
        rows = pl.ds(pl.multiple_of(t * Gi, Gi), Gi)
        b_t = jnp.concatenate([yt_ref[g, rows, :] for g in range(ng)], axis=0)
        y_ref[pl.ds(t, C, stride=T), :] = b_t.T
        return carry

    lax.fori_loop(0, T, relayout_out, 0, unroll=S5_PHASE_UNROLL)


def _s5_operators(a_re, a_im, log_step, b_re, b_im, c_re, c_im):
    T, G, P, Gi = S5_CHUNK, SSM_GROUPS, SSM_STATE, SSM_GROUP
    hp = lax.Precision.HIGHEST
    a_re, a_im = a_re.astype(F32), a_im.astype(F32)
    delta = jnp.exp(log_step.astype(F32))[..., None]
    z_re, z_im = a_re * delta, a_im * delta
    mag = jnp.exp(z_re)
    abar_re, abar_im = mag * jnp.cos(z_im), mag * jnp.sin(z_im)
    den = a_re * a_re + a_im * a_im
    n_re, n_im = abar_re - 1.0, abar_im
    f_re = (n_re * a_re + n_im * a_im) / den
    f_im = (n_im * a_re - n_re * a_im) / den
    b_re, b_im = b_re.astype(F32), b_im.astype(F32)
    bb_re = f_re[..., None] * b_re - f_im[..., None] * b_im
    bb_im = f_re[..., None] * b_im + f_im[..., None] * b_re
    c_re, c_im = c_re.astype(F32), c_im.astype(F32)

    pr, pi = [jnp.ones_like(abar_re)], [jnp.zeros_like(abar_re)]
    for _ in range(T):
        pr, pi = (pr + [pr[-1] * abar_re - pi[-1] * abar_im],
                  pi + [pr[-1] * abar_im + pi[-1] * abar_re])
    pw_re, pw_im = jnp.stack(pr), jnp.stack(pi)

    w_re = pw_re[:T, ..., None] * bb_re[None] - pw_im[:T, ..., None] * bb_im[None]
    w_im = pw_re[:T, ..., None] * bb_im[None] + pw_im[:T, ..., None] * bb_re[None]
    kern = (jnp.einsum('xgip,dxgpj->dxgij', c_re, w_re, precision=hp)
            - jnp.einsum('xgip,dxgpj->dxgij', c_im, w_im, precision=hp))

    lag = jnp.arange(T)[None, :] - jnp.arange(T)[:, None]
    place_f = (jnp.arange(T)[:, None, None] == lag).astype(F32)
    place_b = (jnp.arange(T)[:, None, None] == -lag).astype(F32)
    both = (jnp.einsum('dst,dgij->gsjti', place_f, kern[:, 0], precision=hp)
            + jnp.einsum('dst,dgij->gsjti', place_b, kern[:, 1], precision=hp))
    m_intra = both.reshape(G, T * Gi, T * Gi)

    def w_in(w):
        return jnp.transpose(w, (1, 0, 3, 2)).reshape(G, T * Gi, P)

    win_re = [w_in(w_re[::-1, 0]), w_in(w_re[:, 1])]
    win_im = [w_in(w_im[::-1, 0]), w_in(w_im[:, 1])]

    def w_out(direction, qr, qi):
        cr, ci = c_re[direction], c_im[direction]
        wr = cr[None] * qr[:, :, None, :] - ci[None] * qi[:, :, None, :]
        wi = cr[None] * qi[:, :, None, :] + ci[None] * qr[:, :, None, :]
        fix = lambda w: jnp.transpose(w, (1, 3, 0, 2)).reshape(G, P, T * Gi)
        return fix(wr), fix(-wi)

    of_re, of_im = w_out(0, pw_re[1:, 0], pw_im[1:, 0])
    ob_re, ob_im = w_out(1, pw_re[:0:-1, 1], pw_im[:0:-1, 1])
    zero = jnp.zeros_like(of_re)
    wout = jnp.concatenate([of_re, zero, of_im, zero, zero, ob_re, zero, ob_im], axis=1)

    mt = jnp.transpose(m_intra, (0, 2, 1))
    win = jnp.concatenate(win_re + win_im, axis=-1)
    woutt = jnp.transpose(wout, (0, 2, 1))
    lanes = lambda p: jnp.concatenate([p[T, 0], p[T, 1]], axis=-1)
    nblk = G // S5_GROUPS_PER_STEP
    blocked = lambda w: w.reshape((nblk, S5_GROUPS_PER_STEP) + w.shape[1:])
    return (blocked(mt.astype(BF16)), blocked(win.astype(BF16)), blocked(woutt.astype(BF16)),
            blocked(lanes(pw_re)), blocked(lanes(pw_im)))


def _s5_scan(u, ops):
    B, L, D = u.shape
    T, P, ng = S5_CHUNK, SSM_STATE, S5_GROUPS_PER_STEP
    C = L // T
    TI = T * SSM_GROUP
    lanes = ng * SSM_GROUP
    rows = ng * (C + S5_PITCH_PAD)
    seq = pl.BlockSpec((None, L, lanes), lambda b, j: (b, 0, j))
    op = lambda r, c: pl.BlockSpec((None, ng, r, c), lambda b, j: (j, 0, 0, 0))
    coef = pl.BlockSpec((None, ng, 2 * P), lambda b, j: (j, 0, 0))
    return pl.pallas_call(
        _s5_kernel,
        grid=(B, D // lanes),
        in_specs=[seq, op(TI, TI), op(TI, TI), op(TI, 2 * TI), coef, coef],
        out_specs=seq,
        out_shape=jax.ShapeDtypeStruct((B, L, D), F32),
        scratch_shapes=[pltpu.VMEM((ng, TI, C), BF16), pltpu.VMEM((ng, TI, C), F32),
                        pltpu.VMEM((rows, 2 * P), F32), pltpu.VMEM((rows, 2 * P), F32),
                        pltpu.VMEM((4, rows, 2 * P), F32)],
        compiler_params=_cparams("parallel", "parallel"),
    )(u, *ops)


def _na_kernel(q_ref, k_ref, v_ref, g_ref, bias_ref, o_ref):
    W, kr, dh = GRID_W, NA_ROWS, NA_HEAD_DIM
    n_rows = k_ref.shape[0] // W
    j = pl.program_id(2)
    first = lax.broadcasted_iota(jnp.int32, (W, 2 * dh), 1) < dh

    def scores(i):
        r = j * NA_ROWS_PER_STEP + i
        rs = jnp.clip(r - kr // 2, 0, n_rows - kr)
        var = rs - r + (NA_ROWS - 1)
        keys = pl.ds(pl.multiple_of(rs * W, W), kr * W)
        q2 = q_ref[i * W:(i + 1) * W, :]
        zero = jnp.zeros_like(q2)
        qm = jnp.concatenate([jnp.where(first, q2, zero), jnp.where(first, zero, q2)], axis=0)
        s = lax.dot_general(qm, k_ref[keys, :], (((1,), (1,)), ((), ())), preferred_element_type=F32)
        bias = jnp.concatenate([bias_ref[var + 2 * c] for c in range(kr // 2)], axis=1)
        return s + bias, keys

    def attend(i, s, keys):
        p = jnp.exp2(s - jnp.max(s, axis=-1, keepdims=True))
        den = jnp.sum(p, axis=-1, keepdims=True)
        o2 = jnp.dot(p.astype(BF16), v_ref[keys, :], preferred_element_type=F32) / den
        o = jnp.where(first, o2[:W], o2[W:])
        qrow = slice(i * W, (i + 1) * W)
        o_ref[qrow, :] = (o * _silu(g_ref[qrow, :])).astype(BF16)

    pending = [scores(i) for i in range(NA_SCORE_LEAD)]
    for i in range(NA_ROWS_PER_STEP):
        if i + NA_SCORE_LEAD < NA_ROWS_PER_STEP:
            pending.append(scores(i + NA_SCORE_LEAD))
        attend(i, *pending.pop(0))


def _na_bias_table(rel_bias):
    W, R = GRID_W, NA_ROWS
    c_idx = jnp.arange(W)
    col_start = jnp.clip(c_idx - NA_COLS // 2, 0, W - NA_COLS)
    col_valid = ((c_idx[None, :] >= col_start[:, None])
                 & (c_idx[None, :] < col_start[:, None] + NA_COLS))
    offset = c_idx[None, None, :] - c_idx[None, :, None] + NA_COLS - 1
    onehot = (jnp.arange(2 * NA_COLS - 1)[:, None, None] == offset).astype(F32)
    col_bias = jnp.einsum('hdm,mqk->dhqk', rel_bias.astype(F32) * LOG2E, onehot,
                          precision=lax.Precision.HIGHEST)
    col_bias = jnp.where(col_valid[None, None], col_bias, NEG_INF)
    rows = col_bias.reshape(2 * R - 1, NA_HEADS // 2, 2 * W, W)
    return jnp.concatenate([rows[:-1], rows[1:]], axis=-1)


def _neighbourhood_attention(q, k, v, g, bias_tab):
    B, L, D = q.shape
    rows = L // GRID_W
    assert rows >= NA_ROWS and rows % NA_ROWS_PER_STEP == 0
    tq = NA_ROWS_PER_STEP * GRID_W
    lanes = 2 * NA_HEAD_DIM
    blk = pl.BlockSpec((None, tq, lanes), lambda b, h, j: (b, j, h))
    seq = pl.BlockSpec((None, L, lanes), lambda b, h, j: (b, 0, h))
    return pl.pallas_call(
        _na_kernel,
        grid=(B, NA_HEADS // 2, rows // NA_ROWS_PER_STEP),
        in_specs=[blk, seq, seq, blk,
                  pl.BlockSpec((2 * NA_ROWS - 2, None, 2 * GRID_W, 2 * GRID_W),
                               lambda b, h, j: (0, h, 0, 0))],
        out_specs=blk,
        out_shape=jax.ShapeDtypeStruct((B, L, D), BF16),
        compiler_params=_cparams("parallel", "parallel", "arbitrary"),
    )(q, k, v, g, bias_tab)


def _rotary_tables(L):
    dh = RET_HEAD_DIM
    inv = ROPE_BASE ** (-jnp.arange(0, dh, 2, dtype=F32) / dh)
    ang = jnp.arange(L, dtype=F32)[:, None] * inv[None, :]
    cos, sin = jnp.cos(ang), jnp.sin(ang)
    return jnp.concatenate([cos, cos], axis=1), jnp.concatenate([-sin, sin], axis=1)


def _trunk(x, c, p):
    B, L, D = x.shape
    mods = _modulation(c, p["w_mod"], p["b_mod"]).reshape(DEPTH, B, 3, D)
    cos2, sin2 = p["rope"][L]
    kind =lambda i: "even" if i % 2 == 0 else "odd"

    def open_args(i):
        g_pre = p["norm_pre"][i].reshape(1, D)
        if kind(i) == "even":
            return (mods[i], g_pre, p["w_in_ab"][i // 2], cos2, sin2)
        return (mods[i], g_pre, p["w_in_c"][i // 2])

    opened = _stage(B, L, None, (x,), kind(0), open_args(0))
    for i in range(DEPTH):
        j = i // 2
        g_post = p["norm_post"][i].reshape(1, D)
        if kind(i) == "even":
            q, k, v, ga, u, gb = opened
            oa = _retention(q, k, v, ga)
            ys = _s5_scan(u, p["s5_ops"][j])
            close_args = (oa, ys, u, gb, x, mods[i], g_post, p["ssm_d"][j].reshape(1, D_SSM),
                          p["ssm_w_glu"][j], p["w_out_ab"][j])
        else:
            q, k, v, g = opened
            o = _neighbourhood_attention(q, k, v, g, p["na_bias"][j])
            close_args = (o, x, mods[i], g_post, p["w_out_c"][j])
        last = i + 1 == DEPTH
        res = _stage(B, L, kind(i), close_args, None if last else kind(i + 1),
                     () if last else open_args(i + 1))
        x, opened = res[0], res[1:]
    return x


def kernel(x_prompt, x_sample, c_prompt, c_sample, norm_pre, norm_post, w_mod, b_mod, w_in_ab, w_out_ab, ssm_a_re, ssm_a_im, ssm_log_step, ssm_b_re, ssm_b_im, ssm_c_re, ssm_c_im, ssm_d, ssm_w_glu, w_in_c, w_out_c, na_rel_bias):
    n_even, n_odd = w_in_ab.shape[0], w_in_c.shape[0]
    p = {
        "norm_pre": norm_pre, "norm_post": norm_post, "w_mod": w_mod, "b_mod": b_mod,
        "w_in_ab": w_in_ab.astype(BF16), "w_out_ab": w_out_ab.astype(BF16),
        "ssm_d": ssm_d, "ssm_w_glu": ssm_w_glu.astype(BF16),
        "w_in_c": w_in_c.astype(BF16), "w_out_c": w_out_c.astype(BF16),
        "s5_ops": [_s5_operators(ssm_a_re[j], ssm_a_im[j], ssm_log_step[j], ssm_b_re[j], ssm_b_im[j],
                                 ssm_c_re[j], ssm_c_im[j]) for j in range(n_even)],
        "na_bias": [_na_bias_table(na_rel_bias[j]) for j in range(n_odd)],
        "rope": {L: _rotary_tables(L) for L in {x_prompt.shape[1], x_sample.shape[1]}},
    }
    return _trunk(x_prompt, c_prompt, p), _trunk(x_sample, c_sample, p)
```

```python
import functools
import math

import jax
import jax.numpy as jnp
from jax import lax
from jax.experimental import pallas as pl
from jax.experimental.pallas import tpu as pltpu

F32 = jnp.float32
BF16 = jnp.bfloat16

D_MODEL = 1024
DEPTH = 4
GRID_W = 64
D_RET = 512
RET_HEADS = 4
RET_HEAD_DIM = 128
RET_CHUNK = 128
D_SSM = 512
SSM_GROUP = 16
SSM_GROUPS = 32
SSM_STATE = 64
EVEN_IN = 4 * D_RET + 2 * D_SSM
NA_HEADS = 16
NA_HEAD_DIM = 64
NA_ROWS = 8
NA_COLS = 16
ODD_IN = 4 * D_MODEL
ROPE_BASE = 10000.0
EPS = 1e-6
NEG_INF = -1e30
LOG2E = 1.4426950408889634

S5_CHUNK = 16
S5_GROUPS_PER_STEP = 8
S5_PITCH_PAD = 8
S5_SCAN_UNROLL = 4
S5_PHASE_UNROLL = 4
TOKEN_TILE = 512
NA_ROWS_PER_STEP = 64
NA_SCORE_LEAD = 3
RET_UNROLL = 8
VMEM_LIMIT = 52 * 1024 * 1024


def _cparams(*sem):
    return pltpu.CompilerParams(dimension_semantics=sem, vmem_limit_bytes=VMEM_LIMIT)


def _sigmoid(x):
    return 1.0 / (1.0 + jnp.exp(-x))


def _silu(x):
    return x * _sigmoid(x)


def _gelu_tanh(x):
    c = math.sqrt(2.0 / math.pi)
    return 0.5 * x * (1.0 + jnp.tanh(c * (x + 0.044715 * (x * x * x))))


def _mod_kernel(c_ref, w_ref, b_ref, o_ref):
    a = _silu(c_ref[...])
    o_ref[...] = jnp.dot(a, w_ref[...], preferred_element_type=F32) + b_ref[...]


def _modulation(c, w_mod, b_mod):
    B = c.shape[0]
    tn = 1024
    return pl.pallas_call(
        _mod_kernel,
        grid=(DEPTH, 3 * D_MODEL // tn),
        in_specs=[
            pl.BlockSpec((B, D_MODEL), lambda i, n: (0, 0)),
            pl.BlockSpec((None, D_MODEL, tn), lambda i, n: (i, 0, n)),
            pl.BlockSpec((None, 1, tn), lambda i, n: (i, 0, n)),
        ],
        out_specs=pl.BlockSpec((None, B, tn), lambda i, n: (i, 0, n)),
        out_shape=jax.ShapeDtypeStruct((DEPTH, B, 3 * D_MODEL), F32),
        compiler_params=_cparams("arbitrary", "arbitrary"),
    )(c, w_mod, b_mod.reshape(DEPTH, 1, 3 * D_MODEL))


def _prenorm(x, mod_ref, g_ref):
    y = x * lax.rsqrt(jnp.mean(x * x, axis=-1, keepdims=True) + EPS) * g_ref[...]
    return (y * (1.0 + mod_ref[1:2, :]) + mod_ref[0:1, :]).astype(BF16)


def _even_open(x, mod_ref, g_ref, w_ref, cos_ref, sin_ref, q_ref, k_ref, v_ref, ga_ref, u_ref, gb_ref):
    h = _prenorm(x, mod_ref, g_ref)
    cos2, sin2 = cos_ref[...], sin_ref[...]

    def proj(c):
        return jnp.dot(h, w_ref[:, c * D_RET:(c + 1) * D_RET], preferred_element_type=F32)

    def rotary(z):
        cols = []
        for hd in range(RET_HEADS):
            zh = z[:, hd * RET_HEAD_DIM:(hd + 1) * RET_HEAD_DIM]
            cols.append(zh * cos2 + pltpu.roll(zh, RET_HEAD_DIM // 2, 1) * sin2)
        return jnp.concatenate(cols, axis=1)

    q_ref[...] = rotary(proj(0)).astype(BF16)
    k_ref[...] = (rotary(proj(1)) * (RET_HEAD_DIM ** -0.5)).astype(BF16)
    v_ref[...] = proj(2).astype(BF16)
    ga_ref[...] = proj(3)
    u_ref[...] = proj(4)
    gb_ref[...] = proj(5)


def _odd_open(x, mod_ref, g_ref, w_ref, q_ref, k_ref, v_ref, gate_ref):
    h = _prenorm(x, mod_ref, g_ref)
    half = D_MODEL // 2

    def proj(c):
        return jnp.dot(h, w_ref[:, c * half:(c + 1) * half], preferred_element_type=F32)

    for c in range(2):
        sl = slice(c * half, (c + 1) * half)
        q_ref[:, sl] = (proj(c) * (NA_HEAD_DIM ** -0.5 * LOG2E)).astype(BF16)
        k_ref[:, sl] = proj(2 + c).astype(BF16)
        v_ref[:, sl] = proj(4 + c).astype(BF16)
        gate_ref[:, sl] = proj(6 + c)


def _residual(y, x_ref, mod_ref, g_ref):
    yn = y * lax.rsqrt(jnp.mean(y * y, axis=-1, keepdims=True) + EPS) * g_ref[...]
    return x_ref[...] + mod_ref[2:3, :] * yn


def _even_close(oa_ref, ys_ref, u_ref, gb_ref, x_ref, mod_ref, g_ref, d_ref, wglu_ref, wout_ref):
    y = _gelu_tanh(ys_ref[...] + d_ref[...] * u_ref[...])
    y = y * _sigmoid(jnp.dot(y.astype(BF16), wglu_ref[...], preferred_element_type=F32))
    ob = (y * _silu(gb_ref[...])).astype(BF16)
    out = jnp.dot(jnp.concatenate([oa_ref[...], ob], axis=1), wout_ref[...], preferred_element_type=F32)
    return _residual(out, x_ref, mod_ref, g_ref)


def _odd_close(o_ref, x_ref, mod_ref, g_ref, wout_ref):
    out = jnp.dot(o_ref[...], wout_ref[...], preferred_element_type=F32)
    return _residual(out, x_ref, mod_ref, g_ref)


_CLOSE = {"even": _even_close, "odd": _odd_close}
_OPEN = {"even": _even_open, "odd": _odd_open}


def _stage_kernel(*refs, close, open_, n_close, n_open):
    ins, outs = refs[:n_close + n_open], refs[n_close + n_open:]
    if close is None:
        x = ins[0][...]
    else:
        x = _CLOSE[close](*ins[:n_close])
        outs[0][...] = x
        outs = outs[1:]
    if open_ is not None:
        _OPEN[open_](x, *ins[n_close:], *outs)


def _stage(B, L, close, close_args, open_, open_args):
    D, tm = D_MODEL, TOKEN_TILE
    tok = lambda w: pl.BlockSpec((None, tm, w), lambda b, t: (b, t, 0))
    mod = pl.BlockSpec((None, 3, D), lambda b, t: (b, 0, 0))
    const = lambda r, c: pl.BlockSpec((r, c), lambda b, t: (0, 0), pipeline_mode=pl.Buffered(1))
    rope = pl.BlockSpec((tm, RET_HEAD_DIM), lambda b, t: (t, 0))
    act = lambda w, dt: jax.ShapeDtypeStruct((B, L, w), dt)
    close_specs = {
        None: [tok(D)],
        "even": [tok(D_RET), tok(D_SSM), tok(D_SSM), tok(D_SSM), tok(D), mod,
                 const(1, D), const(1, D_SSM), const(D_SSM, D_SSM), const(D, D)],
        "odd": [tok(D), tok(D), mod, const(1, D), const(D, D)],
    }[close]
    open_specs = {
        None: [],
        "even": [mod, const(1, D), const(D, EVEN_IN), rope, rope],
        "odd": [mod, const(1, D), const(D, ODD_IN)],
    }[open_]
    outs = {
        None: [],
        "even": [(D_RET, BF16)] * 3 + [(D_RET, F32)] * 3,
        "odd": [(D, BF16)] * 3 + [(D, F32)],
    }[open_]
    if close is not None:
        outs = [(D, F32)] + outs
    return pl.pallas_call(
        functools.partial(_stage_kernel, close=close, open_=open_,
                          n_close=len(close_specs), n_open=len(open_specs)),
        grid=(B, L // tm),
        in_specs=close_specs + open_specs,
        out_specs=[tok(w) for w, _ in outs],
        out_shape=[act(w, dt) for w, dt in outs],
        compiler_params=_cparams("parallel", "parallel"),
    )(*close_args, *open_args)


def _ret_kernel(q_ref, k_ref, v_ref, ga_ref, intra_ref, qf_ref, qb_ref, kf_ref, kb_ref, dec_ref,
                o_ref, stb_ref):
    cs = RET_CHUNK
    n = q_ref.shape[0] // cs
    dec = dec_ref[...]
    tdot = lambda a, b: lax.dot_general(a, b, (((0,), (0,)), ((), ())), preferred_element_type=F32)

    def rows(c):
        return pl.ds(pl.multiple_of(c * cs, cs), cs)

    ways = RET_UNROLL
    state0 = jnp.zeros((RET_HEAD_DIM, RET_HEAD_DIM), F32)

    def rev(it, st):
        chunks = [n - 1 - (it * ways + w) for w in range(ways)]
        kvs = [tdot((k_ref[rows(c), :].astype(F32) * kb_ref[...]).astype(BF16), v_ref[rows(c), :])
               for c in chunks]
        for c, kv in zip(chunks, kvs):
            stb_ref[c] = st.astype(BF16)
            st = dec * st + kv
        return st

    lax.fori_loop(0, n // ways, rev, state0)

    def fwd(it, st):
        chunks = [it * ways + w for w in range(ways)]
        q = [q_ref[rows(c), :] for c in chunks]
        k = [k_ref[rows(c), :] for c in chunks]
        v = [v_ref[rows(c), :] for c in chunks]
        s = [lax.dot_general(q[w], k[w], (((1,), (1,)), ((), ())), preferred_element_type=F32)
             * intra_ref[...] for w in range(ways)]
        kvs = [tdot((k[w].astype(F32) * kf_ref[...]).astype(BF16), v[w]) for w in range(ways)]
        before = []
        for kv in kvs:
            before.append(st.astype(BF16))
            st = dec * st + kv
        outs = []
        for w, c in enumerate(chunks):
            qf32 = q[w].astype(F32)
            lhs = jnp.concatenate([s[w].astype(BF16), (qf32 * qf_ref[...]).astype(BF16),
                                   (qf32 * qb_ref[...]).astype(BF16)], axis=1)
            rhs = jnp.concatenate([v[w], before[w], stb_ref[c]], axis=0)
            outs.append(jnp.dot(lhs, rhs, preferred_element_type=F32))
        for c, o in zip(chunks, outs):
            mu = jnp.mean(o, axis=-1, keepdims=True)
            d = o - mu
            hn = d * lax.rsqrt(jnp.mean(d * d, axis=-1, keepdims=True) + EPS)
            o_ref[rows(c), :] = (hn * _silu(ga_ref[rows(c), :])).astype(BF16)
        return st

    lax.fori_loop(0, n // ways, fwd, state0)


def _retention_tables():
    H, cs, dk = RET_HEADS, RET_CHUNK, RET_HEAD_DIM
    log_g = jnp.log1p(-jnp.exp2(-5.0 - jnp.arange(H, dtype=F32)))
    pos = jnp.arange(cs, dtype=F32)
    intra = jnp.exp(jnp.abs(pos[:, None] - pos[None, :])[None] * log_g[:, None, None])
    col = lambda e: jnp.broadcast_to(jnp.exp(e[:, None] * log_g[None]).T[:, :, None], (H, cs, dk))
    q_fwd, q_bwd = col(pos), col(cs - 1.0 - pos)
    k_fwd, k_bwd = col(cs - pos), col(pos + 1.0)
    decay = jnp.broadcast_to(jnp.exp(cs * log_g)[:, None, None], (H, dk, dk))
    return intra, q_fwd, q_bwd, k_fwd, k_bwd, decay


def _retention(q, k, v, ga):
    B, L, _ = q.shape
    dk = RET_HEAD_DIM
    seq = pl.BlockSpec((None, L, dk), lambda b, h: (b, 0, h))
    tab = pl.BlockSpec((None, RET_CHUNK, dk), lambda b, h: (h, 0, 0))
    return pl.pallas_call(
        _ret_kernel,
        grid=(B, RET_HEADS),
        in_specs=[seq, seq, seq, seq] + [tab] * 6,
        out_specs=seq,
        out_shape=jax.ShapeDtypeStruct((B, L, D_RET), BF16),
        scratch_shapes=[pltpu.VMEM((L // RET_CHUNK, dk, dk), BF16)],
        compiler_params=_cparams("parallel", "parallel"),
    )(q, k, v, ga, *_retention_tables())


def _s5_kernel(u_ref, mt_ref, win_ref, woutt_ref, are_ref, aim_ref, y_ref,
               xt_ref, yt_ref, kvre_ref, kvim_ref, h_ref):
    T, Gi, P = S5_CHUNK, SSM_GROUP, SSM_STATE
    C = u_ref.shape[0] // T
    pitch = C + S5_PITCH_PAD
    ng = xt_ref.shape[0]

    def relayout_in(t, carry):
        a_t = u_ref[pl.ds(t, C, stride=T), :].T
        rows = pl.ds(pl.multiple_of(t * Gi, Gi), Gi)
        for g in range(ng):
            xt_ref[g, rows, :] = a_t[g * Gi:(g + 1) * Gi, :].astype(BF16)
        return carry

    lax.fori_loop(0, T, relayout_in, 0, unroll=S5_PHASE_UNROLL)

    def chunk_matmuls(g, carry):
        xt = xt_ref[g]
        yt_ref[g] = jnp.dot(mt_ref[g], xt, preferred_element_type=F32)
        kv = lax.dot_general(xt, win_ref[g], (((0,), (0,)), ((), ())),
                             preferred_element_type=F32)
        rows = pl.ds(pl.multiple_of(g * pitch, 8), C)
        kvre_ref[rows, :] = kv[:, :2 * P]
        kvim_ref[rows, :] = kv[:, 2 * P:]
        return carry

    lax.fori_loop(0, ng, chunk_matmuls, 0, unroll=S5_PHASE_UNROLL)

    are, aim = are_ref[...], aim_ref[...]

    def step(k, carry):
        sf_re, sf_im, sb_re, sb_im = carry
        rf = pl.ds(k, ng, stride=pitch)
        rb = pl.ds(C - 1 - k, ng, stride=pitch)
        h_ref[0, rf, :] = sf_re
        h_ref[1, rf, :] = sf_im
        h_ref[2, rb, :] = sb_re
        h_ref[3, rb, :] = sb_im
        nf_re = are * sf_re - aim * sf_im + kvre_ref[rf, :]
        nf_im = are * sf_im + aim * sf_re + kvim_ref[rf, :]
        nb_re = are * sb_re - aim * sb_im + kvre_ref[rb, :]
        nb_im = are * sb_im + aim * sb_re + kvim_ref[rb, :]
        return nf_re, nf_im, nb_re, nb_im

    z = jnp.zeros((ng, 2 * P), F32)
    lax.fori_loop(0, C, step, (z, z, z, z), unroll=S5_SCAN_UNROLL)

    def carried_outputs(g, carry):
        rows = pl.ds(pl.multiple_of(g * pitch, 8), C)
        h = jnp.concatenate([h_ref[i, rows, :] for i in range(4)], axis=1).astype(BF16)
        yt_ref[g] += lax.dot_general(woutt_ref[g], h, (((1,), (1,)), ((), ())),
                                     preferred_element_type=F32)
        return carry

    lax.fori_loop(0, ng, carried_outputs, 0, unroll=S5_PHASE_UNROLL)

    def relayout_out(t, carry):
        rows = pl.ds(pl.multiple_of(t * Gi, Gi), Gi)
        b_t = jnp.concatenate([yt_ref[g, rows, :] for g in range(ng)], axis=0)
        y_ref[pl.ds(t, C, stride=T), :] = b_t.T
        return carry

    lax.fori_loop(0, T, relayout_out, 0, unroll=S5_PHASE_UNROLL)


def _s5_operators(a_re, a_im, log_step, b_re, b_im, c_re, c_im):
    T, G, P, Gi = S5_CHUNK, SSM_GROUPS, SSM_STATE, SSM_GROUP
    hp = lax.Precision.HIGHEST
    a_re, a_im = a_re.astype(F32), a_im.astype(F32)
    delta = jnp.exp(log_step.astype(F32))[..., None]
    z_re, z_im = a_re * delta, a_im * delta
    mag = jnp.exp(z_re)
    abar_re, abar_im = mag * jnp.cos(z_im), mag * jnp.sin(z_im)
    den = a_re * a_re + a_im * a_im
    n_re, n_im = abar_re - 1.0, abar_im
    f_re = (n_re * a_re + n_im * a_im) / den
    f_im = (n_im * a_re - n_re * a_im) / den
    b_re, b_im = b_re.astype(F32), b_im.astype(F32)
    bb_re = f_re[..., None] * b_re - f_im[..., None] * b_im
    bb_im = f_re[..., None] * b_im + f_im[..., None] * b_re
    c_re, c_im = c_re.astype(F32), c_im.astype(F32)

    pr, pi = [jnp.ones_like(abar_re)], [jnp.zeros_like(abar_re)]
    for _ in range(T):
        pr, pi = (pr + [pr[-1] * abar_re - pi[-1] * abar_im],
                  pi + [pr[-1] * abar_im + pi[-1] * abar_re])
    pw_re, pw_im = jnp.stack(pr), jnp.stack(pi)

    w_re = pw_re[:T, ..., None] * bb_re[None] - pw_im[:T, ..., None] * bb_im[None]
    w_im = pw_re[:T, ..., None] * bb_im[None] + pw_im[:T, ..., None] * bb_re[None]
    kern = (jnp.einsum('xgip,dxgpj->dxgij', c_re, w_re, precision=hp)
            - jnp.einsum('xgip,dxgpj->dxgij', c_im, w_im, precision=hp))

    lag = jnp.arange(T)[None, :] - jnp.arange(T)[:, None]
    place_f = (jnp.arange(T)[:, None, None] == lag).astype(F32)
    place_b = (jnp.arange(T)[:, None, None] == -lag).astype(F32)
    both = (jnp.einsum('dst,dgij->gsjti', place_f, kern[:, 0], precision=hp)
            + jnp.einsum('dst,dgij->gsjti', place_b, kern[:, 1], precision=hp))
    m_intra = both.reshape(G, T * Gi, T * Gi)

    def w_in(w):
        return jnp.transpose(w, (1, 0, 3, 2)).reshape(G, T * Gi, P)

    win_re = [w_in(w_re[::-1, 0]), w_in(w_re[:, 1])]
    win_im = [w_in(w_im[::-1, 0]), w_in(w_im[:, 1])]

    def w_out(direction, qr, qi):
        cr, ci = c_re[direction], c_im[direction]
        wr = cr[None] * qr[:, :, None, :] - ci[None] * qi[:, :, None, :]
        wi = cr[None] * qi[:, :, None, :] + ci[None] * qr[:, :, None, :]
        fix = lambda w: jnp.transpose(w, (1, 3, 0, 2)).reshape(G, P, T * Gi)
        return fix(wr), fix(-wi)

    of_re, of_im = w_out(0, pw_re[1:, 0], pw_im[1:, 0])
    ob_re, ob_im = w_out(1, pw_re[:0:-1, 1], pw_im[:0:-1, 1])
    zero = jnp.zeros_like(of_re)
    wout = jnp.concatenate([of_re, zero, of_im, zero, zero, ob_re, zero, ob_im], axis=1)

    mt = jnp.transpose(m_intra, (0, 2, 1))
    win = jnp.concatenate(win_re + win_im, axis=-1)
    woutt = jnp.transpose(wout, (0, 2, 1))
    lanes = lambda p: jnp.concatenate([p[T, 0], p[T, 1]], axis=-1)
    nblk = G // S5_GROUPS_PER_STEP
    blocked = lambda w: w.reshape((nblk, S5_GROUPS_PER_STEP) + w.shape[1:])
    return (blocked(mt.astype(BF16)), blocked(win.astype(BF16)), blocked(woutt.astype(BF16)),
            blocked(lanes(pw_re)), blocked(lanes(pw_im)))


def _s5_scan(u, ops):
    B, L, D = u.shape
    T, P, ng = S5_CHUNK, SSM_STATE, S5_GROUPS_PER_STEP
    C = L // T
    TI = T * SSM_GROUP
    lanes = ng * SSM_GROUP
    rows = ng * (C + S5_PITCH_PAD)
    seq = pl.BlockSpec((None, L, lanes), lambda b, j: (b, 0, j))
    op = lambda r, c: pl.BlockSpec((None, ng, r, c), lambda b, j: (j, 0, 0, 0))
    coef = pl.BlockSpec((None, ng, 2 * P), lambda b, j: (j, 0, 0))
    return pl.pallas_call(
        _s5_kernel,
        grid=(B, D // lanes),
        in_specs=[seq, op(TI, TI), op(TI, TI), op(TI, 2 * TI), coef, coef],
        out_specs=seq,
        out_shape=jax.ShapeDtypeStruct((B, L, D), F32),
        scratch_shapes=[pltpu.VMEM((ng, TI, C), BF16), pltpu.VMEM((ng, TI, C), F32),
                        pltpu.VMEM((rows, 2 * P), F32), pltpu.VMEM((rows, 2 * P), F32),
                        pltpu.VMEM((4, rows, 2 * P), F32)],
        compiler_params=_cparams("parallel", "parallel"),
    )(u, *ops)


def _na_kernel(q_ref, k_ref, v_ref, g_ref, bias_ref, o_ref):
    W, kr, dh = GRID_W, NA_ROWS, NA_HEAD_DIM
    n_rows = k_ref.shape[0] // W
    per_step = q_ref.shape[0] // W
    j = pl.program_id(2)
    first = lax.broadcasted_iota(jnp.int32, (W, 2 * dh), 1) < dh

    def scores(i):
        r = j * per_step + i
        rs = jnp.clip(r - kr // 2, 0, n_rows - kr)
        var = rs - r + (NA_ROWS - 1)
        keys = pl.ds(pl.multiple_of(rs * W, W), kr * W)
        q2 = q_ref[i * W:(i + 1) * W, :]
        zero = jnp.zeros_like(q2)
        qm = jnp.concatenate([jnp.where(first, q2, zero), jnp.where(first, zero, q2)], axis=0)
        s = lax.dot_general(qm, k_ref[keys, :], (((1,), (1,)), ((), ())), preferred_element_type=F32)
        bias = jnp.concatenate([bias_ref[var + 2 * c] for c in range(kr // 2)], axis=1)
        return s + bias, keys

    def attend(i, s, keys):
        p = jnp.exp2(s - jnp.max(s, axis=-1, keepdims=True))
        den = jnp.sum(p, axis=-1, keepdims=True)
        o2 = jnp.dot(p.astype(BF16), v_ref[keys, :], preferred_element_type=F32) / den
        o = jnp.where(first, o2[:W], o2[W:])
        qrow = slice(i * W, (i + 1) * W)
        o_ref[qrow, :] = (o * _silu(g_ref[qrow, :])).astype(BF16)

    pending = [scores(i) for i in range(NA_SCORE_LEAD)]
    for i in range(per_step):
        if i + NA_SCORE_LEAD < per_step:
            pending.append(scores(i + NA_SCORE_LEAD))
        attend(i, *pending.pop(0))


def _na_bias_table(rel_bias):
    W, R = GRID_W, NA_ROWS
    c_idx = jnp.arange(W)
    col_start = jnp.clip(c_idx - NA_COLS // 2, 0, W - NA_COLS)
    col_valid = ((c_idx[None, :] >= col_start[:, None])
                 & (c_idx[None, :] < col_start[:, None] + NA_COLS))
    offset = c_idx[None, None, :] - c_idx[None, :, None] + NA_COLS - 1
    onehot = (jnp.arange(2 * NA_COLS - 1)[:, None, None] == offset).astype(F32)
    col_bias = jnp.einsum('hdm,mqk->dhqk', rel_bias.astype(F32) * LOG2E, onehot,
                          precision=lax.Precision.HIGHEST)
    col_bias = jnp.where(col_valid[None, None], col_bias, NEG_INF)
    rows = col_bias.reshape(2 * R - 1, NA_HEADS // 2, 2 * W, W)
    return jnp.concatenate([rows[:-1], rows[1:]], axis=-1)


def _neighbourhood_attention(q, k, v, g, bias_tab):
    B, L, D = q.shape
    rows = L // GRID_W
    per_step = min(NA_ROWS_PER_STEP, rows)
    assert rows >= NA_ROWS and rows % per_step == 0
    tq = per_step * GRID_W
    lanes = 2 * NA_HEAD_DIM
    blk = pl.BlockSpec((None, tq, lanes), lambda b, h, j: (b, j, h))
    seq = pl.BlockSpec((None, L, lanes), lambda b, h, j: (b, 0, h))
    return pl.pallas_call(
        _na_kernel,
        grid=(B, NA_HEADS // 2, rows // per_step),
        in_specs=[blk, seq, seq, blk,
                  pl.BlockSpec((2 * NA_ROWS - 2, None, 2 * GRID_W, 2 * GRID_W),
                               lambda b, h, j: (0, h, 0, 0))],
        out_specs=blk,
        out_shape=jax.ShapeDtypeStruct((B, L, D), BF16),
        compiler_params=_cparams("parallel", "parallel", "arbitrary"),
    )(q, k, v, g, bias_tab)


def _rotary_tables(L):
    dh = RET_HEAD_DIM
    inv = ROPE_BASE ** (-jnp.arange(0, dh, 2, dtype=F32) / dh)
    ang = jnp.arange(L, dtype=F32)[:, None] * inv[None, :]
    cos, sin = jnp.cos(ang), jnp.sin(ang)
    return jnp.concatenate([cos, cos], axis=1), jnp.concatenate([-sin, sin], axis=1)


def _trunk(x, c, p):
    B, L, D = x.shape
    mods = _modulation(c, p["w_mod"], p["b_mod"]).reshape(DEPTH, B, 3, D)
    cos2, sin2 = p["rope"][L]
    kind =lambda i: "even" if i % 2 == 0 else "odd"

    def open_args(i):
        g_pre = p["norm_pre"][i].reshape(1, D)
        if kind(i) == "even":
            return (mods[i], g_pre, p["w_in_ab"][i // 2], cos2, sin2)
        return (mods[i], g_pre, p["w_in_c"][i // 2])

    opened = _stage(B, L, None, (x,), kind(0), open_args(0))
    for i in range(DEPTH):
        j = i // 2
        g_post = p["norm_post"][i].reshape(1, D)
        if kind(i) == "even":
            q, k, v, ga, u, gb = opened
            oa = _retention(q, k, v, ga)
            ys = _s5_scan(u, p["s5_ops"][j])
            close_args = (oa, ys, u, gb, x, mods[i], g_post, p["ssm_d"][j].reshape(1, D_SSM),
                          p["ssm_w_glu"][j], p["w_out_ab"][j])
        else:
            q, k, v, g = opened
            o = _neighbourhood_attention(q, k, v, g, p["na_bias"][j])
            close_args = (o, x, mods[i], g_post, p["w_out_c"][j])
        last = i + 1 == DEPTH
        res = _stage(B, L, kind(i), close_args, None if last else kind(i + 1),
                     () if last else open_args(i + 1))
        x, opened = res[0], res[1:]
    return x


def kernel(x_prompt, x_sample, c_prompt, c_sample, norm_pre, norm_post, w_mod, b_mod, w_in_ab, w_out_ab, ssm_a_re, ssm_a_im, ssm_log_step, ssm_b_re, ssm_b_im, ssm_c_re, ssm_c_im, ssm_d, ssm_w_glu, w_in_c, w_out_c, na_rel_bias):
    n_even, n_odd = w_in_ab.shape[0], w_in_c.shape[0]
    p = {
        "norm_pre": norm_pre, "norm_post": norm_post, "w_mod": w_mod, "b_mod": b_mod,
        "w_in_ab": w_in_ab.astype(BF16), "w_out_ab": w_out_ab.astype(BF16),
        "ssm_d": ssm_d, "ssm_w_glu": ssm_w_glu.astype(BF16),
        "w_in_c": w_in_c.astype(BF16), "w_out_c": w_out_c.astype(BF16),
        "s5_ops": [_s5_operators(ssm_a_re[j], ssm_a_im[j], ssm_log_step[j], ssm_b_re[j], ssm_b_im[j],
                                 ssm_c_re[j], ssm_c_im[j]) for j in range(n_even)],
        "na_bias": [_na_bias_table(na_rel_bias[j]) for j in range(n_odd)],
        "rope": {L: _rotary_tables(L) for L in {x_prompt.shape[1], x_sample.shape[1]}},
    }
    return _trunk(x_prompt, c_prompt, p), _trunk(x_sample, c_sample, p)
```

```python
import functools
import math

import jax
import jax.numpy as jnp
from jax import lax
from jax.experimental import pallas as pl
from jax.experimental.pallas import tpu as pltpu

F32 = jnp.float32
BF16 = jnp.bfloat16

D_MODEL = 1024
DEPTH = 4
GRID_W = 64
D_RET = 512
RET_HEADS = 4
RET_HEAD_DIM = 128
RET_CHUNK = 128
D_SSM = 512
SSM_GROUP = 16
SSM_GROUPS = 32
SSM_STATE = 64
EVEN_IN = 4 * D_RET + 2 * D_SSM
NA_HEADS = 16
NA_HEAD_DIM = 64
NA_ROWS = 8
NA_COLS = 16
ODD_IN = 4 * D_MODEL
ROPE_BASE = 10000.0
EPS = 1e-6
NEG_INF = -1e30
LOG2E = 1.4426950408889634

S5_CHUNK = 16
S5_GROUPS_PER_STEP = 8
S5_PITCH_PAD = 8
S5_SCAN_UNROLL = 8
S5_PHASE_UNROLL = 4
TOKEN_TILE = 512
NA_ROWS_PER_STEP = 64
NA_SCORE_LEAD = 3
RET_UNROLL = 32
VMEM_LIMIT = 52 * 1024 * 1024


def _cparams(*sem):
    return pltpu.CompilerParams(dimension_semantics=sem, vmem_limit_bytes=VMEM_LIMIT)


def _sigmoid(x):
    return 1.0 / (1.0 + jnp.exp(-x))


def _silu(x):
    return x * _sigmoid(x)


def _gelu_tanh(x):
    c = math.sqrt(2.0 / math.pi)
    return 0.5 * x * (1.0 + jnp.tanh(c * (x + 0.044715 * (x * x * x))))


def _mod_kernel(c_ref, w_ref, b_ref, o_ref):
    a = _silu(c_ref[...])
    o_ref[...] = jnp.dot(a, w_ref[...], preferred_element_type=F32) + b_ref[...]


def _modulation(c, w_mod, b_mod):
    B = c.shape[0]
    tn = 1024
    return pl.pallas_call(
        _mod_kernel,
        grid=(DEPTH, 3 * D_MODEL // tn),
        in_specs=[
            pl.BlockSpec((B, D_MODEL), lambda i, n: (0, 0)),
            pl.BlockSpec((None, D_MODEL, tn), lambda i, n: (i, 0, n)),
            pl.BlockSpec((None, 1, tn), lambda i, n: (i, 0, n)),
        ],
        out_specs=pl.BlockSpec((None, B, tn), lambda i, n: (i, 0, n)),
        out_shape=jax.ShapeDtypeStruct((DEPTH, B, 3 * D_MODEL), F32),
        compiler_params=_cparams("arbitrary", "arbitrary"),
    )(c, w_mod, b_mod.reshape(DEPTH, 1, 3 * D_MODEL))


def _prenorm(x, mod_ref, g_ref):
    y = x * lax.rsqrt(jnp.mean(x * x, axis=-1, keepdims=True) + EPS) * g_ref[...]
    return (y * (1.0 + mod_ref[1:2, :]) + mod_ref[0:1, :]).astype(BF16)


def _even_open(x, mod_ref, g_ref, w_ref, cos_ref, sin_ref, q_ref, k_ref, v_ref, ga_ref, u_ref, gb_ref):
    h = _prenorm(x, mod_ref, g_ref)
    cos2, sin2 = cos_ref[...], sin_ref[...]

    def proj(c):
        return jnp.dot(h, w_ref[:, c * D_RET:(c + 1) * D_RET], preferred_element_type=F32)

    def rotary(z):
        cols = []
        for hd in range(RET_HEADS):
            zh = z[:, hd * RET_HEAD_DIM:(hd + 1) * RET_HEAD_DIM]
            cols.append(zh * cos2 + pltpu.roll(zh, RET_HEAD_DIM // 2, 1) * sin2)
        return jnp.concatenate(cols, axis=1)

    q_ref[...] = rotary(proj(0)).astype(BF16)
    k_ref[...] = (rotary(proj(1)) * (RET_HEAD_DIM ** -0.5)).astype(BF16)
    v_ref[...] = proj(2).astype(BF16)
    ga_ref[...] = proj(3)
    u_ref[...] = proj(4)
    gb_ref[...] = proj(5)


def _odd_open(x, mod_ref, g_ref, w_ref, q_ref, k_ref, v_ref, gate_ref):
    h = _prenorm(x, mod_ref, g_ref)
    half = D_MODEL // 2

    def proj(c):
        return jnp.dot(h, w_ref[:, c * half:(c + 1) * half], preferred_element_type=F32)

    for c in range(2):
        sl = slice(c * half, (c + 1) * half)
        q_ref[:, sl] = (proj(c) * (NA_HEAD_DIM ** -0.5 * LOG2E)).astype(BF16)
        k_ref[:, sl] = proj(2 + c).astype(BF16)
        v_ref[:, sl] = proj(4 + c).astype(BF16)
        gate_ref[:, sl] = proj(6 + c)


def _residual(y, x_ref, mod_ref, g_ref):
    yn = y * lax.rsqrt(jnp.mean(y * y, axis=-1, keepdims=True) + EPS) * g_ref[...]
    return x_ref[...] + mod_ref[2:3, :] * yn


def _even_close(oa_ref, ys_ref, u_ref, gb_ref, x_ref, mod_ref, g_ref, d_ref, wglu_ref, wout_ref):
    y = _gelu_tanh(ys_ref[...] + d_ref[...] * u_ref[...])
    y = y * _sigmoid(jnp.dot(y.astype(BF16), wglu_ref[...], preferred_element_type=F32))
    ob = (y * _silu(gb_ref[...])).astype(BF16)
    out = jnp.dot(jnp.concatenate([oa_ref[...], ob], axis=1), wout_ref[...], preferred_element_type=F32)
    return _residual(out, x_ref, mod_ref, g_ref)


def _odd_close(o_ref, x_ref, mod_ref, g_ref, wout_ref):
    out = jnp.dot(o_ref[...], wout_ref[...], preferred_element_type=F32)
    return _residual(out, x_ref, mod_ref, g_ref)


_CLOSE = {"even": _even_close, "odd": _odd_close}
_OPEN = {"even": _even_open, "odd": _odd_open}


def _stage_kernel(*refs, close, open_, n_close, n_open):
    ins, outs = refs[:n_close + n_open], refs[n_close + n_open:]
    if close is None:
        x = ins[0][...]
    else:
        x = _CLOSE[close](*ins[:n_close])
        outs[0][...] = x
        outs = outs[1:]
    if open_ is not None:
        _OPEN[open_](x, *ins[n_close:], *outs)


def _stage(B, L, close, close_args, open_, open_args):
    D, tm = D_MODEL, TOKEN_TILE
    tok = lambda w: pl.BlockSpec((None, tm, w), lambda b, t: (b, t, 0))
    mod = pl.BlockSpec((None, 3, D), lambda b, t: (b, 0, 0))
    const = lambda r, c: pl.BlockSpec((r, c), lambda b, t: (0, 0), pipeline_mode=pl.Buffered(1))
    rope = pl.BlockSpec((tm, RET_HEAD_DIM), lambda b, t: (t, 0))
    act = lambda w, dt: jax.ShapeDtypeStruct((B, L, w), dt)
    close_specs = {
        None: [tok(D)],
        "even": [tok(D_RET), tok(D_SSM), tok(D_SSM), tok(D_SSM), tok(D), mod,
                 const(1, D), const(1, D_SSM), const(D_SSM, D_SSM), const(D, D)],
        "odd": [tok(D), tok(D), mod, const(1, D), const(D, D)],
    }[close]
    open_specs = {
        None: [],
        "even": [mod, const(1, D), const(D, EVEN_IN), rope, rope],
        "odd": [mod, const(1, D), const(D, ODD_IN)],
    }[open_]
    outs = {
        None: [],
        "even": [(D_RET, BF16)] * 3 + [(D_RET, F32)] * 3,
        "odd": [(D, BF16)] * 3 + [(D, F32)],
    }[open_]
    if close is not None:
        outs = [(D, F32)] + outs
    return pl.pallas_call(
        functools.partial(_stage_kernel, close=close, open_=open_,
                          n_close=len(close_specs), n_open=len(open_specs)),
        grid=(B, L // tm),
        in_specs=close_specs + open_specs,
        out_specs=[tok(w) for w, _ in outs],
        out_shape=[act(w, dt) for w, dt in outs],
        compiler_params=_cparams("parallel", "parallel"),
    )(*close_args, *open_args)


def _ret_kernel(q_ref, k_ref, v_ref, ga_ref, intra_ref, qf_ref, qb_ref, kf_ref, kb_ref, dec_ref,
                o_ref, stb_ref):
    cs = RET_CHUNK
    n = q_ref.shape[0] // cs
    dec = dec_ref[...]
    tdot = lambda a, b: lax.dot_general(a, b, (((0,), (0,)), ((), ())), preferred_element_type=F32)

    def rows(c):
        return pl.ds(pl.multiple_of(c * cs, cs), cs)

    ways = min(RET_UNROLL, n)
    state0 = jnp.zeros((RET_HEAD_DIM, RET_HEAD_DIM), F32)

    def rev(it, st):
        chunks = [n - 1 - (it * ways + w) for w in range(ways)]
        kvs = [tdot((k_ref[rows(c), :].astype(F32) * kb_ref[...]).astype(BF16), v_ref[rows(c), :])
               for c in chunks]
        for c, kv in zip(chunks, kvs):
            stb_ref[c] = st.astype(BF16)
            st = dec * st + kv
        return st

    lax.fori_loop(0, n // ways, rev, state0)

    def fwd(it, st):
        chunks = [it * ways + w for w in range(ways)]
        q = [q_ref[rows(c), :] for c in chunks]
        k = [k_ref[rows(c), :] for c in chunks]
        v = [v_ref[rows(c), :] for c in chunks]
        s = [lax.dot_general(q[w], k[w], (((1,), (1,)), ((), ())), preferred_element_type=F32)
             * intra_ref[...] for w in range(ways)]
        kvs = [tdot((k[w].astype(F32) * kf_ref[...]).astype(BF16), v[w]) for w in range(ways)]
        before = []
        for kv in kvs:
            before.append(st.astype(BF16))
            st = dec * st + kv
        outs = []
        for w, c in enumerate(chunks):
            qf32 = q[w].astype(F32)
            lhs = jnp.concatenate([s[w].astype(BF16), (qf32 * qf_ref[...]).astype(BF16),
                                   (qf32 * qb_ref[...]).astype(BF16)], axis=1)
            rhs = jnp.concatenate([v[w], before[w], stb_ref[c]], axis=0)
            outs.append(jnp.dot(lhs, rhs, preferred_element_type=F32))
        for c, o in zip(chunks, outs):
            mu = jnp.mean(o, axis=-1, keepdims=True)
            d = o - mu
            hn = d * lax.rsqrt(jnp.mean(d * d, axis=-1, keepdims=True) + EPS)
            o_ref[rows(c), :] = (hn * _silu(ga_ref[rows(c), :])).astype(BF16)
        return st

    lax.fori_loop(0, n // ways, fwd, state0)


def _retention_tables():
    H, cs, dk = RET_HEADS, RET_CHUNK, RET_HEAD_DIM
    log_g = jnp.log1p(-jnp.exp2(-5.0 - jnp.arange(H, dtype=F32)))
    pos = jnp.arange(cs, dtype=F32)
    intra = jnp.exp(jnp.abs(pos[:, None] - pos[None, :])[None] * log_g[:, None, None])
    col = lambda e: jnp.broadcast_to(jnp.exp(e[:, None] * log_g[None]).T[:, :, None], (H, cs, dk))
    q_fwd, q_bwd = col(pos), col(cs - 1.0 - pos)
    k_fwd, k_bwd = col(cs - pos), col(pos + 1.0)
    decay = jnp.broadcast_to(jnp.exp(cs * log_g)[:, None, None], (H, dk, dk))
    return intra, q_fwd, q_bwd, k_fwd, k_bwd, decay


def _retention(q, k, v, ga):
    B, L, _ = q.shape
    dk = RET_HEAD_DIM
    seq = pl.BlockSpec((None, L, dk), lambda b, h: (b, 0, h))
    tab = pl.BlockSpec((None, RET_CHUNK, dk), lambda b, h: (h, 0, 0))
    return pl.pallas_call(
        _ret_kernel,
        grid=(B, RET_HEADS),
        in_specs=[seq, seq, seq, seq] + [tab] * 6,
        out_specs=seq,
        out_shape=jax.ShapeDtypeStruct((B, L, D_RET), BF16),
        scratch_shapes=[pltpu.VMEM((L // RET_CHUNK, dk, dk), BF16)],
        compiler_params=_cparams("parallel", "parallel"),
    )(q, k, v, ga, *_retention_tables())


def _s5_kernel(u_ref, mt_ref, win_ref, woutt_ref, are_ref, aim_ref, y_ref,
               xt_ref, yt_ref, kvre_ref, kvim_ref, h_ref):
    T, Gi, P = S5_CHUNK, SSM_GROUP, SSM_STATE
    C = u_ref.shape[0] // T
    pitch = C + S5_PITCH_PAD
    ng = xt_ref.shape[0]

    def relayout_in(t, carry):
        a_t = u_ref[pl.ds(t, C, stride=T), :].T
        rows = pl.ds(pl.multiple_of(t * Gi, Gi), Gi)
        for g in range(ng):
            xt_ref[g, rows, :] = a_t[g * Gi:(g + 1) * Gi, :].astype(BF16)
        return carry

    lax.fori_loop(0, T, relayout_in, 0, unroll=S5_PHASE_UNROLL)

    def chunk_matmuls(g, carry):
        xt = xt_ref[g]
        yt_ref[g] = jnp.dot(mt_ref[g], xt, preferred_element_type=F32)
        kv = lax.dot_general(xt, win_ref[g], (((0,), (0,)), ((), ())),
                             preferred_element_type=F32)
        rows = pl.ds(pl.multiple_of(g * pitch, 8), C)
        kvre_ref[rows, :] = kv[:, :2 * P]
        kvim_ref[rows, :] = kv[:, 2 * P:]
        return carry

    lax.fori_loop(0, ng, chunk_matmuls, 0, unroll=S5_PHASE_UNROLL)

    are, aim = are_ref[...], aim_ref[...]

    def step(k, carry):
        sf_re, sf_im, sb_re, sb_im = carry
        rf = pl.ds(k, ng, stride=pitch)
        rb = pl.ds(C - 1 - k, ng, stride=pitch)
        h_ref[0, rf, :] = sf_re
        h_ref[1, rf, :] = sf_im
        h_ref[2, rb, :] = sb_re
        h_ref[3, rb, :] = sb_im
        nf_re = are * sf_re - aim * sf_im + kvre_ref[rf, :]
        nf_im = are * sf_im + aim * sf_re + kvim_ref[rf, :]
        nb_re = are * sb_re - aim * sb_im + kvre_ref[rb, :]
        nb_im = are * sb_im + aim * sb_re + kvim_ref[rb, :]
        return nf_re, nf_im, nb_re, nb_im

    z = jnp.zeros((ng, 2 * P), F32)
    lax.fori_loop(0, C, step, (z, z, z, z), unroll=S5_SCAN_UNROLL)

    def carried_outputs(g, carry):
        rows = pl.ds(pl.multiple_of(g * pitch, 8), C)
        h = jnp.concatenate([h_ref[i, rows, :] for i in range(4)], axis=1).astype(BF16)
        yt_ref[g] += lax.dot_general(woutt_ref[g], h, (((1,), (1,)), ((), ())),
                                     preferred_element_type=F32)
        return carry

    lax.fori_loop(0, ng, carried_outputs, 0, unroll=S5_PHASE_UNROLL)

    def relayout_out(t, carry):
        rows = pl.ds(pl.multiple_of(t * Gi, Gi), Gi)
        b_t = jnp.concatenate([yt_ref[g, rows, :] for g in range(ng)], axis=0)
        y_ref[pl.ds(t, C, stride=T), :] = b_t.T
        return carry

    lax.fori_loop(0, T, relayout_out, 0, unroll=S5_PHASE_UNROLL)


def _s5_operators(a_re, a_im, log_step, b_re, b_im, c_re, c_im):
    T, G, P, Gi = S5_CHUNK, SSM_GROUPS, SSM_STATE, SSM_GROUP
    hp = lax.Precision.HIGHEST
    a_re, a_im = a_re.astype(F32), a_im.astype(F32)
    delta = jnp.exp(log_step.astype(F32))[..., None]
    z_re, z_im = a_re * delta, a_im * delta
    mag = jnp.exp(z_re)
    abar_re, abar_im = mag * jnp.cos(z_im), mag * jnp.sin(z_im)
    den = a_re * a_re + a_im * a_im
    n_re, n_im = abar_re - 1.0, abar_im
    f_re = (n_re * a_re + n_im * a_im) / den
    f_im = (n_im * a_re - n_re * a_im) / den
    b_re, b_im = b_re.astype(F32), b_im.astype(F32)
    bb_re = f_re[..., None] * b_re - f_im[..., None] * b_im
    bb_im = f_re[..., None] * b_im + f_im[..., None] * b_re
    c_re, c_im = c_re.astype(F32), c_im.astype(F32)

    pr, pi = [jnp.ones_like(abar_re)], [jnp.zeros_like(abar_re)]
    for _ in range(T):
        pr, pi = (pr + [pr[-1] * abar_re - pi[-1] * abar_im],
                  pi + [pr[-1] * abar_im + pi[-1] * abar_re])
    pw_re, pw_im = jnp.stack(pr), jnp.stack(pi)

    w_re = pw_re[:T, ..., None] * bb_re[None] - pw_im[:T, ..., None] * bb_im[None]
    w_im = pw_re[:T, ..., None] * bb_im[None] + pw_im[:T, ..., None] * bb_re[None]
    kern = (jnp.einsum('xgip,dxgpj->dxgij', c_re, w_re, precision=hp)
            - jnp.einsum('xgip,dxgpj->dxgij', c_im, w_im, precision=hp))

    lag = jnp.arange(T)[None, :] - jnp.arange(T)[:, None]
    place_f = (jnp.arange(T)[:, None, None] == lag).astype(F32)
    place_b = (jnp.arange(T)[:, None, None] == -lag).astype(F32)
    both = (jnp.einsum('dst,dgij->gsjti', place_f, kern[:, 0], precision=hp)
            + jnp.einsum('dst,dgij->gsjti', place_b, kern[:, 1], precision=hp))
    m_intra = both.reshape(G, T * Gi, T * Gi)

    def w_in(w):
        return jnp.transpose(w, (1, 0, 3, 2)).reshape(G, T * Gi, P)

    win_re = [w_in(w_re[::-1, 0]), w_in(w_re[:, 1])]
    win_im = [w_in(w_im[::-1, 0]), w_in(w_im[:, 1])]

    def w_out(direction, qr, qi):
        cr, ci = c_re[direction], c_im[direction]
        wr = cr[None] * qr[:, :, None, :] - ci[None] * qi[:, :, None, :]
        wi = cr[None] * qi[:, :, None, :] + ci[None] * qr[:, :, None, :]
        fix = lambda w: jnp.transpose(w, (1, 3, 0, 2)).reshape(G, P, T * Gi)
        return fix(wr), fix(-wi)

    of_re, of_im = w_out(0, pw_re[1:, 0], pw_im[1:, 0])
    ob_re, ob_im = w_out(1, pw_re[:0:-1, 1], pw_im[:0:-1, 1])
    zero = jnp.zeros_like(of_re)
    wout = jnp.concatenate([of_re, zero, of_im, zero, zero, ob_re, zero, ob_im], axis=1)

    mt = jnp.transpose(m_intra, (0, 2, 1))
    win = jnp.concatenate(win_re + win_im, axis=-1)
    woutt = jnp.transpose(wout, (0, 2, 1))
    lanes = lambda p: jnp.concatenate([p[T, 0], p[T, 1]], axis=-1)
    nblk = G // S5_GROUPS_PER_STEP
    blocked = lambda w: w.reshape((nblk, S5_GROUPS_PER_STEP) + w.shape[1:])
    return (blocked(mt.astype(BF16)), blocked(win.astype(BF16)), blocked(woutt.astype(BF16)),
            blocked(lanes(pw_re)), blocked(lanes(pw_im)))


def _s5_scan(u, ops):
    B, L, D = u.shape
    T, P, ng = S5_CHUNK, SSM_STATE, S5_GROUPS_PER_STEP
    C = L // T
    TI = T * SSM_GROUP
    lanes = ng * SSM_GROUP
    rows = ng * (C + S5_PITCH_PAD)
    seq = pl.BlockSpec((None, L, lanes), lambda b, j: (b, 0, j))
    op = lambda r, c: pl.BlockSpec((None, ng, r, c), lambda b, j: (j, 0, 0, 0))
    coef = pl.BlockSpec((None, ng, 2 * P), lambda b, j: (j, 0, 0))
    return pl.pallas_call(
        _s5_kernel,
        grid=(B, D // lanes),
        in_specs=[seq, op(TI, TI), op(TI, TI), op(TI, 2 * TI), coef, coef],
        out_specs=seq,
        out_shape=jax.ShapeDtypeStruct((B, L, D), F32),
        scratch_shapes=[pltpu.VMEM((ng, TI, C), BF16), pltpu.VMEM((ng, TI, C), F32),
                        pltpu.VMEM((rows, 2 * P), F32), pltpu.VMEM((rows, 2 * P), F32),
                        pltpu.VMEM((4, rows, 2 * P), F32)],
        compiler_params=_cparams("parallel", "parallel"),
    )(u, *ops)


def _na_kernel(q_ref, k_ref, v_ref, g_ref, bias_ref, o_ref):
    W, kr, dh = GRID_W, NA_ROWS, NA_HEAD_DIM
    n_rows = k_ref.shape[0] // W
    per_step = q_ref.shape[0] // W
    j = pl.program_id(2)
    first = lax.broadcasted_iota(jnp.int32, (W, 2 * dh), 1) < dh

    def scores(i):
        r = j * per_step + i
        rs = jnp.clip(r - kr // 2, 0, n_rows - kr)
        var = rs - r + (NA_ROWS - 1)
        keys = pl.ds(pl.multiple_of(rs * W, W), kr * W)
        q2 = q_ref[i * W:(i + 1) * W, :]
        zero = jnp.zeros_like(q2)
        qm = jnp.concatenate([jnp.where(first, q2, zero), jnp.where(first, zero, q2)], axis=0)
        s = lax.dot_general(qm, k_ref[keys, :], (((1,), (1,)), ((), ())), preferred_element_type=F32)
        bias = jnp.concatenate([bias_ref[var + 2 * c] for c in range(kr // 2)], axis=1)
        return s + bias, keys

    def attend(i, s, keys):
        p = jnp.exp2(s - jnp.max(s, axis=-1, keepdims=True))
        den = jnp.sum(p, axis=-1, keepdims=True)
        o2 = jnp.dot(p.astype(BF16), v_ref[keys, :], preferred_element_type=F32) / den
        o = jnp.where(first, o2[:W], o2[W:])
        qrow = slice(i * W, (i + 1) * W)
        o_ref[qrow, :] = (o * _silu(g_ref[qrow, :])).astype(BF16)

    pending = [scores(i) for i in range(NA_SCORE_LEAD)]
    for i in range(per_step):
        if i + NA_SCORE_LEAD < per_step:
            pending.append(scores(i + NA_SCORE_LEAD))
        attend(i, *pending.pop(0))


def _na_bias_table(rel_bias):
    W, R = GRID_W, NA_ROWS
    c_idx = jnp.arange(W)
    col_start = jnp.clip(c_idx - NA_COLS // 2, 0, W - NA_COLS)
    col_valid = ((c_idx[None, :] >= col_start[:, None])
                 & (c_idx[None, :] < col_start[:, None] + NA_COLS))
    offset = c_idx[None, None, :] - c_idx[None, :, None] + NA_COLS - 1
    onehot = (jnp.arange(2 * NA_COLS - 1)[:, None, None] == offset).astype(F32)
    col_bias = jnp.einsum('hdm,mqk->dhqk', rel_bias.astype(F32) * LOG2E, onehot,
                          precision=lax.Precision.HIGHEST)
    col_bias = jnp.where(col_valid[None, None], col_bias, NEG_INF)
    rows = col_bias.reshape(2 * R - 1, NA_HEADS // 2, 2 * W, W)
    return jnp.concatenate([rows[:-1], rows[1:]], axis=-1)


def _neighbourhood_attention(q, k, v, g, bias_tab):
    B, L, D = q.shape
    rows = L // GRID_W
    per_step = min(NA_ROWS_PER_STEP, rows)
    assert rows >= NA_ROWS and rows % per_step == 0
    tq = per_step * GRID_W
    lanes = 2 * NA_HEAD_DIM
    blk = pl.BlockSpec((None, tq, lanes), lambda b, h, j: (b, j, h))
    seq = pl.BlockSpec((None, L, lanes), lambda b, h, j: (b, 0, h))
    return pl.pallas_call(
        _na_kernel,
        grid=(B, NA_HEADS // 2, rows // per_step),
        in_specs=[blk, seq, seq, blk,
                  pl.BlockSpec((2 * NA_ROWS - 2, None, 2 * GRID_W, 2 * GRID_W),
                               lambda b, h, j: (0, h, 0, 0))],
        out_specs=blk,
        out_shape=jax.ShapeDtypeStruct((B, L, D), BF16),
        compiler_params=_cparams("parallel", "parallel", "arbitrary"),
    )(q, k, v, g, bias_tab)


def _rotary_tables(L):
    dh = RET_HEAD_DIM
    inv = ROPE_BASE ** (-jnp.arange(0, dh, 2, dtype=F32) / dh)
    ang = jnp.arange(L, dtype=F32)[:, None] * inv[None, :]
    cos, sin = jnp.cos(ang), jnp.sin(ang)
    return jnp.concatenate([cos, cos], axis=1), jnp.concatenate([-sin, sin], axis=1)


def _trunk(x, c, p):
    B, L, D = x.shape
    mods = _modulation(c, p["w_mod"], p["b_mod"]).reshape(DEPTH, B, 3, D)
    cos2, sin2 = p["rope"][L]
    kind =lambda i: "even" if i % 2 == 0 else "odd"

    def open_args(i):
        g_pre = p["norm_pre"][i].reshape(1, D)
        if kind(i) == "even":
            return (mods[i], g_pre, p["w_in_ab"][i // 2], cos2, sin2)
        return (mods[i], g_pre, p["w_in_c"][i // 2])

    opened = _stage(B, L, None, (x,), kind(0), open_args(0))
    for i in range(DEPTH):
        j = i // 2
        g_post = p["norm_post"][i].reshape(1, D)
        if kind(i) == "even":
            q, k, v, ga, u, gb = opened
            oa = _retention(q, k, v, ga)
            ys = _s5_scan(u, p["s5_ops"][j])
            close_args = (oa, ys, u, gb, x, mods[i], g_post, p["ssm_d"][j].reshape(1, D_SSM),
                          p["ssm_w_glu"][j], p["w_out_ab"][j])
        else:
            q, k, v, g = opened
            o = _neighbourhood_attention(q, k, v, g, p["na_bias"][j])
            close_args = (o, x, mods[i], g_post, p["w_out_c"][j])
        last = i + 1 == DEPTH
        res = _stage(B, L, kind(i), close_args, None if last else kind(i + 1),
                     () if last else open_args(i + 1))
        x, opened = res[0], res[1:]
    return x


def kernel(x_prompt, x_sample, c_prompt, c_sample, norm_pre, norm_post, w_mod, b_mod, w_in_ab, w_out_ab, ssm_a_re, ssm_a_im, ssm_log_step, ssm_b_re, ssm_b_im, ssm_c_re, ssm_c_im, ssm_d, ssm_w_glu, w_in_c, w_out_c, na_rel_bias):
    n_even, n_odd = w_in_ab.shape[0], w_in_c.shape[0]
    p = {
        "norm_pre": norm_pre, "norm_post": norm_post, "w_mod": w_mod, "b_mod": b_mod,
        "w_in_ab": w_in_ab.astype(BF16), "w_out_ab": w_out_ab.astype(BF16),
        "ssm_d": ssm_d, "ssm_w_glu": ssm_w_glu.astype(BF16),
        "w_in_c": w_in_c.astype(BF16), "w_out_c": w_out_c.astype(BF16),
        "s5_ops": [_s5_operators(ssm_a_re[j], ssm_a_im[j], ssm_log_step[j], ssm_b_re[j], ssm_b_im[j],
                                 ssm_c_re[j], ssm_c_im[j]) for j in range(n_even)],
        "na_bias": [_na_bias_table(na_rel_bias[j]) for j in range(n_odd)],
        "rope": {L: _rotary_tables(L) for L in {x_prompt.shape[1], x_sample.shape[1]}},
    }
    return _trunk(x_prompt, c_prompt, p), _trunk(x_sample, c_sample, p)
```

```python
import functools
import math

import jax
import jax.numpy as jnp
from jax import lax
from jax.experimental import pallas as pl
from jax.experimental.pallas import tpu as pltpu

F32 = jnp.float32
BF16 = jnp.bfloat16

D_MODEL = 1024
DEPTH = 4
GRID_W = 64
D_RET = 512
RET_HEADS = 4
RET_HEAD_DIM = 128
RET_CHUNK = 128
D_SSM = 512
SSM_GROUP = 16
SSM_GROUPS = 32
SSM_STATE = 64
EVEN_IN = 4 * D_RET + 2 * D_SSM
NA_HEADS = 16
NA_HEAD_DIM = 64
NA_ROWS = 8
NA_COLS = 16
ODD_IN = 4 * D_MODEL
ROPE_BASE = 10000.0
EPS = 1e-6
NEG_INF = -1e30
LOG2E = 1.4426950408889634

S5_CHUNK = 16
S5_GROUPS_PER_STEP = 8
S5_PITCH_PAD = 8
S5_SCAN_UNROLL = 8
S5_PHASE_UNROLL = 4
TOKEN_TILE = 512
NA_ROWS_PER_STEP = 64
NA_SCORE_LEAD = 3
RET_UNROLL = 32
VMEM_LIMIT = 52 * 1024 * 1024


def _cparams(*sem):
    return pltpu.CompilerParams(dimension_semantics=sem, vmem_limit_bytes=VMEM_LIMIT)


def _sigmoid(x):
    return 0.5 * jnp.tanh(0.5 * x) + 0.5


def _silu(x):
    return x * _sigmoid(x)


def _gelu_tanh(x):
    c = math.sqrt(2.0 / math.pi)
    return 0.5 * x * (1.0 + jnp.tanh(c * (x + 0.044715 * (x * x * x))))


def _mod_kernel(c_ref, w_ref, b_ref, o_ref):
    a = _silu(c_ref[...])
    o_ref[...] = jnp.dot(a, w_ref[...], preferred_element_type=F32) + b_ref[...]


def _modulation(c, w_mod, b_mod):
    B = c.shape[0]
    tn = 1024
    return pl.pallas_call(
        _mod_kernel,
        grid=(DEPTH, 3 * D_MODEL // tn),
        in_specs=[
            pl.BlockSpec((B, D_MODEL), lambda i, n: (0, 0)),
            pl.BlockSpec((None, D_MODEL, tn), lambda i, n: (i, 0, n)),
            pl.BlockSpec((None, 1, tn), lambda i, n: (i, 0, n)),
        ],
        out_specs=pl.BlockSpec((None, B, tn), lambda i, n: (i, 0, n)),
        out_shape=jax.ShapeDtypeStruct((DEPTH, B, 3 * D_MODEL), F32),
        compiler_params=_cparams("arbitrary", "arbitrary"),
    )(c, w_mod, b_mod.reshape(DEPTH, 1, 3 * D_MODEL))


def _prenorm(x, mod_ref, g_ref):
    y = x * lax.rsqrt(jnp.mean(x * x, axis=-1, keepdims=True) + EPS) * g_ref[...]
    return (y * (1.0 + mod_ref[1:2, :]) + mod_ref[0:1, :]).astype(BF16)


def _even_open(x, mod_ref, g_ref, w_ref, cos_ref, sin_ref, q_ref, k_ref, v_ref, ga_ref, u_ref, gb_ref):
    h = _prenorm(x, mod_ref, g_ref)
    cos2, sin2 = cos_ref[...], sin_ref[...]

    def proj(c):
        return jnp.dot(h, w_ref[:, c * D_RET:(c + 1) * D_RET], preferred_element_type=F32)

    def rotary(z):
        cols = []
        for hd in range(RET_HEADS):
            zh = z[:, hd * RET_HEAD_DIM:(hd + 1) * RET_HEAD_DIM]
            cols.append(zh * cos2 + pltpu.roll(zh, RET_HEAD_DIM // 2, 1) * sin2)
        return jnp.concatenate(cols, axis=1)

    q_ref[...] = rotary(proj(0)).astype(BF16)
    k_ref[...] = (rotary(proj(1)) * (RET_HEAD_DIM ** -0.5)).astype(BF16)
    v_ref[...] = proj(2).astype(BF16)
    ga_ref[...] = proj(3)
    u_ref[...] = proj(4)
    gb_ref[...] = proj(5)


def _odd_open(x, mod_ref, g_ref, w_ref, q_ref, k_ref, v_ref, gate_ref):
    h = _prenorm(x, mod_ref, g_ref)
    half = D_MODEL // 2

    def proj(c):
        return jnp.dot(h, w_ref[:, c * half:(c + 1) * half], preferred_element_type=F32)

    for c in range(2):
        sl = slice(c * half, (c + 1) * half)
        q_ref[:, sl] = (proj(c) * (NA_HEAD_DIM ** -0.5 * LOG2E)).astype(BF16)
        k_ref[:, sl] = proj(2 + c).astype(BF16)
        v_ref[:, sl] = proj(4 + c).astype(BF16)
        gate_ref[:, sl] = proj(6 + c)


def _residual(y, x_ref, mod_ref, g_ref):
    yn = y * lax.rsqrt(jnp.mean(y * y, axis=-1, keepdims=True) + EPS) * g_ref[...]
    return x_ref[...] + mod_ref[2:3, :] * yn


def _even_close(oa_ref, ys_ref, u_ref, gb_ref, x_ref, mod_ref, g_ref, d_ref, wglu_ref, wout_ref):
    y = _gelu_tanh(ys_ref[...] + d_ref[...] * u_ref[...])
    y = y * _sigmoid(jnp.dot(y.astype(BF16), wglu_ref[...], preferred_element_type=F32))
    ob = (y * _silu(gb_ref[...])).astype(BF16)
    out = jnp.dot(jnp.concatenate([oa_ref[...], ob], axis=1), wout_ref[...], preferred_element_type=F32)
    return _residual(out, x_ref, mod_ref, g_ref)


def _odd_close(o_ref, x_ref, mod_ref, g_ref, wout_ref):
    out = jnp.dot(o_ref[...], wout_ref[...], preferred_element_type=F32)
    return _residual(out, x_ref, mod_ref, g_ref)


_CLOSE = {"even": _even_close, "odd": _odd_close}
_OPEN = {"even": _even_open, "odd": _odd_open}


def _stage_kernel(*refs, close, open_, n_close, n_open):
    ins, outs = refs[:n_close + n_open], refs[n_close + n_open:]
    if close is None:
        x = ins[0][...]
    else:
        x = _CLOSE[close](*ins[:n_close])
        outs[0][...] = x
        outs = outs[1:]
    if open_ is not None:
        _OPEN[open_](x, *ins[n_close:], *outs)


def _stage(B, L, close, close_args, open_, open_args):
    D, tm = D_MODEL, TOKEN_TILE
    tok = lambda w: pl.BlockSpec((None, tm, w), lambda b, t: (b, t, 0))
    mod = pl.BlockSpec((None, 3, D), lambda b, t: (b, 0, 0))
    const = lambda r, c: pl.BlockSpec((r, c), lambda b, t: (0, 0), pipeline_mode=pl.Buffered(1))
    rope = pl.BlockSpec((tm, RET_HEAD_DIM), lambda b, t: (t, 0))
    act = lambda w, dt: jax.ShapeDtypeStruct((B, L, w), dt)
    close_specs = {
        None: [tok(D)],
        "even": [tok(D_RET), tok(D_SSM), tok(D_SSM), tok(D_SSM), tok(D), mod,
                 const(1, D), const(1, D_SSM), const(D_SSM, D_SSM), const(D, D)],
        "odd": [tok(D), tok(D), mod, const(1, D), const(D, D)],
    }[close]
    open_specs = {
        None: [],
        "even": [mod, const(1, D), const(D, EVEN_IN), rope, rope],
        "odd": [mod, const(1, D), const(D, ODD_IN)],
    }[open_]
    outs = {
        None: [],
        "even": [(D_RET, BF16)] * 3 + [(D_RET, F32)] * 3,
        "odd": [(D, BF16)] * 3 + [(D, F32)],
    }[open_]
    if close is not None:
        outs = [(D, F32)] + outs
    return pl.pallas_call(
        functools.partial(_stage_kernel, close=close, open_=open_,
                          n_close=len(close_specs), n_open=len(open_specs)),
        grid=(B, L // tm),
        in_specs=close_specs + open_specs,
        out_specs=[tok(w) for w, _ in outs],
        out_shape=[act(w, dt) for w, dt in outs],
        compiler_params=_cparams("parallel", "parallel"),
    )(*close_args, *open_args)


def _ret_kernel(q_ref, k_ref, v_ref, ga_ref, intra_ref, qf_ref, qb_ref, kf_ref, kb_ref, dec_ref,
                o_ref, stb_ref):
    cs = RET_CHUNK
    n = q_ref.shape[0] // cs
    dec = dec_ref[...]
    tdot = lambda a, b: lax.dot_general(a, b, (((0,), (0,)), ((), ())), preferred_element_type=F32)

    def rows(c):
        return pl.ds(pl.multiple_of(c * cs, cs), cs)

    ways = min(RET_UNROLL, n)
    state0 = jnp.zeros((RET_HEAD_DIM, RET_HEAD_DIM), F32)

    def rev(it, st):
        chunks = [n - 1 - (it * ways + w) for w in range(ways)]
        kvs = [tdot((k_ref[rows(c), :].astype(F32) * kb_ref[...]).astype(BF16), v_ref[rows(c), :])
               for c in chunks]
        for c, kv in zip(chunks, kvs):
            stb_ref[c] = st.astype(BF16)
            st = dec * st + kv
        return st

    lax.fori_loop(0, n // ways, rev, state0)

    def fwd(it, st):
        chunks = [it * ways + w for w in range(ways)]
        q = [q_ref[rows(c), :] for c in chunks]
        k = [k_ref[rows(c), :] for c in chunks]
        v = [v_ref[rows(c), :] for c in chunks]
        s = [lax.dot_general(q[w], k[w], (((1,), (1,)), ((), ())), preferred_element_type=F32)
             * intra_ref[...] for w in range(ways)]
        kvs = [tdot((k[w].astype(F32) * kf_ref[...]).astype(BF16), v[w]) for w in range(ways)]
        before = []
        for kv in kvs:
            before.append(st.astype(BF16))
            st = dec * st + kv
        outs = []
        for w, c in enumerate(chunks):
            qf32 = q[w].astype(F32)
            lhs = jnp.concatenate([s[w].astype(BF16), (qf32 * qf_ref[...]).astype(BF16),
                                   (qf32 * qb_ref[...]).astype(BF16)], axis=1)
            rhs = jnp.concatenate([v[w], before[w], stb_ref[c]], axis=0)
            outs.append(jnp.dot(lhs, rhs, preferred_element_type=F32))
        for c, o in zip(chunks, outs):
            mu = jnp.mean(o, axis=-1, keepdims=True)
            d = o - mu
            hn = d * lax.rsqrt(jnp.mean(d * d, axis=-1, keepdims=True) + EPS)
            o_ref[rows(c), :] = (hn * _silu(ga_ref[rows(c), :])).astype(BF16)
        return st

    lax.fori_loop(0, n // ways, fwd, state0)


def _retention_tables():
    H, cs, dk = RET_HEADS, RET_CHUNK, RET_HEAD_DIM
    log_g = jnp.log1p(-jnp.exp2(-5.0 - jnp.arange(H, dtype=F32)))
    pos = jnp.arange(cs, dtype=F32)
    intra = jnp.exp(jnp.abs(pos[:, None] - pos[None, :])[None] * log_g[:, None, None])
    col = lambda e: jnp.broadcast_to(jnp.exp(e[:, None] * log_g[None]).T[:, :, None], (H, cs, dk))
    q_fwd, q_bwd = col(pos), col(cs - 1.0 - pos)
    k_fwd, k_bwd = col(cs - pos), col(pos + 1.0)
    decay = jnp.broadcast_to(jnp.exp(cs * log_g)[:, None, None], (H, dk, dk))
    return intra, q_fwd, q_bwd, k_fwd, k_bwd, decay


def _retention(q, k, v, ga):
    B, L, _ = q.shape
    dk = RET_HEAD_DIM
    seq = pl.BlockSpec((None, L, dk), lambda b, h: (b, 0, h))
    tab = pl.BlockSpec((None, RET_CHUNK, dk), lambda b, h: (h, 0, 0))
    return pl.pallas_call(
        _ret_kernel,
        grid=(B, RET_HEADS),
        in_specs=[seq, seq, seq, seq] + [tab] * 6,
        out_specs=seq,
        out_shape=jax.ShapeDtypeStruct((B, L, D_RET), BF16),
        scratch_shapes=[pltpu.VMEM((L // RET_CHUNK, dk, dk), BF16)],
        compiler_params=_cparams("parallel", "parallel"),
    )(q, k, v, ga, *_retention_tables())


def _s5_kernel(u_ref, mt_ref, win_ref, woutt_ref, are_ref, aim_ref, y_ref,
               xt_ref, yt_ref, kvre_ref, kvim_ref, h_ref):
    T, Gi, P = S5_CHUNK, SSM_GROUP, SSM_STATE
    C = u_ref.shape[0] // T
    pitch = C + S5_PITCH_PAD
    ng = xt_ref.shape[0]

    def relayout_in(t, carry):
        a_t = u_ref[pl.ds(t, C, stride=T), :].T
        rows = pl.ds(pl.multiple_of(t * Gi, Gi), Gi)
        for g in range(ng):
            xt_ref[g, rows, :] = a_t[g * Gi:(g + 1) * Gi, :].astype(BF16)
        return carry

    lax.fori_loop(0, T, relayout_in, 0, unroll=S5_PHASE_UNROLL)

    def chunk_matmuls(g, carry):
        xt = xt_ref[g]
        yt_ref[g] = jnp.dot(mt_ref[g], xt, preferred_element_type=F32)
        kv = lax.dot_general(xt, win_ref[g], (((0,), (0,)), ((), ())),
                             preferred_element_type=F32)
        rows = pl.ds(pl.multiple_of(g * pitch, 8), C)
        kvre_ref[rows, :] = kv[:, :2 * P]
        kvim_ref[rows, :] = kv[:, 2 * P:]
        return carry

    lax.fori_loop(0, ng, chunk_matmuls, 0, unroll=S5_PHASE_UNROLL)

    are, aim = are_ref[...], aim_ref[...]

    def step(k, carry):
        sf_re, sf_im, sb_re, sb_im = carry
        rf = pl.ds(k, ng, stride=pitch)
        rb = pl.ds(C - 1 - k, ng, stride=pitch)
        h_ref[0, rf, :] = sf_re
        h_ref[1, rf, :] = sf_im
        h_ref[2, rb, :] = sb_re
        h_ref[3, rb, :] = sb_im
        nf_re = are * sf_re - aim * sf_im + kvre_ref[rf, :]
        nf_im = are * sf_im + aim * sf_re + kvim_ref[rf, :]
        nb_re = are * sb_re - aim * sb_im + kvre_ref[rb, :]
        nb_im = are * sb_im + aim * sb_re + kvim_ref[rb, :]
        return nf_re, nf_im, nb_re, nb_im

    z = jnp.zeros((ng, 2 * P), F32)
    lax.fori_loop(0, C, step, (z, z, z, z), unroll=S5_SCAN_UNROLL)

    def carried_outputs(g, carry):
        rows = pl.ds(pl.multiple_of(g * pitch, 8), C)
        h = jnp.concatenate([h_ref[i, rows, :] for i in range(4)], axis=1).astype(BF16)
        yt_ref[g] += lax.dot_general(woutt_ref[g], h, (((1,), (1,)), ((), ())),
                                     preferred_element_type=F32)
        return carry

    lax.fori_loop(0, ng, carried_outputs, 0, unroll=S5_PHASE_UNROLL)

    def relayout_out(t, carry):
        rows = pl.ds(pl.multiple_of(t * Gi, Gi), Gi)
        b_t = jnp.concatenate([yt_ref[g, rows, :] for g in range(ng)], axis=0)
        y_ref[pl.ds(t, C, stride=T), :] = b_t.T
        return carry

    lax.fori_loop(0, T, relayout_out, 0, unroll=S5_PHASE_UNROLL)


def _s5_operators(a_re, a_im, log_step, b_re, b_im, c_re, c_im):
    T, G, P, Gi = S5_CHUNK, SSM_GROUPS, SSM_STATE, SSM_GROUP
    hp = lax.Precision.HIGHEST
    a_re, a_im = a_re.astype(F32), a_im.astype(F32)
    delta = jnp.exp(log_step.astype(F32))[..., None]
    z_re, z_im = a_re * delta, a_im * delta
    mag = jnp.exp(z_re)
    abar_re, abar_im = mag * jnp.cos(z_im), mag * jnp.sin(z_im)
    den = a_re * a_re + a_im * a_im
    n_re, n_im = abar_re - 1.0, abar_im
    f_re = (n_re * a_re + n_im * a_im) / den
    f_im = (n_im * a_re - n_re * a_im) / den
    b_re, b_im = b_re.astype(F32), b_im.astype(F32)
    bb_re = f_re[..., None] * b_re - f_im[..., None] * b_im
    bb_im = f_re[..., None] * b_im + f_im[..., None] * b_re
    c_re, c_im = c_re.astype(F32), c_im.astype(F32)

    pr, pi = [jnp.ones_like(abar_re)], [jnp.zeros_like(abar_re)]
    for _ in range(T):
        pr, pi = (pr + [pr[-1] * abar_re - pi[-1] * abar_im],
                  pi + [pr[-1] * abar_im + pi[-1] * abar_re])
    pw_re, pw_im = jnp.stack(pr), jnp.stack(pi)

    w_re = pw_re[:T, ..., None] * bb_re[None] - pw_im[:T, ..., None] * bb_im[None]
    w_im = pw_re[:T, ..., None] * bb_im[None] + pw_im[:T, ..., None] * bb_re[None]
    kern = (jnp.einsum('xgip,dxgpj->dxgij', c_re, w_re, precision=hp)
            - jnp.einsum('xgip,dxgpj->dxgij', c_im, w_im, precision=hp))

    lag = jnp.arange(T)[None, :] - jnp.arange(T)[:, None]
    place_f = (jnp.arange(T)[:, None, None] == lag).astype(F32)
    place_b = (jnp.arange(T)[:, None, None] == -lag).astype(F32)
    both = (jnp.einsum('dst,dgij->gsjti', place_f, kern[:, 0], precision=hp)
            + jnp.einsum('dst,dgij->gsjti', place_b, kern[:, 1], precision=hp))
    m_intra = both.reshape(G, T * Gi, T * Gi)

    def w_in(w):
        return jnp.transpose(w, (1, 0, 3, 2)).reshape(G, T * Gi, P)

    win_re = [w_in(w_re[::-1, 0]), w_in(w_re[:, 1])]
    win_im = [w_in(w_im[::-1, 0]), w_in(w_im[:, 1])]

    def w_out(direction, qr, qi):
        cr, ci = c_re[direction], c_im[direction]
        wr = cr[None] * qr[:, :, None, :] - ci[None] * qi[:, :, None, :]
        wi = cr[None] * qi[:, :, None, :] + ci[None] * qr[:, :, None, :]
        fix = lambda w: jnp.transpose(w, (1, 3, 0, 2)).reshape(G, P, T * Gi)
        return fix(wr), fix(-wi)

    of_re, of_im = w_out(0, pw_re[1:, 0], pw_im[1:, 0])
    ob_re, ob_im = w_out(1, pw_re[:0:-1, 1], pw_im[:0:-1, 1])
    zero = jnp.zeros_like(of_re)
    wout = jnp.concatenate([of_re, zero, of_im, zero, zero, ob_re, zero, ob_im], axis=1)

    mt = jnp.transpose(m_intra, (0, 2, 1))
    win = jnp.concatenate(win_re + win_im, axis=-1)
    woutt = jnp.transpose(wout, (0, 2, 1))
    lanes = lambda p: jnp.concatenate([p[T, 0], p[T, 1]], axis=-1)
    nblk = G // S5_GROUPS_PER_STEP
    blocked = lambda w: w.reshape((nblk, S5_GROUPS_PER_STEP) + w.shape[1:])
    return (blocked(mt.astype(BF16)), blocked(win.astype(BF16)), blocked(woutt.astype(BF16)),
            blocked(lanes(pw_re)), blocked(lanes(pw_im)))


def _s5_scan(u, ops):
    B, L, D = u.shape
    T, P, ng = S5_CHUNK, SSM_STATE, S5_GROUPS_PER_STEP
    C = L // T
    TI = T * SSM_GROUP
    lanes = ng * SSM_GROUP
    rows = ng * (C + S5_PITCH_PAD)
    seq = pl.BlockSpec((None, L, lanes), lambda b, j: (b, 0, j))
    op = lambda r, c: pl.BlockSpec((None, ng, r, c), lambda b, j: (j, 0, 0, 0))
    coef = pl.BlockSpec((None, ng, 2 * P), lambda b, j: (j, 0, 0))
    return pl.pallas_call(
        _s5_kernel,
        grid=(B, D // lanes),
        in_specs=[seq, op(TI, TI), op(TI, TI), op(TI, 2 * TI), coef, coef],
        out_specs=seq,
        out_shape=jax.ShapeDtypeStruct((B, L, D), F32),
        scratch_shapes=[pltpu.VMEM((ng, TI, C), BF16), pltpu.VMEM((ng, TI, C), F32),
                        pltpu.VMEM((rows, 2 * P), F32), pltpu.VMEM((rows, 2 * P), F32),
                        pltpu.VMEM((4, rows, 2 * P), F32)],
        compiler_params=_cparams("parallel", "parallel"),
    )(u, *ops)


def _na_kernel(q_ref, k_ref, v_ref, g_ref, bias_ref, o_ref):
    W, kr, dh = GRID_W, NA_ROWS, NA_HEAD_DIM
    n_rows = k_ref.shape[0] // W
    per_step = q_ref.shape[0] // W
    j = pl.program_id(2)
    first = lax.broadcasted_iota(jnp.int32, (W, 2 * dh), 1) < dh

    def scores(i):
        r = j * per_step + i
        rs = jnp.clip(r - kr // 2, 0, n_rows - kr)
        var = rs - r + (NA_ROWS - 1)
        keys = pl.ds(pl.multiple_of(rs * W, W), kr * W)
        q2 = q_ref[i * W:(i + 1) * W, :]
        zero = jnp.zeros_like(q2)
        qm = jnp.concatenate([jnp.where(first, q2, zero), jnp.where(first, zero, q2)], axis=0)
        s = lax.dot_general(qm, k_ref[keys, :], (((1,), (1,)), ((), ())), preferred_element_type=F32)
        bias = jnp.concatenate([bias_ref[var + 2 * c] for c in range(kr // 2)], axis=1)
        return s + bias, keys

    def attend(i, s, keys):
        p = jnp.exp2(s - jnp.max(s, axis=-1, keepdims=True))
        den = jnp.sum(p, axis=-1, keepdims=True)
        o2 = jnp.dot(p.astype(BF16), v_ref[keys, :], preferred_element_type=F32) / den
        o = jnp.where(first, o2[:W], o2[W:])
        qrow = slice(i * W, (i + 1) * W)
        o_ref[qrow, :] = (o * _silu(g_ref[qrow, :])).astype(BF16)

    pending = [scores(i) for i in range(NA_SCORE_LEAD)]
    for i in range(per_step):
        if i + NA_SCORE_LEAD < per_step:
            pending.append(scores(i + NA_SCORE_LEAD))
        attend(i, *pending.pop(0))


def _na_bias_table(rel_bias):
    W, R = GRID_W, NA_ROWS
    c_idx = jnp.arange(W)
    col_start = jnp.clip(c_idx - NA_COLS // 2, 0, W - NA_COLS)
    col_valid = ((c_idx[None, :] >= col_start[:, None])
                 & (c_idx[None, :] < col_start[:, None] + NA_COLS))
    offset = c_idx[None, None, :] - c_idx[None, :, None] + NA_COLS - 1
    onehot = (jnp.arange(2 * NA_COLS - 1)[:, None, None] == offset).astype(F32)
    col_bias = jnp.einsum('hdm,mqk->dhqk', rel_bias.astype(F32) * LOG2E, onehot,
                          precision=lax.Precision.HIGHEST)
    col_bias = jnp.where(col_valid[None, None], col_bias, NEG_INF)
    rows = col_bias.reshape(2 * R - 1, NA_HEADS // 2, 2 * W, W)
    return jnp.concatenate([rows[:-1], rows[1:]], axis=-1)


def _neighbourhood_attention(q, k, v, g, bias_tab):
    B, L, D = q.shape
    rows = L // GRID_W
    per_step = min(NA_ROWS_PER_STEP, rows)
    assert rows >= NA_ROWS and rows % per_step == 0
    tq = per_step * GRID_W
    lanes = 2 * NA_HEAD_DIM
    blk = pl.BlockSpec((None, tq, lanes), lambda b, h, j: (b, j, h))
    seq = pl.BlockSpec((None, L, lanes), lambda b, h, j: (b, 0, h))
    return pl.pallas_call(
        _na_kernel,
        grid=(B, NA_HEADS // 2, rows // per_step),
        in_specs=[blk, seq, seq, blk,
                  pl.BlockSpec((2 * NA_ROWS - 2, None, 2 * GRID_W, 2 * GRID_W),
                               lambda b, h, j: (0, h, 0, 0))],
        out_specs=blk,
        out_shape=jax.ShapeDtypeStruct((B, L, D), BF16),
        compiler_params=_cparams("parallel", "parallel", "arbitrary"),
    )(q, k, v, g, bias_tab)


def _rotary_tables(L):
    dh = RET_HEAD_DIM
    inv = ROPE_BASE ** (-jnp.arange(0, dh, 2, dtype=F32) / dh)
    ang = jnp.arange(L, dtype=F32)[:, None] * inv[None, :]
    cos, sin = jnp.cos(ang), jnp.sin(ang)
    return jnp.concatenate([cos, cos], axis=1), jnp.concatenate([-sin, sin], axis=1)


def _trunk(x, c, p):
    B, L, D = x.shape
    mods = _modulation(c, p["w_mod"], p["b_mod"]).reshape(DEPTH, B, 3, D)
    cos2, sin2 = p["rope"][L]
    kind =lambda i: "even" if i % 2 == 0 else "odd"

    def open_args(i):
        g_pre = p["norm_pre"][i].reshape(1, D)
        if kind(i) == "even":
            return (mods[i], g_pre, p["w_in_ab"][i // 2], cos2, sin2)
        return (mods[i], g_pre, p["w_in_c"][i // 2])

    opened = _stage(B, L, None, (x,), kind(0), open_args(0))
    for i in range(DEPTH):
        j = i // 2
        g_post = p["norm_post"][i].reshape(1, D)
        if kind(i) == "even":
            q, k, v, ga, u, gb = opened
            oa = _retention(q, k, v, ga)
            ys = _s5_scan(u, p["s5_ops"][j])
            close_args = (oa, ys, u, gb, x, mods[i], g_post, p["ssm_d"][j].reshape(1, D_SSM),
                          p["ssm_w_glu"][j], p["w_out_ab"][j])
        else:
            q, k, v, g = opened
            o = _neighbourhood_attention(q, k, v, g, p["na_bias"][j])
            close_args = (o, x, mods[i], g_post, p["w_out_c"][j])
        last = i + 1 == DEPTH
        res = _stage(B, L, kind(i), close_args, None if last else kind(i + 1),
                     () if last else open_args(i + 1))
        x, opened = res[0], res[1:]
    return x


def kernel(x_prompt, x_sample, c_prompt, c_sample, norm_pre, norm_post, w_mod, b_mod, w_in_ab, w_out_ab, ssm_a_re, ssm_a_im, ssm_log_step, ssm_b_re, ssm_b_im, ssm_c_re, ssm_c_im, ssm_d, ssm_w_glu, w_in_c, w_out_c, na_rel_bias):
    n_even, n_odd = w_in_ab.shape[0], w_in_c.shape[0]
    p = {
        "norm_pre": norm_pre, "norm_post": norm_post, "w_mod": w_mod, "b_mod": b_mod,
        "w_in_ab": w_in_ab.astype(BF16), "w_out_ab": w_out_ab.astype(BF16),
        "ssm_d": ssm_d, "ssm_w_glu": ssm_w_glu.astype(BF16),
        "w_in_c": w_in_c.astype(BF16), "w_out_c": w_out_c.astype(BF16),
        "s5_ops": [_s5_operators(ssm_a_re[j], ssm_a_im[j], ssm_log_step[j], ssm_b_re[j], ssm_b_im[j],
                                 ssm_c_re[j], ssm_c_im[j]) for j in range(n_even)],
        "na_bias": [_na_bias_table(na_rel_bias[j]) for j in range(n_odd)],
        "rope": {L: _rotary_tables(L) for L in {x_prompt.shape[1], x_sample.shape[1]}},
    }
    return _trunk(x_prompt, c_prompt, p), _trunk(x_sample, c_sample, p)
```

```python
import functools
import math

import jax
import jax.numpy as jnp
from jax import lax
from jax.experimental import pallas as pl
from jax.experimental.pallas import tpu as pltpu

F32 = jnp.float32
BF16 = jnp.bfloat16

D_MODEL = 1024
DEPTH = 4
GRID_W = 64
D_RET = 512
RET_HEADS = 4
RET_HEAD_DIM = 128
RET_CHUNK = 128
D_SSM = 512
SSM_GROUP = 16
SSM_GROUPS = 32
SSM_STATE = 64
EVEN_IN = 4 * D_RET + 2 * D_SSM
NA_HEADS = 16
NA_HEAD_DIM = 64
NA_ROWS = 8
NA_COLS = 16
ODD_IN = 4 * D_MODEL
ROPE_BASE = 10000.0
EPS = 1e-6
NEG_INF = -1e30
LOG2E = 1.4426950408889634

S5_CHUNK = 16
S5_GROUPS_PER_STEP = 8
S5_PITCH_PAD = 8
S5_SCAN_UNROLL = 8
S5_PHASE_UNROLL = 4
TOKEN_TILE = 512
NA_ROWS_PER_STEP = 64
NA_SCORE_LEAD = 3
RET_UNROLL = 32
VMEM_LIMIT = 52 * 1024 * 1024


def _cparams(*sem):
    return pltpu.CompilerParams(dimension_semantics=sem, vmem_limit_bytes=VMEM_LIMIT)


def _sigmoid(x):
    return 0.5 * jnp.tanh(0.5 * x) + 0.5


def _silu(x):
    return x * _sigmoid(x)


def _gelu_tanh(x):
    c = math.sqrt(2.0 / math.pi)
    return 0.5 * x * (1.0 + jnp.tanh(c * (x + 0.044715 * (x * x * x))))


def _mod_kernel(c_ref, w_ref, b_ref, o_ref):
    a = _silu(c_ref[...])
    o_ref[...] = jnp.dot(a, w_ref[...], preferred_element_type=F32) + b_ref[...]


def _modulation(c, w_mod, b_mod):
    B = c.shape[0]
    tn = 1024
    return pl.pallas_call(
        _mod_kernel,
        grid=(DEPTH, 3 * D_MODEL // tn),
        in_specs=[
            pl.BlockSpec((B, D_MODEL), lambda i, n: (0, 0)),
            pl.BlockSpec((None, D_MODEL, tn), lambda i, n: (i, 0, n)),
            pl.BlockSpec((None, 1, tn), lambda i, n: (i, 0, n)),
        ],
        out_specs=pl.BlockSpec((None, B, tn), lambda i, n: (i, 0, n)),
        out_shape=jax.ShapeDtypeStruct((DEPTH, B, 3 * D_MODEL), F32),
        compiler_params=_cparams("arbitrary", "arbitrary"),
    )(c, w_mod, b_mod.reshape(DEPTH, 1, 3 * D_MODEL))


def _prenorm(x, mod_ref, g_ref):
    gain = g_ref[...] * (1.0 + mod_ref[1:2, :])
    y = x * lax.rsqrt(jnp.mean(x * x, axis=-1, keepdims=True) + EPS)
    return (y * gain + mod_ref[0:1, :]).astype(BF16)


def _even_open(x, mod_ref, g_ref, w_ref, cos_ref, sin_ref, q_ref, k_ref, v_ref, ga_ref, u_ref, gb_ref):
    h = _prenorm(x, mod_ref, g_ref)
    cos2, sin2 = cos_ref[...], sin_ref[...]

    def proj(c):
        return jnp.dot(h, w_ref[:, c * D_RET:(c + 1) * D_RET], preferred_element_type=F32)

    def rotary(z):
        cols = []
        for hd in range(RET_HEADS):
            zh = z[:, hd * RET_HEAD_DIM:(hd + 1) * RET_HEAD_DIM]
            cols.append(zh * cos2 + pltpu.roll(zh, RET_HEAD_DIM // 2, 1) * sin2)
        return jnp.concatenate(cols, axis=1)

    q_ref[...] = rotary(proj(0)).astype(BF16)
    k_ref[...] = (rotary(proj(1)) * (RET_HEAD_DIM ** -0.5)).astype(BF16)
    v_ref[...] = proj(2).astype(BF16)
    ga_ref[...] = proj(3)
    u_ref[...] = proj(4)
    gb_ref[...] = proj(5)


def _odd_open(x, mod_ref, g_ref, w_ref, q_ref, k_ref, v_ref, gate_ref):
    h = _prenorm(x, mod_ref, g_ref)
    half = D_MODEL // 2

    def proj(c):
        return jnp.dot(h, w_ref[:, c * half:(c + 1) * half], preferred_element_type=F32)

    for c in range(2):
        sl = slice(c * half, (c + 1) * half)
        q_ref[:, sl] = (proj(c) * (NA_HEAD_DIM ** -0.5 * LOG2E)).astype(BF16)
        k_ref[:, sl] = proj(2 + c).astype(BF16)
        v_ref[:, sl] = proj(4 + c).astype(BF16)
        gate_ref[:, sl] = proj(6 + c)


def _residual(y, x_ref, mod_ref, g_ref):
    gain = g_ref[...] * mod_ref[2:3, :]
    yn = y * lax.rsqrt(jnp.mean(y * y, axis=-1, keepdims=True) + EPS)
    return x_ref[...] + yn * gain


def _even_close(oa_ref, ys_ref, u_ref, gb_ref, x_ref, mod_ref, g_ref, d_ref, wglu_ref, wout_ref):
    y = _gelu_tanh(ys_ref[...] + d_ref[...] * u_ref[...])
    y = y * _sigmoid(jnp.dot(y.astype(BF16), wglu_ref[...], preferred_element_type=F32))
    ob = (y * _silu(gb_ref[...])).astype(BF16)
    out = jnp.dot(jnp.concatenate([oa_ref[...], ob], axis=1), wout_ref[...], preferred_element_type=F32)
    return _residual(out, x_ref, mod_ref, g_ref)


def _odd_close(o_ref, x_ref, mod_ref, g_ref, wout_ref):
    out = jnp.dot(o_ref[...], wout_ref[...], preferred_element_type=F32)
    return _residual(out, x_ref, mod_ref, g_ref)


_CLOSE = {"even": _even_close, "odd": _odd_close}
_OPEN = {"even": _even_open, "odd": _odd_open}


def _stage_kernel(*refs, close, open_, n_close, n_open):
    ins, outs = refs[:n_close + n_open], refs[n_close + n_open:]
    if close is None:
        x = ins[0][...]
    else:
        x = _CLOSE[close](*ins[:n_close])
        outs[0][...] = x
        outs = outs[1:]
    if open_ is not None:
        _OPEN[open_](x, *ins[n_close:], *outs)


def _stage(B, L, close, close_args, open_, open_args):
    D, tm = D_MODEL, TOKEN_TILE
    tok = lambda w: pl.BlockSpec((None, tm, w), lambda b, t: (b, t, 0))
    mod = pl.BlockSpec((None, 3, D), lambda b, t: (b, 0, 0))
    const = lambda r, c: pl.BlockSpec((r, c), lambda b, t: (0, 0), pipeline_mode=pl.Buffered(1))
    rope = pl.BlockSpec((tm, RET_HEAD_DIM), lambda b, t: (t, 0))
    act = lambda w, dt: jax.ShapeDtypeStruct((B, L, w), dt)
    close_specs = {
        None: [tok(D)],
        "even": [tok(D_RET), tok(D_SSM), tok(D_SSM), tok(D_SSM), tok(D), mod,
                 const(1, D), const(1, D_SSM), const(D_SSM, D_SSM), const(D, D)],
        "odd": [tok(D), tok(D), mod, const(1, D), const(D, D)],
    }[close]
    open_specs = {
        None: [],
        "even": [mod, const(1, D), const(D, EVEN_IN), rope, rope],
        "odd": [mod, const(1, D), const(D, ODD_IN)],
    }[open_]
    outs = {
        None: [],
        "even": [(D_RET, BF16)] * 3 + [(D_RET, F32)] * 3,
        "odd": [(D, BF16)] * 3 + [(D, F32)],
    }[open_]
    if close is not None:
        outs = [(D, F32)] + outs
    return pl.pallas_call(
        functools.partial(_stage_kernel, close=close, open_=open_,
                          n_close=len(close_specs), n_open=len(open_specs)),
        grid=(B, L // tm),
        in_specs=close_specs + open_specs,
        out_specs=[tok(w) for w, _ in outs],
        out_shape=[act(w, dt) for w, dt in outs],
        compiler_params=_cparams("parallel", "parallel"),
    )(*close_args, *open_args)


def _ret_kernel(q_ref, k_ref, v_ref, ga_ref, intra_ref, qf_ref, qb_ref, kf_ref, kb_ref, dec_ref,
                o_ref, stb_ref):
    cs = RET_CHUNK
    n = q_ref.shape[0] // cs
    dec = dec_ref[...]
    tdot = lambda a, b: lax.dot_general(a, b, (((0,), (0,)), ((), ())), preferred_element_type=F32)

    def rows(c):
        return pl.ds(pl.multiple_of(c * cs, cs), cs)

    ways = min(RET_UNROLL, n)
    state0 = jnp.zeros((RET_HEAD_DIM, RET_HEAD_DIM), F32)

    def rev(it, st):
        chunks = [n - 1 - (it * ways + w) for w in range(ways)]
        kvs = [tdot((k_ref[rows(c), :].astype(F32) * kb_ref[...]).astype(BF16), v_ref[rows(c), :])
               for c in chunks]
        for c, kv in zip(chunks, kvs):
            stb_ref[c] = st.astype(BF16)
            st = dec * st + kv
        return st

    lax.fori_loop(0, n // ways, rev, state0)

    def fwd(it, st):
        chunks = [it * ways + w for w in range(ways)]
        q = [q_ref[rows(c), :] for c in chunks]
        k = [k_ref[rows(c), :] for c in chunks]
        v = [v_ref[rows(c), :] for c in chunks]
        s = [lax.dot_general(q[w], k[w], (((1,), (1,)), ((), ())), preferred_element_type=F32)
             * intra_ref[...] for w in range(ways)]
        kvs = [tdot((k[w].astype(F32) * kf_ref[...]).astype(BF16), v[w]) for w in range(ways)]
        before = []
        for kv in kvs:
            before.append(st.astype(BF16))
            st = dec * st + kv
        outs = []
        for w, c in enumerate(chunks):
            qf32 = q[w].astype(F32)
            lhs = jnp.concatenate([s[w].astype(BF16), (qf32 * qf_ref[...]).astype(BF16),
                                   (qf32 * qb_ref[...]).astype(BF16)], axis=1)
            rhs = jnp.concatenate([v[w], before[w], stb_ref[c]], axis=0)
            outs.append(jnp.dot(lhs, rhs, preferred_element_type=F32))
        for c, o in zip(chunks, outs):
            mu = jnp.mean(o, axis=-1, keepdims=True)
            d = o - mu
            hn = d * lax.rsqrt(jnp.mean(d * d, axis=-1, keepdims=True) + EPS)
            o_ref[rows(c), :] = (hn * _silu(ga_ref[rows(c), :])).astype(BF16)
        return st

    lax.fori_loop(0, n // ways, fwd, state0)


def _retention_tables():
    H, cs, dk = RET_HEADS, RET_CHUNK, RET_HEAD_DIM
    log_g = jnp.log1p(-jnp.exp2(-5.0 - jnp.arange(H, dtype=F32)))
    pos = jnp.arange(cs, dtype=F32)
    intra = jnp.exp(jnp.abs(pos[:, None] - pos[None, :])[None] * log_g[:, None, None])
    col = lambda e: jnp.broadcast_to(jnp.exp(e[:, None] * log_g[None]).T[:, :, None], (H, cs, dk))
    q_fwd, q_bwd = col(pos), col(cs - 1.0 - pos)
    k_fwd, k_bwd = col(cs - pos), col(pos + 1.0)
    decay = jnp.broadcast_to(jnp.exp(cs * log_g)[:, None, None], (H, dk, dk))
    return intra, q_fwd, q_bwd, k_fwd, k_bwd, decay


def _retention(q, k, v, ga):
    B, L, _ = q.shape
    dk = RET_HEAD_DIM
    seq = pl.BlockSpec((None, L, dk), lambda b, h: (b, 0, h))
    tab = pl.BlockSpec((None, RET_CHUNK, dk), lambda b, h: (h, 0, 0))
    return pl.pallas_call(
        _ret_kernel,
        grid=(B, RET_HEADS),
        in_specs=[seq, seq, seq, seq] + [tab] * 6,
        out_specs=seq,
        out_shape=jax.ShapeDtypeStruct((B, L, D_RET), BF16),
        scratch_shapes=[pltpu.VMEM((L // RET_CHUNK, dk, dk), BF16)],
        compiler_params=_cparams("parallel", "parallel"),
    )(q, k, v, ga, *_retention_tables())


def _s5_kernel(u_ref, mt_ref, win_ref, woutt_ref, are_ref, aim_ref, y_ref,
               xt_ref, yt_ref, kvre_ref, kvim_ref, h_ref):
    T, Gi, P = S5_CHUNK, SSM_GROUP, SSM_STATE
    C = u_ref.shape[0] // T
    pitch = C + S5_PITCH_PAD
    ng = xt_ref.shape[0]

    def relayout_in(t, carry):
        a_t = u_ref[pl.ds(t, C, stride=T), :].T
        rows = pl.ds(pl.multiple_of(t * Gi, Gi), Gi)
        for g in range(ng):
            xt_ref[g, rows, :] = a_t[g * Gi:(g + 1) * Gi, :].astype(BF16)
        return carry

    lax.fori_loop(0, T, relayout_in, 0, unroll=S5_PHASE_UNROLL)

    def chunk_matmuls(g, carry):
        xt = xt_ref[g]
        yt_ref[g] = jnp.dot(mt_ref[g], xt, preferred_element_type=F32)
        kv = lax.dot_general(xt, win_ref[g], (((0,), (0,)), ((), ())),
                             preferred_element_type=F32)
        rows = pl.ds(pl.multiple_of(g * pitch, 8), C)
        kvre_ref[rows, :] = kv[:, :2 * P]
        kvim_ref[rows, :] = kv[:, 2 * P:]
        return carry

    lax.fori_loop(0, ng, chunk_matmuls, 0, unroll=S5_PHASE_UNROLL)

    are, aim = are_ref[...], aim_ref[...]

    def step(k, carry):
        sf_re, sf_im, sb_re, sb_im = carry
        rf = pl.ds(k, ng, stride=pitch)
        rb = pl.ds(C - 1 - k, ng, stride=pitch)
        h_ref[0, rf, :] = sf_re
        h_ref[1, rf, :] = sf_im
        h_ref[2, rb, :] = sb_re
        h_ref[3, rb, :] = sb_im
        nf_re = are * sf_re - aim * sf_im + kvre_ref[rf, :]
        nf_im = are * sf_im + aim * sf_re + kvim_ref[rf, :]
        nb_re = are * sb_re - aim * sb_im + kvre_ref[rb, :]
        nb_im = are * sb_im + aim * sb_re + kvim_ref[rb, :]
        return nf_re, nf_im, nb_re, nb_im

    z = jnp.zeros((ng, 2 * P), F32)
    lax.fori_loop(0, C, step, (z, z, z, z), unroll=S5_SCAN_UNROLL)

    def carried_outputs(g, carry):
        rows = pl.ds(pl.multiple_of(g * pitch, 8), C)
        h = jnp.concatenate([h_ref[i, rows, :] for i in range(4)], axis=1).astype(BF16)
        yt_ref[g] += lax.dot_general(woutt_ref[g], h, (((1,), (1,)), ((), ())),
                                     preferred_element_type=F32)
        return carry

    lax.fori_loop(0, ng, carried_outputs, 0, unroll=S5_PHASE_UNROLL)

    def relayout_out(t, carry):
        rows = pl.ds(pl.multiple_of(t * Gi, Gi), Gi)
        b_t = jnp.concatenate([yt_ref[g, rows, :] for g in range(ng)], axis=0)
        y_ref[pl.ds(t, C, stride=T), :] = b_t.T
        return carry

    lax.fori_loop(0, T, relayout_out, 0, unroll=S5_PHASE_UNROLL)


def _s5_operators(a_re, a_im, log_step, b_re, b_im, c_re, c_im):
    T, G, P, Gi = S5_CHUNK, SSM_GROUPS, SSM_STATE, SSM_GROUP
    hp = lax.Precision.HIGHEST
    a_re, a_im = a_re.astype(F32), a_im.astype(F32)
    delta = jnp.exp(log_step.astype(F32))[..., None]
    z_re, z_im = a_re * delta, a_im * delta
    mag = jnp.exp(z_re)
    abar_re, abar_im = mag * jnp.cos(z_im), mag * jnp.sin(z_im)
    den = a_re * a_re + a_im * a_im
    n_re, n_im = abar_re - 1.0, abar_im
    f_re = (n_re * a_re + n_im * a_im) / den
    f_im = (n_im * a_re - n_re * a_im) / den
    b_re, b_im = b_re.astype(F32), b_im.astype(F32)
    bb_re = f_re[..., None] * b_re - f_im[..., None] * b_im
    bb_im = f_re[..., None] * b_im + f_im[..., None] * b_re
    c_re, c_im = c_re.astype(F32), c_im.astype(F32)

    pr, pi = [jnp.ones_like(abar_re)], [jnp.zeros_like(abar_re)]
    for _ in range(T):
        pr, pi = (pr + [pr[-1] * abar_re - pi[-1] * abar_im],
                  pi + [pr[-1] * abar_im + pi[-1] * abar_re])
    pw_re, pw_im = jnp.stack(pr), jnp.stack(pi)

    w_re = pw_re[:T, ..., None] * bb_re[None] - pw_im[:T, ..., None] * bb_im[None]
    w_im = pw_re[:T, ..., None] * bb_im[None] + pw_im[:T, ..., None] * bb_re[None]
    kern = (jnp.einsum('xgip,dxgpj->dxgij', c_re, w_re, precision=hp)
            - jnp.einsum('xgip,dxgpj->dxgij', c_im, w_im, precision=hp))

    lag = jnp.arange(T)[None, :] - jnp.arange(T)[:, None]
    place_f = (jnp.arange(T)[:, None, None] == lag).astype(F32)
    place_b = (jnp.arange(T)[:, None, None] == -lag).astype(F32)
    both = (jnp.einsum('dst,dgij->gsjti', place_f, kern[:, 0], precision=hp)
            + jnp.einsum('dst,dgij->gsjti', place_b, kern[:, 1], precision=hp))
    m_intra = both.reshape(G, T * Gi, T * Gi)

    def w_in(w):
        return jnp.transpose(w, (1, 0, 3, 2)).reshape(G, T * Gi, P)

    win_re = [w_in(w_re[::-1, 0]), w_in(w_re[:, 1])]
    win_im = [w_in(w_im[::-1, 0]), w_in(w_im[:, 1])]

    def w_out(direction, qr, qi):
        cr, ci = c_re[direction], c_im[direction]
        wr = cr[None] * qr[:, :, None, :] - ci[None] * qi[:, :, None, :]
        wi = cr[None] * qi[:, :, None, :] + ci[None] * qr[:, :, None, :]
        fix = lambda w: jnp.transpose(w, (1, 3, 0, 2)).reshape(G, P, T * Gi)
        return fix(wr), fix(-wi)

    of_re, of_im = w_out(0, pw_re[1:, 0], pw_im[1:, 0])
    ob_re, ob_im = w_out(1, pw_re[:0:-1, 1], pw_im[:0:-1, 1])
    zero = jnp.zeros_like(of_re)
    wout = jnp.concatenate([of_re, zero, of_im, zero, zero, ob_re, zero, ob_im], axis=1)

    mt = jnp.transpose(m_intra, (0, 2, 1))
    win = jnp.concatenate(win_re + win_im, axis=-1)
    woutt = jnp.transpose(wout, (0, 2, 1))
    lanes = lambda p: jnp.concatenate([p[T, 0], p[T, 1]], axis=-1)
    nblk = G // S5_GROUPS_PER_STEP
    blocked = lambda w: w.reshape((nblk, S5_GROUPS_PER_STEP) + w.shape[1:])
    return (blocked(mt.astype(BF16)), blocked(win.astype(BF16)), blocked(woutt.astype(BF16)),
            blocked(lanes(pw_re)), blocked(lanes(pw_im)))


def _s5_scan(u, ops):
    B, L, D = u.shape
    T, P, ng = S5_CHUNK, SSM_STATE, S5_GROUPS_PER_STEP
    C = L // T
    TI = T * SSM_GROUP
    lanes = ng * SSM_GROUP
    rows = ng * (C + S5_PITCH_PAD)
    seq = pl.BlockSpec((None, L, lanes), lambda b, j: (b, 0, j))
    op = lambda r, c: pl.BlockSpec((None, ng, r, c), lambda b, j: (j, 0, 0, 0))
    coef = pl.BlockSpec((None, ng, 2 * P), lambda b, j: (j, 0, 0))
    return pl.pallas_call(
        _s5_kernel,
        grid=(B, D // lanes),
        in_specs=[seq, op(TI, TI), op(TI, TI), op(TI, 2 * TI), coef, coef],
        out_specs=seq,
        out_shape=jax.ShapeDtypeStruct((B, L, D), F32),
        scratch_shapes=[pltpu.VMEM((ng, TI, C), BF16), pltpu.VMEM((ng, TI, C), F32),
                        pltpu.VMEM((rows, 2 * P), F32), pltpu.VMEM((rows, 2 * P), F32),
                        pltpu.VMEM((4, rows, 2 * P), F32)],
        compiler_params=_cparams("parallel", "parallel"),
    )(u, *ops)


def _na_kernel(q_ref, k_ref, v_ref, g_ref, bias_ref, o_ref):
    W, kr, dh = GRID_W, NA_ROWS, NA_HEAD_DIM
    n_rows = k_ref.shape[0] // W
    per_step = q_ref.shape[0] // W
    j = pl.program_id(2)
    first = lax.broadcasted_iota(jnp.int32, (W, 2 * dh), 1) < dh

    def scores(i):
        r = j * per_step + i
        rs = jnp.clip(r - kr // 2, 0, n_rows - kr)
        var = rs - r + (NA_ROWS - 1)
        keys = pl.ds(pl.multiple_of(rs * W, W), kr * W)
        q2 = q_ref[i * W:(i + 1) * W, :]
        zero = jnp.zeros_like(q2)
        qm = jnp.concatenate([jnp.where(first, q2, zero), jnp.where(first, zero, q2)], axis=0)
        s = lax.dot_general(qm, k_ref[keys, :], (((1,), (1,)), ((), ())), preferred_element_type=F32)
        bias = jnp.concatenate([bias_ref[var + 2 * c] for c in range(kr // 2)], axis=1)
        return s + bias, keys

    def attend(i, s, keys):
        p = jnp.exp2(s - jnp.max(s, axis=-1, keepdims=True))
        den = jnp.sum(p, axis=-1, keepdims=True)
        o2 = jnp.dot(p.astype(BF16), v_ref[keys, :], preferred_element_type=F32) / den
        o = jnp.where(first, o2[:W], o2[W:])
        qrow = slice(i * W, (i + 1) * W)
        o_ref[qrow, :] = (o * _silu(g_ref[qrow, :])).astype(BF16)

    pending = [scores(i) for i in range(NA_SCORE_LEAD)]
    for i in range(per_step):
        if i + NA_SCORE_LEAD < per_step:
            pending.append(scores(i + NA_SCORE_LEAD))
        attend(i, *pending.pop(0))


def _na_bias_table(rel_bias):
    W, R = GRID_W, NA_ROWS
    c_idx = jnp.arange(W)
    col_start = jnp.clip(c_idx - NA_COLS // 2, 0, W - NA_COLS)
    col_valid = ((c_idx[None, :] >= col_start[:, None])
                 & (c_idx[None, :] < col_start[:, None] + NA_COLS))
    offset = c_idx[None, None, :] - c_idx[None, :, None] + NA_COLS - 1
    onehot = (jnp.arange(2 * NA_COLS - 1)[:, None, None] == offset).astype(F32)
    col_bias = jnp.einsum('hdm,mqk->dhqk', rel_bias.astype(F32) * LOG2E, onehot,
                          precision=lax.Precision.HIGHEST)
    col_bias = jnp.where(col_valid[None, None], col_bias, NEG_INF)
    rows = col_bias.reshape(2 * R - 1, NA_HEADS // 2, 2 * W, W)
    return jnp.concatenate([rows[:-1], rows[1:]], axis=-1)


def _neighbourhood_attention(q, k, v, g, bias_tab):
    B, L, D = q.shape
    rows = L // GRID_W
    per_step = min(NA_ROWS_PER_STEP, rows)
    assert rows >= NA_ROWS and rows % per_step == 0
    tq = per_step * GRID_W
    lanes = 2 * NA_HEAD_DIM
    blk = pl.BlockSpec((None, tq, lanes), lambda b, h, j: (b, j, h))
    seq = pl.BlockSpec((None, L, lanes), lambda b, h, j: (b, 0, h))
    return pl.pallas_call(
        _na_kernel,
        grid=(B, NA_HEADS // 2, rows // per_step),
        in_specs=[blk, seq, seq, blk,
                  pl.BlockSpec((2 * NA_ROWS - 2, None, 2 * GRID_W, 2 * GRID_W),
                               lambda b, h, j: (0, h, 0, 0))],
        out_specs=blk,
        out_shape=jax.ShapeDtypeStruct((B, L, D), BF16),
        compiler_params=_cparams("parallel", "parallel", "arbitrary"),
    )(q, k, v, g, bias_tab)


def _rotary_tables(L):
    dh = RET_HEAD_DIM
    inv = ROPE_BASE ** (-jnp.arange(0, dh, 2, dtype=F32) / dh)
    ang = jnp.arange(L, dtype=F32)[:, None] * inv[None, :]
    cos, sin = jnp.cos(ang), jnp.sin(ang)
    return jnp.concatenate([cos, cos], axis=1), jnp.concatenate([-sin, sin], axis=1)


def _trunk(x, c, p):
    B, L, D = x.shape
    mods = _modulation(c, p["w_mod"], p["b_mod"]).reshape(DEPTH, B, 3, D)
    cos2, sin2 = p["rope"][L]
    kind =lambda i: "even" if i % 2 == 0 else "odd"

    def open_args(i):
        g_pre = p["norm_pre"][i].reshape(1, D)
        if kind(i) == "even":
            return (mods[i], g_pre, p["w_in_ab"][i // 2], cos2, sin2)
        return (mods[i], g_pre, p["w_in_c"][i // 2])

    opened = _stage(B, L, None, (x,), kind(0), open_args(0))
    for i in range(DEPTH):
        j = i // 2
        g_post = p["norm_post"][i].reshape(1, D)
        if kind(i) == "even":
            q, k, v, ga, u, gb = opened
            oa = _retention(q, k, v, ga)
            ys = _s5_scan(u, p["s5_ops"][j])
            close_args = (oa, ys, u, gb, x, mods[i], g_post, p["ssm_d"][j].reshape(1, D_SSM),
                          p["ssm_w_glu"][j], p["w_out_ab"][j])
        else:
            q, k, v, g = opened
            o = _neighbourhood_attention(q, k, v, g, p["na_bias"][j])
            close_args = (o, x, mods[i], g_post, p["w_out_c"][j])
        last = i + 1 == DEPTH
        res = _stage(B, L, kind(i), close_args, None if last else kind(i + 1),
                     () if last else open_args(i + 1))
        x, opened = res[0], res[1:]
    return x


def kernel(x_prompt, x_sample, c_prompt, c_sample, norm_pre, norm_post, w_mod, b_mod, w_in_ab, w_out_ab, ssm_a_re, ssm_a_im, ssm_log_step, ssm_b_re, ssm_b_im, ssm_c_re, ssm_c_im, ssm_d, ssm_w_glu, w_in_c, w_out_c, na_rel_bias):
    n_even, n_odd = w_in_ab.shape[0], w_in_c.shape[0]
    p = {
        "norm_pre": norm_pre, "norm_post": norm_post, "w_mod": w_mod, "b_mod": b_mod,
        "w_in_ab": w_in_ab.astype(BF16), "w_out_ab": w_out_ab.astype(BF16),
        "ssm_d": ssm_d, "ssm_w_glu": ssm_w_glu.astype(BF16),
        "w_in_c": w_in_c.astype(BF16), "w_out_c": w_out_c.astype(BF16),
        "s5_ops": [_s5_operators(ssm_a_re[j], ssm_a_im[j], ssm_log_step[j], ssm_b_re[j], ssm_b_im[j],
                                 ssm_c_re[j], ssm_c_im[j]) for j in range(n_even)],
        "na_bias": [_na_bias_table(na_rel_bias[j]) for j in range(n_odd)],
        "rope": {L: _rotary_tables(L) for L in {x_prompt.shape[1], x_sample.shape[1]}},
    }
    return _trunk(x_prompt, c_prompt, p), _trunk(x_sample, c_sample, p)
```

```python
import functools
import math

import jax
import jax.numpy as jnp
from jax import lax
from jax.experimental import pallas as pl
from jax.experimental.pallas import tpu as pltpu

F32 = jnp.float32
BF16 = jnp.bfloat16

D_MODEL = 1024
DEPTH = 4
GRID_W = 64
D_RET = 512
RET_HEADS = 4
RET_HEAD_DIM = 128
RET_CHUNK = 128
D_SSM = 512
SSM_GROUP = 16
SSM_GROUPS = 32
SSM_STATE = 64
EVEN_IN = 4 * D_RET + 2 * D_SSM
NA_HEADS = 16
NA_HEAD_DIM = 64
NA_ROWS = 8
NA_COLS = 16
ODD_IN = 4 * D_MODEL
ROPE_BASE = 10000.0
EPS = 1e-6
NEG_INF = -1e30
LOG2E = 1.4426950408889634

S5_CHUNK = 16
S5_GROUPS_PER_STEP = 8
S5_PITCH_PAD = 8
S5_SCAN_UNROLL = 8
S5_PHASE_UNROLL = 4
TOKEN_TILE = 512
NA_ROWS_PER_STEP = 64
NA_SCORE_LEAD = 3
RET_UNROLL = 32
VMEM_LIMIT = 52 * 1024 * 1024


def _cparams(*sem):
    return pltpu.CompilerParams(dimension_semantics=sem, vmem_limit_bytes=VMEM_LIMIT)


def _sigmoid(x):
    return 0.5 * jnp.tanh(0.5 * x) + 0.5


def _silu(x):
    return x * _sigmoid(x)


def _gelu_tanh(x):
    c = math.sqrt(2.0 / math.pi)
    return 0.5 * x * (1.0 + jnp.tanh(c * (x + 0.044715 * (x * x * x))))


def _mod_kernel(c_ref, w_ref, b_ref, o_ref):
    a = _silu(c_ref[...])
    o_ref[...] = jnp.dot(a, w_ref[...], preferred_element_type=F32) + b_ref[...]


def _modulation(c, w_mod, b_mod):
    B = c.shape[0]
    tn = 1024
    return pl.pallas_call(
        _mod_kernel,
        grid=(DEPTH, 3 * D_MODEL // tn),
        in_specs=[
            pl.BlockSpec((B, D_MODEL), lambda i, n: (0, 0)),
            pl.BlockSpec((None, D_MODEL, tn), lambda i, n: (i, 0, n)),
            pl.BlockSpec((None, 1, tn), lambda i, n: (i, 0, n)),
        ],
        out_specs=pl.BlockSpec((None, B, tn), lambda i, n: (i, 0, n)),
        out_shape=jax.ShapeDtypeStruct((DEPTH, B, 3 * D_MODEL), F32),
        compiler_params=_cparams("arbitrary", "arbitrary"),
    )(c, w_mod, b_mod.reshape(DEPTH, 1, 3 * D_MODEL))


def _prenorm(x, mod_ref, g_ref):
    gain = g_ref[...] * (1.0 + mod_ref[1:2, :])
    y = x * lax.rsqrt(jnp.mean(x * x, axis=-1, keepdims=True) + EPS)
    return (y * gain + mod_ref[0:1, :]).astype(BF16)


def _even_open(x, mod_ref, g_ref, w_ref, cos_ref, sin_ref, q_ref, k_ref, v_ref, ga_ref, u_ref, gb_ref):
    h = _prenorm(x, mod_ref, g_ref)
    cos2, sin2 = cos_ref[...], sin_ref[...]

    def proj(c):
        return jnp.dot(h, w_ref[:, c * D_RET:(c + 1) * D_RET], preferred_element_type=F32)

    def rotary(z):
        cols = []
        for hd in range(RET_HEADS):
            zh = z[:, hd * RET_HEAD_DIM:(hd + 1) * RET_HEAD_DIM]
            cols.append(zh * cos2 + pltpu.roll(zh, RET_HEAD_DIM // 2, 1) * sin2)
        return jnp.concatenate(cols, axis=1)

    q_ref[...] = rotary(proj(0)).astype(BF16)
    k_ref[...] = (rotary(proj(1)) * (RET_HEAD_DIM ** -0.5)).astype(BF16)
    v_ref[...] = proj(2).astype(BF16)
    ga_ref[...] = proj(3)
    u_ref[...] = proj(4)
    gb_ref[...] = proj(5)


def _odd_open(x, mod_ref, g_ref, w_ref, q_ref, k_ref, v_ref, gate_ref):
    h = _prenorm(x, mod_ref, g_ref)
    half = D_MODEL // 2

    def proj(c):
        return jnp.dot(h, w_ref[:, c * half:(c + 1) * half], preferred_element_type=F32)

    for c in range(2):
        sl = slice(c * half, (c + 1) * half)
        q_ref[:, sl] = (proj(c) * (NA_HEAD_DIM ** -0.5 * LOG2E)).astype(BF16)
        k_ref[:, sl] = proj(2 + c).astype(BF16)
        v_ref[:, sl] = proj(4 + c).astype(BF16)
        gate_ref[:, sl] = proj(6 + c)


def _residual(y, x_ref, mod_ref, g_ref):
    gain = g_ref[...] * mod_ref[2:3, :]
    yn = y * lax.rsqrt(jnp.mean(y * y, axis=-1, keepdims=True) + EPS)
    return x_ref[...] + yn * gain


def _even_close(oa_ref, ys_ref, u_ref, gb_ref, x_ref, mod_ref, g_ref, d_ref, wglu_ref, wout_ref):
    y = _gelu_tanh(ys_ref[...] + d_ref[...] * u_ref[...])
    y = y * _sigmoid(jnp.dot(y.astype(BF16), wglu_ref[...], preferred_element_type=F32))
    ob = (y * _silu(gb_ref[...])).astype(BF16)
    out = jnp.dot(jnp.concatenate([oa_ref[...], ob], axis=1), wout_ref[...], preferred_element_type=F32)
    return _residual(out, x_ref, mod_ref, g_ref)


def _odd_close(o_ref, x_ref, mod_ref, g_ref, wout_ref):
    out = jnp.dot(o_ref[...], wout_ref[...], preferred_element_type=F32)
    return _residual(out, x_ref, mod_ref, g_ref)


_CLOSE = {"even": _even_close, "odd": _odd_close}
_OPEN = {"even": _even_open, "odd": _odd_open}


def _stage_kernel(*refs, close, open_, n_close, n_open):
    ins, outs = refs[:n_close + n_open], refs[n_close + n_open:]
    if close is None:
        x = ins[0][...]
    else:
        x = _CLOSE[close](*ins[:n_close])
        outs[0][...] = x
        outs = outs[1:]
    if open_ is not None:
        _OPEN[open_](x, *ins[n_close:], *outs)


def _stage(B, L, close, close_args, open_, open_args):
    D, tm = D_MODEL, TOKEN_TILE
    tok = lambda w: pl.BlockSpec((None, tm, w), lambda b, t: (b, t, 0))
    mod = pl.BlockSpec((None, 3, D), lambda b, t: (b, 0, 0))
    const = lambda r, c: pl.BlockSpec((r, c), lambda b, t: (0, 0), pipeline_mode=pl.Buffered(1))
    rope = pl.BlockSpec((tm, RET_HEAD_DIM), lambda b, t: (t, 0))
    act = lambda w, dt: jax.ShapeDtypeStruct((B, L, w), dt)
    close_specs = {
        None: [tok(D)],
        "even": [tok(D_RET), tok(D_SSM), tok(D_SSM), tok(D_SSM), tok(D), mod,
                 const(1, D), const(1, D_SSM), const(D_SSM, D_SSM), const(D, D)],
        "odd": [tok(D), tok(D), mod, const(1, D), const(D, D)],
    }[close]
    open_specs = {
        None: [],
        "even": [mod, const(1, D), const(D, EVEN_IN), rope, rope],
        "odd": [mod, const(1, D), const(D, ODD_IN)],
    }[open_]
    outs = {
        None: [],
        "even": [(D_RET, BF16)] * 3 + [(D_RET, F32)] * 3,
        "odd": [(D, BF16)] * 3 + [(D, F32)],
    }[open_]
    if close is not None:
        outs = [(D, F32)] + outs
    return pl.pallas_call(
        functools.partial(_stage_kernel, close=close, open_=open_,
                          n_close=len(close_specs), n_open=len(open_specs)),
        grid=(B, L // tm),
        in_specs=close_specs + open_specs,
        out_specs=[tok(w) for w, _ in outs],
        out_shape=[act(w, dt) for w, dt in outs],
        compiler_params=_cparams("parallel", "parallel"),
    )(*close_args, *open_args)


def _ret_kernel(q_ref, k_ref, v_ref, ga_ref, intra_ref, qf_ref, qb_ref, kf_ref, kb_ref, dec_ref,
                o_ref, stb_ref):
    cs = RET_CHUNK
    n = q_ref.shape[0] // cs
    dec = dec_ref[...]
    tdot = lambda a, b: lax.dot_general(a, b, (((0,), (0,)), ((), ())), preferred_element_type=F32)

    def rows(c):
        return pl.ds(pl.multiple_of(c * cs, cs), cs)

    ways = min(RET_UNROLL, n)
    state0 = jnp.zeros((RET_HEAD_DIM, RET_HEAD_DIM), F32)

    def rev(it, st):
        chunks = [n - 1 - (it * ways + w) for w in range(ways)]
        kvs = [tdot((k_ref[rows(c), :].astype(F32) * kb_ref[...]).astype(BF16), v_ref[rows(c), :])
               for c in chunks]
        for c, kv in zip(chunks, kvs):
            stb_ref[c] = st.astype(BF16)
            st = dec * st + kv
        return st

    lax.fori_loop(0, n // ways, rev, state0)

    def fwd(it, st):
        chunks = [it * ways + w for w in range(ways)]
        q = [q_ref[rows(c), :] for c in chunks]
        k = [k_ref[rows(c), :] for c in chunks]
        v = [v_ref[rows(c), :] for c in chunks]
        s = [lax.dot_general(q[w], k[w], (((1,), (1,)), ((), ())), preferred_element_type=F32)
             * intra_ref[...] for w in range(ways)]
        kvs = [tdot((k[w].astype(F32) * kf_ref[...]).astype(BF16), v[w]) for w in range(ways)]
        before = []
        for kv in kvs:
            before.append(st.astype(BF16))
            st = dec * st + kv
        outs = []
        for w, c in enumerate(chunks):
            qf32 = q[w].astype(F32)
            lhs = jnp.concatenate([s[w].astype(BF16), (qf32 * qf_ref[...]).astype(BF16),
                                   (qf32 * qb_ref[...]).astype(BF16)], axis=1)
            rhs = jnp.concatenate([v[w], before[w], stb_ref[c]], axis=0)
            outs.append(jnp.dot(lhs, rhs, preferred_element_type=F32))
        for c, o in zip(chunks, outs):
            mu = jnp.mean(o, axis=-1, keepdims=True)
            d = o - mu
            hn = d * lax.rsqrt(jnp.mean(d * d, axis=-1, keepdims=True) + EPS)
            o_ref[rows(c), :] = (hn * _silu(ga_ref[rows(c), :])).astype(BF16)
        return st

    lax.fori_loop(0, n // ways, fwd, state0)


def _retention_tables():
    H, cs, dk = RET_HEADS, RET_CHUNK, RET_HEAD_DIM
    log_g = jnp.log1p(-jnp.exp2(-5.0 - jnp.arange(H, dtype=F32)))
    pos = jnp.arange(cs, dtype=F32)
    intra = jnp.exp(jnp.abs(pos[:, None] - pos[None, :])[None] * log_g[:, None, None])
    col = lambda e: jnp.broadcast_to(jnp.exp(e[:, None] * log_g[None]).T[:, :, None], (H, cs, dk))
    q_fwd, q_bwd = col(pos), col(cs - 1.0 - pos)
    k_fwd, k_bwd = col(cs - pos), col(pos + 1.0)
    decay = jnp.broadcast_to(jnp.exp(cs * log_g)[:, None, None], (H, dk, dk))
    return intra, q_fwd, q_bwd, k_fwd, k_bwd, decay


def _retention(q, k, v, ga):
    B, L, _ = q.shape
    dk = RET_HEAD_DIM
    seq = pl.BlockSpec((None, L, dk), lambda b, h: (b, 0, h))
    tab = pl.BlockSpec((None, RET_CHUNK, dk), lambda b, h: (h, 0, 0))
    return pl.pallas_call(
        _ret_kernel,
        grid=(B, RET_HEADS),
        in_specs=[seq, seq, seq, seq] + [tab] * 6,
        out_specs=seq,
        out_shape=jax.ShapeDtypeStruct((B, L, D_RET), BF16),
        scratch_shapes=[pltpu.VMEM((L // RET_CHUNK, dk, dk), BF16)],
        compiler_params=_cparams("parallel", "parallel"),
    )(q, k, v, ga, *_retention_tables())


def _s5_kernel(u_ref, mt_ref, win_ref, woutt_ref, are_ref, aim_ref, y_ref,
               xt_ref, yt_ref, kvre_ref, kvim_ref, h_ref):
    T, Gi, P = S5_CHUNK, SSM_GROUP, SSM_STATE
    C = u_ref.shape[0] // T
    pitch = C + S5_PITCH_PAD
    ng = xt_ref.shape[0]

    def relayout_in(t, carry):
        a_t = u_ref[pl.ds(t, C, stride=T), :].T
        rows = pl.ds(pl.multiple_of(t * Gi, Gi), Gi)
        for g in range(ng):
            xt_ref[g, rows, :] = a_t[g * Gi:(g + 1) * Gi, :].astype(BF16)
        return carry

    lax.fori_loop(0, T, relayout_in, 0, unroll=S5_PHASE_UNROLL)

    def chunk_matmuls(g, carry):
        xt = xt_ref[g]
        yt_ref[g] = jnp.dot(mt_ref[g], xt, preferred_element_type=F32)
        kv = lax.dot_general(xt, win_ref[g], (((0,), (0,)), ((), ())),
                             preferred_element_type=F32)
        rows = pl.ds(pl.multiple_of(g * pitch, 8), C)
        kvre_ref[rows, :] = kv[:, :2 * P]
        kvim_ref[rows, :] = kv[:, 2 * P:]
        return carry

    lax.fori_loop(0, ng, chunk_matmuls, 0, unroll=S5_PHASE_UNROLL)

    are, aim = are_ref[...], aim_ref[...]

    def step(k, carry):
        sf_re, sf_im, sb_re, sb_im = carry
        rf = pl.ds(k, ng, stride=pitch)
        rb = pl.ds(C - 1 - k, ng, stride=pitch)
        h_ref[0, rf, :] = sf_re
        h_ref[1, rf, :] = sf_im
        h_ref[2, rb, :] = sb_re
        h_ref[3, rb, :] = sb_im
        nf_re = are * sf_re - aim * sf_im + kvre_ref[rf, :]
        nf_im = are * sf_im + aim * sf_re + kvim_ref[rf, :]
        nb_re = are * sb_re - aim * sb_im + kvre_ref[rb, :]
        nb_im = are * sb_im + aim * sb_re + kvim_ref[rb, :]
        return nf_re, nf_im, nb_re, nb_im

    z = jnp.zeros((ng, 2 * P), F32)
    lax.fori_loop(0, C, step, (z, z, z, z), unroll=S5_SCAN_UNROLL)

    def carried_outputs(g, carry):
        rows = pl.ds(pl.multiple_of(g * pitch, 8), C)
        h = jnp.concatenate([h_ref[i, rows, :] for i in range(4)], axis=1).astype(BF16)
        yt_ref[g] += lax.dot_general(woutt_ref[g], h, (((1,), (1,)), ((), ())),
                                     preferred_element_type=F32)
        return carry

    lax.fori_loop(0, ng, carried_outputs, 0, unroll=S5_PHASE_UNROLL)

    def relayout_out(t, carry):
        rows = pl.ds(pl.multiple_of(t * Gi, Gi), Gi)
        b_t = jnp.concatenate([yt_ref[g, rows, :] for g in range(ng)], axis=0)
        y_ref[pl.ds(t, C, stride=T), :] = b_t.T
        return carry

    lax.fori_loop(0, T, relayout_out, 0, unroll=S5_PHASE_UNROLL)


def _s5_operators(a_re, a_im, log_step, b_re, b_im, c_re, c_im):
    T, G, P, Gi = S5_CHUNK, SSM_GROUPS, SSM_STATE, SSM_GROUP
    hp = lax.Precision.HIGHEST
    a_re, a_im = a_re.astype(F32), a_im.astype(F32)
    delta = jnp.exp(log_step.astype(F32))[..., None]
    z_re, z_im = a_re * delta, a_im * delta
    mag = jnp.exp(z_re)
    abar_re, abar_im = mag * jnp.cos(z_im), mag * jnp.sin(z_im)
    den = a_re * a_re + a_im * a_im
    n_re, n_im = abar_re - 1.0, abar_im
    f_re = (n_re * a_re + n_im * a_im) / den
    f_im = (n_im * a_re - n_re * a_im) / den
    b_re, b_im = b_re.astype(F32), b_im.astype(F32)
    bb_re = f_re[..., None] * b_re - f_im[..., None] * b_im
    bb_im = f_re[..., None] * b_im + f_im[..., None] * b_re
    c_re, c_im = c_re.astype(F32), c_im.astype(F32)

    pr, pi = [jnp.ones_like(abar_re)], [jnp.zeros_like(abar_re)]
    for _ in range(T):
        pr, pi = (pr + [pr[-1] * abar_re - pi[-1] * abar_im],
                  pi + [pr[-1] * abar_im + pi[-1] * abar_re])
    pw_re, pw_im = jnp.stack(pr), jnp.stack(pi)

    w_re = pw_re[:T, ..., None] * bb_re[None] - pw_im[:T, ..., None] * bb_im[None]
    w_im = pw_re[:T, ..., None] * bb_im[None] + pw_im[:T, ..., None] * bb_re[None]
    kern = (jnp.einsum('xgip,dxgpj->dxgij', c_re, w_re, precision=hp)
            - jnp.einsum('xgip,dxgpj->dxgij', c_im, w_im, precision=hp))

    lag = jnp.arange(T)[None, :] - jnp.arange(T)[:, None]
    place_f = (jnp.arange(T)[:, None, None] == lag).astype(F32)
    place_b = (jnp.arange(T)[:, None, None] == -lag).astype(F32)
    both = (jnp.einsum('dst,dgij->gsjti', place_f, kern[:, 0], precision=hp)
            + jnp.einsum('dst,dgij->gsjti', place_b, kern[:, 1], precision=hp))
    m_intra = both.reshape(G, T * Gi, T * Gi)

    def w_in(w):
        return jnp.transpose(w, (1, 0, 3, 2)).reshape(G, T * Gi, P)

    win_re = [w_in(w_re[::-1, 0]), w_in(w_re[:, 1])]
    win_im = [w_in(w_im[::-1, 0]), w_in(w_im[:, 1])]

    def w_out(direction, qr, qi):
        cr, ci = c_re[direction], c_im[direction]
        wr = cr[None] * qr[:, :, None, :] - ci[None] * qi[:, :, None, :]
        wi = cr[None] * qi[:, :, None, :] + ci[None] * qr[:, :, None, :]
        fix = lambda w: jnp.transpose(w, (1, 3, 0, 2)).reshape(G, P, T * Gi)
        return fix(wr), fix(-wi)

    of_re, of_im = w_out(0, pw_re[1:, 0], pw_im[1:, 0])
    ob_re, ob_im = w_out(1, pw_re[:0:-1, 1], pw_im[:0:-1, 1])
    zero = jnp.zeros_like(of_re)
    wout = jnp.concatenate([of_re, zero, of_im, zero, zero, ob_re, zero, ob_im], axis=1)

    mt = jnp.transpose(m_intra, (0, 2, 1))
    win = jnp.concatenate(win_re + win_im, axis=-1)
    woutt = jnp.transpose(wout, (0, 2, 1))
    lanes = lambda p: jnp.concatenate([p[T, 0], p[T, 1]], axis=-1)
    nblk = G // S5_GROUPS_PER_STEP
    blocked = lambda w: w.reshape((nblk, S5_GROUPS_PER_STEP) + w.shape[1:])
    return (blocked(mt.astype(BF16)), blocked(win.astype(BF16)), blocked(woutt.astype(BF16)),
            blocked(lanes(pw_re)), blocked(lanes(pw_im)))


def _s5_scan(u, ops):
    B, L, D = u.shape
    T, P, ng = S5_CHUNK, SSM_STATE, S5_GROUPS_PER_STEP
    C = L // T
    TI = T * SSM_GROUP
    lanes = ng * SSM_GROUP
    rows = ng * (C + S5_PITCH_PAD)
    seq = pl.BlockSpec((None, L, lanes), lambda b, j: (b, 0, j))
    op = lambda r, c: pl.BlockSpec((None, ng, r, c), lambda b, j: (j, 0, 0, 0))
    coef = pl.BlockSpec((None, ng, 2 * P), lambda b, j: (j, 0, 0))
    return pl.pallas_call(
        _s5_kernel,
        grid=(B, D // lanes),
        in_specs=[seq, op(TI, TI), op(TI, TI), op(TI, 2 * TI), coef, coef],
        out_specs=seq,
        out_shape=jax.ShapeDtypeStruct((B, L, D), F32),
        scratch_shapes=[pltpu.VMEM((ng, TI, C), BF16), pltpu.VMEM((ng, TI, C), F32),
                        pltpu.VMEM((rows, 2 * P), F32), pltpu.VMEM((rows, 2 * P), F32),
                        pltpu.VMEM((4, rows, 2 * P), F32)],
        compiler_params=_cparams("parallel", "parallel"),
    )(u, *ops)


def _na_kernel(q_ref, k_ref, v_ref, g_ref, bias_ref, o_ref):
    W, kr, dh = GRID_W, NA_ROWS, NA_HEAD_DIM
    n_rows = k_ref.shape[0] // W
    per_step = q_ref.shape[0] // W
    j = pl.program_id(2)
    first = lax.broadcasted_iota(jnp.int32, (W, 2 * dh), 1) < dh

    def scores(i):
        r = j * per_step + i
        rs = jnp.clip(r - kr // 2, 0, n_rows - kr)
        var = rs - r + (NA_ROWS - 1)
        keys = pl.ds(pl.multiple_of(rs * W, W), kr * W)
        q2 = q_ref[i * W:(i + 1) * W, :]
        zero = jnp.zeros_like(q2)
        qm = jnp.concatenate([jnp.where(first, q2, zero), jnp.where(first, zero, q2)], axis=0)
        s = lax.dot_general(qm, k_ref[keys, :], (((1,), (1,)), ((), ())), preferred_element_type=F32)
        bias = jnp.concatenate([bias_ref[var + 2 * c] for c in range(kr // 2)], axis=1)
        return s + bias, keys

    def attend(i, s, keys):
        p = jnp.exp2(s - jnp.max(s, axis=-1, keepdims=True))
        den = jnp.sum(p, axis=-1, keepdims=True)
        o2 = jnp.dot(p.astype(BF16), v_ref[keys, :], preferred_element_type=F32) / den
        o = jnp.where(first, o2[:W], o2[W:])
        qrow = slice(i * W, (i + 1) * W)
        o_ref[qrow, :] = (o * _silu(g_ref[qrow, :])).astype(BF16)

    pending = [scores(i) for i in range(NA_SCORE_LEAD)]
    for i in range(per_step):
        if i + NA_SCORE_LEAD < per_step:
            pending.append(scores(i + NA_SCORE_LEAD))
        attend(i, *pending.pop(0))


def _na_bias_table(rel_bias):
    W, R = GRID_W, NA_ROWS
    c_idx = jnp.arange(W)
    col_start = jnp.clip(c_idx - NA_COLS // 2, 0, W - NA_COLS)
    col_valid = ((c_idx[None, :] >= col_start[:, None])
                 & (c_idx[None, :] < col_start[:, None] + NA_COLS))
    offset = c_idx[None, None, :] - c_idx[None, :, None] + NA_COLS - 1
    onehot = (jnp.arange(2 * NA_COLS - 1)[:, None, None] == offset).astype(F32)
    col_bias = jnp.einsum('hdm,mqk->dhqk', rel_bias.astype(F32) * LOG2E, onehot,
                          precision=lax.Precision.HIGHEST)
    col_bias = jnp.where(col_valid[None, None], col_bias, NEG_INF)
    rows = col_bias.reshape(2 * R - 1, NA_HEADS // 2, 2 * W, W)
    return jnp.concatenate([rows[:-1], rows[1:]], axis=-1)


def _neighbourhood_attention(q, k, v, g, bias_tab):
    B, L, D = q.shape
    rows = L // GRID_W
    per_step = min(NA_ROWS_PER_STEP, rows)
    assert rows >= NA_ROWS and rows % per_step == 0
    tq = per_step * GRID_W
    lanes = 2 * NA_HEAD_DIM
    blk = pl.BlockSpec((None, tq, lanes), lambda b, h, j: (b, j, h))
    seq = pl.BlockSpec((None, L, lanes), lambda b, h, j: (b, 0, h))
    return pl.pallas_call(
        _na_kernel,
        grid=(B, NA_HEADS // 2, rows // per_step),
        in_specs=[blk, seq, seq, blk,
                  pl.BlockSpec((2 * NA_ROWS - 2, None, 2 * GRID_W, 2 * GRID_W),
                               lambda b, h, j: (0, h, 0, 0))],
        out_specs=blk,
        out_shape=jax.ShapeDtypeStruct((B, L, D), BF16),
        compiler_params=_cparams("parallel", "parallel", "arbitrary"),
    )(q, k, v, g, bias_tab)


def _rotary_tables(L):
    dh = RET_HEAD_DIM
    inv = ROPE_BASE ** (-jnp.arange(0, dh, 2, dtype=F32) / dh)
    ang = jnp.arange(L, dtype=F32)[:, None] * inv[None, :]
    cos, sin = jnp.cos(ang), jnp.sin(ang)
    return jnp.concatenate([cos, cos], axis=1), jnp.concatenate([-sin, sin], axis=1)


def _trunk(x, mods, p):
    B, L, D = x.shape
    mods = mods.reshape(DEPTH, B, 3, D)
    cos2, sin2 = p["rope"][L]
    kind =lambda i: "even" if i % 2 == 0 else "odd"

    def open_args(i):
        g_pre = p["norm_pre"][i].reshape(1, D)
        if kind(i) == "even":
            return (mods[i], g_pre, p["w_in_ab"][i // 2], cos2, sin2)
        return (mods[i], g_pre, p["w_in_c"][i // 2])

    opened = _stage(B, L, None, (x,), kind(0), open_args(0))
    for i in range(DEPTH):
        j = i // 2
        g_post = p["norm_post"][i].reshape(1, D)
        if kind(i) == "even":
            q, k, v, ga, u, gb = opened
            oa = _retention(q, k, v, ga)
            ys = _s5_scan(u, p["s5_ops"][j])
            close_args = (oa, ys, u, gb, x, mods[i], g_post, p["ssm_d"][j].reshape(1, D_SSM),
                          p["ssm_w_glu"][j], p["w_out_ab"][j])
        else:
            q, k, v, g = opened
            o = _neighbourhood_attention(q, k, v, g, p["na_bias"][j])
            close_args = (o, x, mods[i], g_post, p["w_out_c"][j])
        last = i + 1 == DEPTH
        res = _stage(B, L, kind(i), close_args, None if last else kind(i + 1),
                     () if last else open_args(i + 1))
        x, opened = res[0], res[1:]
    return x


def kernel(x_prompt, x_sample, c_prompt, c_sample, norm_pre, norm_post, w_mod, b_mod, w_in_ab, w_out_ab, ssm_a_re, ssm_a_im, ssm_log_step, ssm_b_re, ssm_b_im, ssm_c_re, ssm_c_im, ssm_d, ssm_w_glu, w_in_c, w_out_c, na_rel_bias):
    n_even, n_odd = w_in_ab.shape[0], w_in_c.shape[0]
    p = {
        "norm_pre": norm_pre, "norm_post": norm_post, "w_mod": w_mod, "b_mod": b_mod,
        "w_in_ab": w_in_ab.astype(BF16), "w_out_ab": w_out_ab.astype(BF16),
        "ssm_d": ssm_d, "ssm_w_glu": ssm_w_glu.astype(BF16),
        "w_in_c": w_in_c.astype(BF16), "w_out_c": w_out_c.astype(BF16),
        "s5_ops": [_s5_operators(ssm_a_re[j], ssm_a_im[j], ssm_log_step[j], ssm_b_re[j], ssm_b_im[j],
                                 ssm_c_re[j], ssm_c_im[j]) for j in range(n_even)],
        "na_bias": [_na_bias_table(na_rel_bias[j]) for j in range(n_odd)],
        "rope": {L: _rotary_tables(L) for L in {x_prompt.shape[1], x_sample.shape[1]}},
    }
    n_prompt = c_prompt.shape[0]
    mods = _modulation(jnp.concatenate([c_prompt, c_sample], axis=0), w_mod, b_mod)
    return _trunk(x_prompt, mods[:, :n_prompt], p), _trunk(x_sample, mods[:, n_prompt:], p)
```
